```python
import jax, jax.numpy as jnp
from jax import lax
import numpy as np

D_MODEL = 1024
BATCH = 2
SEQ = 8192
DEPTH = 2

CHUNK = 64
N_A_LAYERS = max(1, DEPTH // 2)
N_B_LAYERS = DEPTH - N_A_LAYERS
A_HEADS = 16
A_HEAD_DIM = D_MODEL // A_HEADS
A_LEFT_CHUNKS = 8
A_BAND = (A_LEFT_CHUNKS + 1) * CHUNK
A_MAX_REL = 2 * CHUNK
B_HEADS = 16
B_NOPE_DIM = 64
B_ROPE_DIM = 32
B_V_DIM = 64
B_Q_LORA = 384
B_KV_LORA = 256
ROPE_THETA = 10000.0
Q_BLOCK = 128
FFN_DIM = 2816
CONV_WIDTH = 3
NORM_EPS = 1e-6
NEG_INF = -1e30

kernel_name = "yoco_chunkrel_mla_convffn_adaln"


def rms_norm(x, g):
    xf = x.astype(jnp.float32)
    y = xf * lax.rsqrt(jnp.mean(xf * xf, axis=-1, keepdims=True) + NORM_EPS)
    return (y * g.astype(jnp.float32)).astype(x.dtype)


def modulate(x, shift, scale):
    return x * (1.0 + scale[:, None, :]) + shift[:, None, :]


def rope_tables(positions):
    half = B_ROPE_DIM // 2
    inv_freq = jnp.power(jnp.float32(ROPE_THETA),
                         -jnp.arange(half, dtype=jnp.float32) * (2.0 / B_ROPE_DIM))
    ang = positions.astype(jnp.float32)[..., None] * inv_freq
    return jnp.cos(ang), jnp.sin(ang)


def apply_rope(t, cos, sin):
    half = t.shape[-1] // 2
    tf = t.astype(jnp.float32)
    t1, t2 = tf[..., :half], tf[..., half:]
    return jnp.concatenate([t1 * cos - t2 * sin, t2 * cos + t1 * sin], axis=-1).astype(t.dtype)


def chunk_rel_attention(hn, wqkv, wo, rel_bias):
    B, S, _ = hn.shape
    nc = S // CHUNK
    qkv = (hn @ wqkv).reshape(B, S, 3, A_HEADS, A_HEAD_DIM)
    q, k, v = qkv[:, :, 0], qkv[:, :, 1], qkv[:, :, 2]
    pad = A_LEFT_CHUNKS * CHUNK
    kp = jnp.pad(k, ((0, 0), (pad, 0), (0, 0), (0, 0)))
    vp = jnp.pad(v, ((0, 0), (pad, 0), (0, 0), (0, 0)))
    qi = jnp.arange(CHUNK)[:, None]
    kj = jnp.arange(A_BAND)[None, :]
    rel = jnp.clip(qi + pad - kj, -A_MAX_REL, A_MAX_REL) + A_MAX_REL
    bias = rel_bias.astype(jnp.float32)[:, rel]
    scale = A_HEAD_DIM ** -0.5
    band_idx = jnp.arange(A_BAND)

    def one_chunk(n):
        start = n * CHUNK
        qc = lax.dynamic_slice_in_dim(q, start, CHUNK, axis=1)
        kc = lax.dynamic_slice_in_dim(kp, start, A_BAND, axis=1)
        vc = lax.dynamic_slice_in_dim(vp, start, A_BAND, axis=1)
        s = jnp.einsum('bqhd,bkhd->bhqk', qc, kc).astype(jnp.float32) * scale + bias[None]
        valid = (start - pad + band_idx) >= 0
        s = jnp.where(valid[None, None, None, :], s, NEG_INF)
        p = jax.nn.softmax(s, axis=-1).astype(vc.dtype)
        return jnp.einsum('bhqk,bkhd->bqhd', p, vc)

    o = lax.map(one_chunk, jnp.arange(nc))
    o = jnp.moveaxis(o, 0, 1).reshape(B, S, A_HEADS * A_HEAD_DIM)
    return o @ wo


def shared_kv(h, c_act, kv_mod_w, kv_mod_b, kv_norm_g, wdkv, kv_lat_norm_g, wuk, wuv, wkr, cos, sin):
    B, S, _ = h.shape
    shift, scale = jnp.split(c_act @ kv_mod_w + kv_mod_b, 2, axis=-1)
    hn = modulate(rms_norm(h, kv_norm_g), shift, scale)
    ckv = rms_norm(hn @ wdkv, kv_lat_norm_g)
    k_nope = (ckv @ wuk).reshape(B, S, B_HEADS, B_NOPE_DIM)
    v = (ckv @ wuv).reshape(B, S, B_HEADS, B_V_DIM)
    k_rope = apply_rope(hn @ wkr, cos, sin)
    return k_nope, k_rope, v


def mla_attention(hn, wdq, q_norm_g, wuq, wqr, wo, k_nope, k_rope, v, cos, sin):
    B, S, _ = hn.shape
    cq = rms_norm(hn @ wdq, q_norm_g)
    q_nope = (cq @ wuq).reshape(B, S, B_HEADS, B_NOPE_DIM)
    q_rope = apply_rope((cq @ wqr).reshape(B, S, B_HEADS, B_ROPE_DIM),
                        cos[:, :, None, :], sin[:, :, None, :])
    key_chunk = jnp.arange(S) // CHUNK
    scale = (B_NOPE_DIM + B_ROPE_DIM) ** -0.5

    def one_block(i):
        start = i * Q_BLOCK
        qn = lax.dynamic_slice_in_dim(q_nope, start, Q_BLOCK, axis=1)
        qr = lax.dynamic_slice_in_dim(q_rope, start, Q_BLOCK, axis=1)
        s = (jnp.einsum('bqhd,bkhd->bhqk', qn, k_nope).astype(jnp.float32)
             + jnp.einsum('bqhd,bkd->bhqk', qr, k_rope).astype(jnp.float32)) * scale
        q_chunk = (start + jnp.arange(Q_BLOCK)) // CHUNK
        mask = key_chunk[None, :] <= q_chunk[:, None]
        s = jnp.where(mask[None, None], s, NEG_INF)
        p = jax.nn.softmax(s, axis=-1).astype(v.dtype)
        return jnp.einsum('bhqk,bkhd->bqhd', p, v)

    o = lax.map(one_block, jnp.arange(S // Q_BLOCK))
    o = jnp.moveaxis(o, 0, 1).reshape(B, S, B_HEADS * B_V_DIM)
    return o @ wo


def conv_ffn(hn, win, conv_w, conv_b, wout):
    S = hn.shape[1]
    u = hn @ win
    up = jnp.pad(u, ((0, 0), (CONV_WIDTH - 1, 0), (0, 0)))
    y = conv_b
    for tap in range(CONV_WIDTH):
        y = y + up[:, tap:tap + S] * conv_w[tap]
    gate, val = jnp.split(y, 2, axis=-1)
    return (jax.nn.silu(gate) * val) @ wout


def setup_inputs(seed: int = 0) -> dict:
    key = jax.random.key(seed)
    ks = iter(jax.random.split(key, 40))
    D = D_MODEL

    def nrm(shape, scale):
        return jax.random.normal(next(ks), shape, jnp.float32) * scale

    def gain(shape):
        return 1.0 + nrm(shape, 0.02)

    x = nrm((BATCH, SEQ, D), 1.0)
    c = nrm((BATCH, D), 1.0)
    positions = (jnp.arange(SEQ, dtype=jnp.int32)[None, :]
                 + jax.random.randint(next(ks), (BATCH, 1), 0, 4096, dtype=jnp.int32))
    return {
        "x": x,
        "c": c,
        "positions": positions,
        "mod_w": nrm((DEPTH, D, 6 * D), 0.5 * D ** -0.5),
        "mod_b": nrm((DEPTH, 6 * D), 0.02),
        "norm1_g": gain((DEPTH, D)),
        "norm2_g": gain((DEPTH, D)),
        "a_wqkv": nrm((N_A_LAYERS, D, 3 * A_HEADS * A_HEAD_DIM), D ** -0.5),
        "a_wo": nrm((N_A_LAYERS, A_HEADS * A_HEAD_DIM, D), (A_HEADS * A_HEAD_DIM) ** -0.5),
        "a_rel_bias": nrm((N_A_LAYERS, A_HEADS, 2 * A_MAX_REL + 1), 0.5),
        "kv_mod_w": nrm((D, 2 * D), 0.5 * D ** -0.5),
        "kv_mod_b": nrm((2 * D,), 0.02),
        "kv_norm_g": gain((D,)),
        "b_wdkv": nrm((D, B_KV_LORA), D ** -0.5),
        "b_kv_lat_norm_g": gain((B_KV_LORA,)),
        "b_wuk": nrm((B_KV_LORA, B_HEADS * B_NOPE_DIM), B_KV_LORA ** -0.5),
        "b_wuv": nrm((B_KV_LORA, B_HEADS * B_V_DIM), B_KV_LORA ** -0.5),
        "b_wkr": nrm((D, B_ROPE_DIM), D ** -0.5),
        "b_wdq": nrm((N_B_LAYERS, D, B_Q_LORA), D ** -0.5),
        "b_q_norm_g": gain((N_B_LAYERS, B_Q_LORA)),
        "b_wuq": nrm((N_B_LAYERS, B_Q_LORA, B_HEADS * B_NOPE_DIM), B_Q_LORA ** -0.5),
        "b_wqr": nrm((N_B_LAYERS, B_Q_LORA, B_HEADS * B_ROPE_DIM), B_Q_LORA ** -0.5),
        "b_wo": nrm((N_B_LAYERS, B_HEADS * B_V_DIM, D), (B_HEADS * B_V_DIM) ** -0.5),
        "f_win": nrm((DEPTH, D, 2 * FFN_DIM), D ** -0.5),
        "f_conv_w": nrm((DEPTH, CONV_WIDTH, 2 * FFN_DIM), CONV_WIDTH ** -0.5),
        "f_conv_b": nrm((DEPTH, 2 * FFN_DIM), 0.02),
        "f_wout": nrm((DEPTH, FFN_DIM, D), FFN_DIM ** -0.5),
        "final_g": gain((D,)),
    }


def reference(x, c, positions, mod_w, mod_b, norm1_g, norm2_g, a_wqkv, a_wo, a_rel_bias,
              kv_mod_w, kv_mod_b, kv_norm_g, b_wdkv, b_kv_lat_norm_g, b_wuk, b_wuv, b_wkr,
              b_wdq, b_q_norm_g, b_wuq, b_wqr, b_wo, f_win, f_conv_w, f_conv_b, f_wout, final_g):
    c_act = jax.nn.silu(c)
    cos, sin = rope_tables(positions)
    h = x
    kv = None
    for l in range(DEPTH):
        mod = c_act @ mod_w[l] + mod_b[l]
        sh1, sc1, g1, sh2, sc2, g2 = jnp.split(mod, 6, axis=-1)
        hn = modulate(rms_norm(h, norm1_g[l]), sh1, sc1)
        if l < N_A_LAYERS:
            mix = chunk_rel_attention(hn, a_wqkv[l], a_wo[l], a_rel_bias[l])
        else:
            j = l - N_A_LAYERS
            mix = mla_attention(hn, b_wdq[j], b_q_norm_g[j], b_wuq[j], b_wqr[j], b_wo[j],
                                kv[0], kv[1], kv[2], cos, sin)
        h = h + g1[:, None, :] * mix
        hn = modulate(rms_norm(h, norm2_g[l]), sh2, sc2)
        h = h + g2[:, None, :] * conv_ffn(hn, f_win[l], f_conv_w[l], f_conv_b[l], f_wout[l])
        if l == N_A_LAYERS - 1:
            kv = shared_kv(h, c_act, kv_mod_w, kv_mod_b, kv_norm_g, b_wdkv, b_kv_lat_norm_g,
                           b_wuk, b_wuv, b_wkr, cos, sin)
    return rms_norm(h, final_g)
```

```python
import functools

import jax
import jax.numpy as jnp
from jax import lax
from jax.experimental import pallas as pl
from jax.experimental.pallas import tpu as pltpu

F32 = jnp.float32
BF16 = jnp.bfloat16

CHUNK = 64
A_HEADS = 16
A_HEAD_DIM = 64
A_LEFT_CHUNKS = 8
A_MAX_REL = 2 * CHUNK
B_HEADS = 16
B_NOPE_DIM = 64
B_ROPE_DIM = 32
ROPE_THETA = 10000.0
CONV_WIDTH = 3
NORM_EPS = 1e-6
NEG_INF = -1e30

LANES = 128
MXU_COLS = 256
SUBLANES_F32 = 8
SUBLANES_BF16 = 16
VMEM_BYTES = 64 * 1024 * 1024

HEAD_PAIRS = 8
HALO = SUBLANES_BF16
ROW_TILE = 512
A_QTILE = 512
A_SUB = 2 * CHUNK
A_WIN = A_LEFT_CHUNKS * CHUNK + A_SUB
MLA_TILE = 512
FFN_CHUNK = MXU_COLS


def _params(vmem_mib, n_axes):
    return pltpu.CompilerParams(
        dimension_semantics=("arbitrary",) * n_axes,
        vmem_limit_bytes=vmem_mib * 1024 * 1024)


def _resident(shape):
    zeros = (0,) * len(shape)
    return pl.BlockSpec(shape, lambda *_: zeros, pipeline_mode=pl.Buffered(1))


def _rms(x, g):
    return x * lax.rsqrt(jnp.mean(x * x, axis=-1, keepdims=True) + NORM_EPS) * g


def _norm_mod(x, g, shift, scale):
    return _rms(x, g) * (1.0 + scale) + shift


def _silu(x):
    return x * (1.0 / (1.0 + jnp.exp(-x)))


def _mod_body(c_ref, w_ref, b_ref, o_ref):
    ca = _silu(c_ref[...]).astype(BF16)
    o_ref[0] = jnp.dot(ca, w_ref[0].astype(BF16), preferred_element_type=F32) + b_ref[0]


def _mod_call(c_pad, w, b, tn=1024):
    L, D, N = w.shape
    return pl.pallas_call(
        _mod_body,
        grid=(L, N // tn),
        in_specs=[pl.BlockSpec((SUBLANES_F32, D), lambda l, n: (0, 0)),
                  pl.BlockSpec((1, D, tn), lambda l, n: (l, 0, n)),
                  pl.BlockSpec((1, 1, tn), lambda l, n: (l, 0, n))],
        out_specs=pl.BlockSpec((1, SUBLANES_F32, tn), lambda l, n: (l, 0, n)),
        out_shape=jax.ShapeDtypeStruct((L, SUBLANES_F32, N), F32),
        compiler_params=_params(32, 2),
        name="mod",
    )(c_pad, w, b)


def _qkv_body(x_ref, vec_ref, w_ref, o_ref, hn_ref):
    vec = vec_ref[0]
    hn_ref[...] = _norm_mod(x_ref[0], vec[0:1], vec[1:2], vec[2:3]).astype(BF16)
    n_groups = o_ref.shape[1]
    for c in range(n_groups // 2):
        y = jnp.dot(hn_ref[...], w_ref[:, c * MXU_COLS:(c + 1) * MXU_COLS],
                    preferred_element_type=F32)
        o_ref[0, 2 * c] = y[:, :LANES].astype(BF16)
        o_ref[0, 2 * c + 1] = y[:, LANES:].astype(BF16)


def _qkv_call(x, vec, w):
    B, S, D = x.shape
    N = w.shape[1]
    tm = ROW_TILE
    return pl.pallas_call(
        _qkv_body,
        grid=(B, S // tm),
        in_specs=[pl.BlockSpec((1, tm, D), lambda b, i: (b, i, 0)),
                  pl.BlockSpec((1, SUBLANES_F32, D), lambda b, i: (b, 0, 0)),
                  _resident((D, N))],
        out_specs=pl.BlockSpec((1, N // LANES, tm, LANES), lambda b, i: (b, 0, i, 0)),
        out_shape=jax.ShapeDtypeStruct((B, N // LANES, S, LANES), BF16),
        scratch_shapes=[pltpu.VMEM((tm, D), BF16)],
        compiler_params=_params(40, 2),
        name="qkv",
    )(x, vec, w)


def _chunk_attn_body(q_ref, kp_ref, kc_ref, vp_ref, vc_ref, bias_ref, o_ref):
    i = pl.program_id(2)
    even = lax.broadcasted_iota(jnp.int32, (1, LANES), 1) < A_HEAD_DIM
    col = lax.broadcasted_iota(jnp.int32, (1, A_WIN), 1)
    prev_rows = A_WIN - A_SUB
    for j in range(A_QTILE // A_SUB):
        lo = j * A_SUB
        q2 = q_ref[0, 0, lo:lo + A_SUB, :]
        k2 = jnp.concatenate([kp_ref[0, 0, lo:, :], kc_ref[0, 0, :lo + A_SUB, :]], axis=0)
        v2 = jnp.concatenate([vp_ref[0, 0, lo:, :], vc_ref[0, 0, :lo + A_SUB, :]], axis=0)
        valid = (col + (i * A_QTILE + lo - prev_rows)) >= 0
        outs = []
        for hh in range(2):
            qh = jnp.where(even if hh == 0 else jnp.logical_not(even), q2, jnp.zeros_like(q2))
            s = lax.dot_general(qh, k2, (((1,), (1,)), ((), ())), preferred_element_type=F32)
            s = jnp.where(valid, s + bias_ref[hh], NEG_INF)
            m = jnp.max(s, axis=-1, keepdims=True)
            p = jnp.exp(s - m)
            l = jnp.sum(p, axis=-1, keepdims=True)
            pv = jnp.dot(p.astype(BF16), v2, preferred_element_type=F32)
            outs.append(pv / l)
        o_ref[0, 0, lo:lo + A_SUB, :] = jnp.where(even, outs[0], outs[1]).astype(BF16)


def _chunk_attn_call(qkv, bias):
    B, _, S, _ = qkv.shape
    tq = A_QTILE
    blk = (1, 1, tq, LANES)
    prev = lambda g: (lambda p, b, i: (b, g + p, jnp.maximum(i - 1, 0), 0))
    cur = lambda g: (lambda p, b, i: (b, g + p, i, 0))
    return pl.pallas_call(
        _chunk_attn_body,
        grid=(HEAD_PAIRS, B, S // tq),
        in_specs=[pl.BlockSpec(blk, cur(0)),
                  pl.BlockSpec(blk, prev(HEAD_PAIRS)),
                  pl.BlockSpec(blk, cur(HEAD_PAIRS)),
                  pl.BlockSpec(blk, prev(2 * HEAD_PAIRS)),
                  pl.BlockSpec(blk, cur(2 * HEAD_PAIRS)),
                  pl.BlockSpec((2, A_SUB, A_WIN), lambda p, b, i: (p, 0, 0))],
        out_specs=pl.BlockSpec(blk, lambda p, b, i: (b, p, i, 0)),
        out_shape=jax.ShapeDtypeStruct((B, HEAD_PAIRS, S, LANES), BF16),
        compiler_params=_params(32, 3),
        name="chunk_attn",
    )(qkv, qkv, qkv, qkv, qkv, bias)


def _ffn_body(o_ref, oh_ref, h_ref, hh_ref, vec_ref, wo_ref, win_ref, cw_ref, wout_ref,
              out_ref, hn_ref, h1_ref, acc_ref, *, final_norm):
    i = pl.program_id(1)
    tm = h_ref.shape[1]
    vec = vec_ref[0]
    g1, n2g, sh2, sc2, g2, fg = (vec[r:r + 1] for r in range(6))

    o_cat = jnp.concatenate(
        [jnp.concatenate([oh_ref[0, p] for p in range(HEAD_PAIRS)], axis=1),
         jnp.concatenate([o_ref[0, p] for p in range(HEAD_PAIRS)], axis=1)], axis=0)
    h_cat = jnp.concatenate([hh_ref[0], h_ref[0]], axis=0)
    h1 = h_cat + g1 * jnp.dot(o_cat, wo_ref[...], preferred_element_type=F32)
    hn = _norm_mod(h1, n2g, sh2, sc2)
    row = lax.broadcasted_iota(jnp.int32, (tm + HALO, 1), 0)
    hn = jnp.where(jnp.logical_and(row < HALO, i == 0), 0.0, hn)
    hn_ref[...] = hn.astype(BF16)
    h1_ref[...] = h1[HALO:]

    n_chunks = wout_ref.shape[0] // FFN_CHUNK
    for f in range(n_chunks):
        cols = slice(2 * f * FFN_CHUNK, 2 * (f + 1) * FFN_CHUNK)
        u = jnp.dot(hn_ref[...], win_ref[:, cols], preferred_element_type=F32)
        y = cw_ref[3:4, cols] + pltpu.roll(u, 2, 0) * cw_ref[0:1, cols]
        y = y + pltpu.roll(u, 1, 0) * cw_ref[1:2, cols]
        y = (y + u * cw_ref[2:3, cols])[HALO:]
        act = (_silu(y[:, :FFN_CHUNK]) * y[:, FFN_CHUNK:]).astype(BF16)
        part = jnp.dot(act, wout_ref[f * FFN_CHUNK:(f + 1) * FFN_CHUNK, :],
                       preferred_element_type=F32)
        if f == 0:
            acc_ref[...] = part
        else:
            acc_ref[...] += part

    h2 = h1_ref[...] + g2 * acc_ref[...]
    out_ref[0] = _rms(h2, fg) if final_norm else h2


def _ffn_call(o, h, vec, wo, win, cw, wout, final_norm):
    B, S, D = h.shape
    tm = ROW_TILE
    halo_blocks = tm // HALO
    halo_idx = lambda i: jnp.maximum(i * halo_blocks - 1, 0)
    return pl.pallas_call(
        functools.partial(_ffn_body, final_norm=final_norm),
        grid=(B, S // tm),
        in_specs=[pl.BlockSpec((1, HEAD_PAIRS, tm, LANES), lambda b, i: (b, 0, i, 0)),
                  pl.BlockSpec((1, HEAD_PAIRS, HALO, LANES), lambda b, i: (b, 0, halo_idx(i), 0)),
                  pl.BlockSpec((1, tm, D), lambda b, i: (b, i, 0)),
                  pl.BlockSpec((1, HALO, D), lambda b, i: (b, halo_idx(i), 0)),
                  pl.BlockSpec((1, SUBLANES_F32, D), lambda b, i: (b, 0, 0)),
                  _resident(wo.shape), _resident(win.shape), _resident(cw.shape),
                  _resident(wout.shape)],
        out_specs=pl.BlockSpec((1, tm, D), lambda b, i: (b, i, 0)),
        out_shape=jax.ShapeDtypeStruct((B, S, D), F32),
        scratch_shapes=[pltpu.VMEM((tm + HALO, D), BF16),
                        pltpu.VMEM((tm, D), F32),
                        pltpu.VMEM((tm, D), F32)],
        compiler_params=_params(56, 2),
        name="ffn_final" if final_norm else "ffn",
    )(o, o, h, h, vec, wo, win, cw, wout)


def _kv_body(h_ref, vec_ref, lg_ref, w1_ref, wu_ref, cos_ref, sin_ref, k_ref, v_ref):
    vec = vec_ref[0]
    hn = _norm_mod(h_ref[0], vec[0:1], vec[1:2], vec[2:3]).astype(BF16)
    t = jnp.dot(hn, w1_ref[...], preferred_element_type=F32)
    lat = lg_ref.shape[1]
    ckv = _rms(t[:, :lat], lg_ref[0:1]).astype(BF16)
    kr = (t[:, lat:lat + LANES] * cos_ref[0] + t[:, lat + LANES:] * sin_ref[0]).astype(BF16)
    kv = jnp.dot(ckv, wu_ref[...], preferred_element_type=F32)
    half = HEAD_PAIRS * LANES
    for p in range(HEAD_PAIRS):
        k_ref[0, p, :, :LANES] = kv[:, p * LANES:(p + 1) * LANES].astype(BF16)
        k_ref[0, p, :, LANES:] = kr
        v_ref[0, p] = kv[:, half + p * LANES:half + (p + 1) * LANES].astype(BF16)


def _kv_call(h, vec, lat_g, w1, wu, cos, sin):
    B, S, D = h.shape
    tm = ROW_TILE
    row = lambda b, i: (b, i, 0)
    return pl.pallas_call(
        _kv_body,
        grid=(B, S // tm),
        in_specs=[pl.BlockSpec((1, tm, D), row),
                  pl.BlockSpec((1, SUBLANES_F32, D), lambda b, i: (b, 0, 0)),
                  _resident(lat_g.shape), _resident(w1.shape), _resident(wu.shape),
                  pl.BlockSpec((1, tm, LANES), row), pl.BlockSpec((1, tm, LANES), row)],
        out_specs=[pl.BlockSpec((1, HEAD_PAIRS, tm, 2 * LANES), lambda b, i: (b, 0, i, 0)),
                   pl.BlockSpec((1, HEAD_PAIRS, tm, LANES), lambda b, i: (b, 0, i, 0))],
        out_shape=[jax.ShapeDtypeStruct((B, HEAD_PAIRS, S, 2 * LANES), BF16),
                   jax.ShapeDtypeStruct((B, HEAD_PAIRS, S, LANES), BF16)],
        compiler_params=_params(40, 2),
        name="shared_kv",
    )(h, vec, lat_g, w1, wu, cos, sin)


def _q_body(h_ref, vec_ref, qg_ref, wdq_ref, wq_ref, cos_ref, sin_ref, q_ref):
    vec = vec_ref[0]
    hn = _norm_mod(h_ref[0], vec[0:1], vec[1:2], vec[2:3]).astype(BF16)
    cq = _rms(jnp.dot(hn, wdq_ref[...], preferred_element_type=F32), qg_ref[0:1]).astype(BF16)
    t = jnp.dot(cq, wq_ref[...], preferred_element_type=F32)
    width = HEAD_PAIRS * LANES
    cos = cos_ref[0]
    sin = sin_ref[0]
    for p in range(HEAD_PAIRS):
        g = slice(p * LANES, (p + 1) * LANES)
        q_ref[0, p, :, :LANES] = t[:, g].astype(BF16)
        rope = t[:, width:2 * width][:, g] * cos + t[:, 2 * width:][:, g] * sin
        q_ref[0, p, :, LANES:] = rope.astype(BF16)


def _q_call(h, vec, q_g, wdq, wq, cos, sin):
    B, S, D = h.shape
    tm = ROW_TILE
    row = lambda b, i: (b, i, 0)
    return pl.pallas_call(
        _q_body,
        grid=(B, S // tm),
        in_specs=[pl.BlockSpec((1, tm, D), row),
                  pl.BlockSpec((1, SUBLANES_F32, D), lambda b, i: (b, 0, 0)),
                  _resident(q_g.shape), _resident(wdq.shape), _resident(wq.shape),
                  pl.BlockSpec((1, tm, LANES), row), pl.BlockSpec((1, tm, LANES), row)],
        out_specs=pl.BlockSpec((1, HEAD_PAIRS, tm, 2 * LANES), lambda b, i: (b, 0, i, 0)),
        out_shape=jax.ShapeDtypeStruct((B, HEAD_PAIRS, S, 2 * LANES), BF16),
        compiler_params=_params(40, 2),
        name="mla_q",
    )(h, vec, q_g, wdq, wq, cos, sin)


def _mla_body(q_ref, k_ref, v_ref, o_ref, m_ref, l_ref, acc_ref, *, scale):
    qi = pl.program_id(2)
    t = q_ref.shape[2]
    q2 = q_ref[0, 0]
    lane = lax.broadcasted_iota(jnp.int32, (1, 2 * LANES), 1)
    in_even = jnp.logical_or(lane < B_NOPE_DIM,
                             jnp.logical_and(lane >= LANES, lane < LANES + B_ROPE_DIM))
    in_odd = jnp.logical_and(jnp.logical_not(in_even), lane < LANES + 2 * B_ROPE_DIM)
    qs = [jnp.where(in_even, q2, jnp.zeros_like(q2)), jnp.where(in_odd, q2, jnp.zeros_like(q2))]

    m_ref[...] = jnp.full(m_ref.shape, NEG_INF, F32)
    l_ref[...] = jnp.zeros(l_ref.shape, F32)
    acc_ref[...] = jnp.zeros(acc_ref.shape, F32)

    def block(j, diagonal):
        rows = pl.ds(pl.multiple_of(j * t, t), t)
        k = k_ref[0, 0, rows, :]
        v = v_ref[0, 0, rows, :]
        for hh in range(2):
            s = lax.dot_general(qs[hh], k, (((1,), (1,)), ((), ())),
                                preferred_element_type=F32) * scale
            if diagonal:
                qc = lax.broadcasted_iota(jnp.int32, (t, 1), 0) // CHUNK
                kc = lax.broadcasted_iota(jnp.int32, (1, t), 1) // CHUNK
                s = jnp.where(kc <= qc, s, NEG_INF)
            m_old = m_ref[hh]
            m_new = jnp.maximum(m_old, jnp.max(s, axis=-1, keepdims=True))
            alpha = jnp.exp(m_old - m_new)
            p = jnp.exp(s - m_new)
            l_ref[hh] = alpha * l_ref[hh] + jnp.sum(p, axis=-1, keepdims=True)
            acc_ref[hh] = alpha * acc_ref[hh] + jnp.dot(p.astype(BF16), v,
                                                        preferred_element_type=F32)
            m_ref[hh] = m_new

    def off_diagonal(j, carry):
        block(j, False)
        return carry

    lax.fori_loop(0, qi, off_diagonal, 0)
    block(qi, True)

    even_out = lax.broadcasted_iota(jnp.int32, (1, LANES), 1) < B_NOPE_DIM
    o_ref[0, 0] = jnp.where(even_out, acc_ref[0] / l_ref[0], acc_ref[1] / l_ref[1]).astype(BF16)


def _mla_call(q, k, v):
    B, P, S, _ = q.shape
    t = MLA_TILE
    scale = float((B_NOPE_DIM + B_ROPE_DIM) ** -0.5)
    return pl.pallas_call(
        functools.partial(_mla_body, scale=scale),
        grid=(B, P, S // t),
        in_specs=[pl.BlockSpec((1, 1, t, 2 * LANES), lambda b, p, i: (b, p, i, 0)),
                  pl.BlockSpec((1, 1, S, 2 * LANES), lambda b, p, i: (b, p, 0, 0)),
                  pl.BlockSpec((1, 1, S, LANES), lambda b, p, i: (b, p, 0, 0))],
        out_specs=pl.BlockSpec((1, 1, t, LANES), lambda b, p, i: (b, p, i, 0)),
        out_shape=jax.ShapeDtypeStruct((B, P, S, LANES), BF16),
        scratch_shapes=[pltpu.VMEM((2, t, 1), F32),
                        pltpu.VMEM((2, t, 1), F32),
                        pltpu.VMEM((2, t, LANES), F32)],
        compiler_params=_params(40, 3),
        name="mla_attn",
    )(q, k, v)


def _vec_rows(rows, batch, d):
    full = [jnp.broadcast_to(r, (batch, d)) for r in rows]
    full += [jnp.zeros((batch, d), F32)] * (SUBLANES_F32 - len(full))
    return jnp.stack(full, axis=1)


def _pad_rows(rows, width):
    full = [r.reshape(1, width) for r in rows]
    full += [jnp.zeros((1, width), F32)] * (SUBLANES_F32 - len(full))
    return jnp.concatenate(full, axis=0)


def _pair_rope_cols(w, heads):
    kdim = w.shape[0]
    half = B_ROPE_DIM // 2
    wh = w.reshape(kdim, heads // 2, 2 * B_ROPE_DIM)
    pad = jnp.zeros((kdim, heads // 2, LANES - 2 * B_ROPE_DIM), w.dtype)
    plain = jnp.concatenate([wh, pad], axis=-1).reshape(kdim, -1)
    w4 = w.reshape(kdim, heads, 2, half)
    sw = jnp.concatenate([w4[:, :, 1], w4[:, :, 0]], axis=-1).reshape(kdim, heads // 2, 2 * B_ROPE_DIM)
    swapped = jnp.concatenate([sw, pad], axis=-1).reshape(kdim, -1)
    return plain, swapped


def _ffn_cols(a, ffn_dim):
    lead = a.shape[:-1]
    n = ffn_dim // FFN_CHUNK
    a = a.reshape(lead + (2, n, FFN_CHUNK))
    return jnp.swapaxes(a, -3, -2).reshape(lead + (2 * ffn_dim,))


def _bias_table(rel_bias):
    ql = jnp.arange(A_SUB)[:, None]
    kl = jnp.arange(A_WIN)[None, :]
    rel = jnp.clip(ql + (A_WIN - A_SUB) - kl, -A_MAX_REL, A_MAX_REL) + A_MAX_REL
    band = kl // CHUNK - ql // CHUNK
    in_band = jnp.logical_and(band >= 0, band <= A_LEFT_CHUNKS)
    return jnp.where(in_band[None], rel_bias.astype(F32)[:, rel], NEG_INF)


def kernel(x, c, positions, mod_w, mod_b, norm1_g, norm2_g, a_wqkv, a_wo, a_rel_bias, kv_mod_w, kv_mod_b, kv_norm_g, b_wdkv, b_kv_lat_norm_g, b_wuk, b_wuv, b_wkr, b_wdq, b_q_norm_g, b_wuq, b_wqr, b_wo, f_win, f_conv_w, f_conv_b, f_wout, final_g):
    B, S, D = x.shape
    depth = mod_w.shape[0]
    ffn_dim = f_wout.shape[1]

    c_pad = jnp.pad(c, ((0, SUBLANES_F32 - B), (0, 0)))
    mod = _mod_call(c_pad, mod_w, mod_b.reshape(depth, 1, 6 * D))[:, :B]
    kv_mod = _mod_call(c_pad, kv_mod_w[None], kv_mod_b.reshape(1, 1, 2 * D))[0, :B]
    mods = [[mod[l, :, k * D:(k + 1) * D] for k in range(6)] for l in range(depth)]

    half = B_ROPE_DIM // 2
    inv_freq = jnp.power(jnp.float32(ROPE_THETA),
                         -jnp.arange(half, dtype=F32) * (2.0 / B_ROPE_DIM))
    ang = positions.astype(F32)[..., None] * inv_freq
    cos, sin = jnp.cos(ang), jnp.sin(ang)
    reps = LANES // B_ROPE_DIM
    cos_t = jnp.tile(jnp.concatenate([cos, cos], axis=-1), (1, 1, reps))
    sin_t = jnp.tile(jnp.concatenate([-sin, sin], axis=-1), (1, 1, reps))

    def ffn_weights(l):
        win = _ffn_cols(f_win[l], ffn_dim).astype(BF16)
        cw = _pad_rows([_ffn_cols(f_conv_w[l, t], ffn_dim) for t in range(CONV_WIDTH)]
                       + [_ffn_cols(f_conv_b[l], ffn_dim)], 2 * ffn_dim)
        return win, cw, f_wout[l].astype(BF16)

    sh1, sc1, g1, sh2, sc2, g2 = mods[0]
    hd = A_HEADS * A_HEAD_DIM
    wqkv = jnp.concatenate([a_wqkv[0][:, :hd] * (A_HEAD_DIM ** -0.5), a_wqkv[0][:, hd:]],
                           axis=1).astype(BF16)
    qkv = _qkv_call(x, _vec_rows([norm1_g[0], sh1, sc1], B, D), wqkv)
    o = _chunk_attn_call(qkv, _bias_table(a_rel_bias[0]))
    win, cw, wout = ffn_weights(0)
    h = _ffn_call(o, x, _vec_rows([g1, norm2_g[0], sh2, sc2, g2, final_g], B, D),
                  a_wo[0].astype(BF16), win, cw, wout, final_norm=False)

    kr_plain, kr_swapped = _pair_rope_cols(jnp.concatenate([b_wkr, b_wkr], axis=1), 2)
    w1 = jnp.concatenate([b_wdkv, kr_plain, kr_swapped], axis=1).astype(BF16)
    wu = jnp.concatenate([b_wuk, b_wuv], axis=1).astype(BF16)
    k_cat, v = _kv_call(h, _vec_rows([kv_norm_g, kv_mod[:, :D], kv_mod[:, D:]], B, D),
                        _pad_rows([b_kv_lat_norm_g], b_kv_lat_norm_g.shape[0]), w1, wu,
                        cos_t, sin_t)

    sh1, sc1, g1, sh2, sc2, g2 = mods[1]
    qr_plain, qr_swapped = _pair_rope_cols(b_wqr[0], B_HEADS)
    wq = jnp.concatenate([b_wuq[0], qr_plain, qr_swapped], axis=1).astype(BF16)
    q_cat = _q_call(h, _vec_rows([norm1_g[1], sh1, sc1], B, D),
                    _pad_rows([b_q_norm_g[0]], b_q_norm_g.shape[1]),
                    b_wdq[0].astype(BF16), wq, cos_t, sin_t)
    o = _mla_call(q_cat, k_cat, v)
    win, cw, wout = ffn_weights(1)
    return _ffn_call(o, h, _vec_rows([g1, norm2_g[1], sh2, sc2, g2, final_g], B, D),
                     b_wo[0].astype(BF16), win, cw, wout, final_norm=True)
```

```python
import functools
import math

import jax
import jax.numpy as jnp
import numpy as np
from jax import lax
from jax.experimental import pallas as pl
from jax.experimental.pallas import tpu as pltpu

F32 = jnp.float32
BF16 = jnp.bfloat16

CHUNK = 64
A_HEADS = 16
A_HEAD_DIM = 64
A_LEFT_CHUNKS = 8
A_MAX_REL = 2 * CHUNK
B_HEADS = 16
B_NOPE_DIM = 64
B_ROPE_DIM = 32
B_V_DIM = 64
ROPE_THETA = 10000.0
CONV_WIDTH = 3
NORM_EPS = 1e-6
NEG_INF = -1e30

LANES = 128
MXU_COLS = 256
SUBLANES_F32 = 8
SUBLANES_BF16 = 16
VMEM_BYTES = 64 * 1024 * 1024

HEAD_PAIRS = 8
HALO = SUBLANES_BF16
ROW_TILE = 512
A_QTILE = 512
A_SUB = 2 * CHUNK
A_WIN = A_LEFT_CHUNKS * CHUNK + A_SUB
MLA_QTILE = 512
MLA_KTILE = 256
FFN_CHUNK = MXU_COLS


def _params(vmem_mib, n_axes):
    return pltpu.CompilerParams(
        dimension_semantics=("arbitrary",) * n_axes,
        vmem_limit_bytes=vmem_mib * 1024 * 1024)


def _resident(shape):
    zeros = (0,) * len(shape)
    return pl.BlockSpec(shape, lambda *_: zeros, pipeline_mode=pl.Buffered(1))


def _rms(x, g):
    return x * lax.rsqrt(jnp.mean(x * x, axis=-1, keepdims=True) + NORM_EPS) * g


def _norm_mod(x, g, shift, scale):
    return _rms(x, g) * (1.0 + scale) + shift


def _silu(x):
    return x * (1.0 / (1.0 + jnp.exp(-x)))


def _mod_body(c_ref, w_ref, b_ref, o_ref):
    ca = _silu(c_ref[...]).astype(BF16)
    o_ref[0] = jnp.dot(ca, w_ref[0].astype(BF16), preferred_element_type=F32) + b_ref[0]


def _mod_call(c_pad, w, b, tn=1024):
    L, D, N = w.shape
    return pl.pallas_call(
        _mod_body,
        grid=(L, N // tn),
        in_specs=[pl.BlockSpec((SUBLANES_F32, D), lambda l, n: (0, 0)),
                  pl.BlockSpec((1, D, tn), lambda l, n: (l, 0, n)),
                  pl.BlockSpec((1, 1, tn), lambda l, n: (l, 0, n))],
        out_specs=pl.BlockSpec((1, SUBLANES_F32, tn), lambda l, n: (l, 0, n)),
        out_shape=jax.ShapeDtypeStruct((L, SUBLANES_F32, N), F32),
        compiler_params=_params(32, 2),
        name="mod",
    )(c_pad, w, b)


def _qkv_body(x_ref, vec_ref, w_ref, o_ref, hn_ref):
    vec = vec_ref[0]
    hn_ref[...] = _norm_mod(x_ref[0], vec[0:1], vec[1:2], vec[2:3]).astype(BF16)
    n_groups = o_ref.shape[1]
    for c in range(n_groups // 2):
        y = jnp.dot(hn_ref[...], w_ref[:, c * MXU_COLS:(c + 1) * MXU_COLS],
                    preferred_element_type=F32)
        o_ref[0, 2 * c] = y[:, :LANES].astype(BF16)
        o_ref[0, 2 * c + 1] = y[:, LANES:].astype(BF16)


def _qkv_call(x, vec, w):
    B, S, D = x.shape
    N = w.shape[1]
    tm = ROW_TILE
    return pl.pallas_call(
        _qkv_body,
        grid=(B, S // tm),
        in_specs=[pl.BlockSpec((1, tm, D), lambda b, i: (b, i, 0)),
                  pl.BlockSpec((1, SUBLANES_F32, D), lambda b, i: (b, 0, 0)),
                  _resident((D, N))],
        out_specs=pl.BlockSpec((1, N // LANES, tm, LANES), lambda b, i: (b, 0, i, 0)),
        out_shape=jax.ShapeDtypeStruct((B, N // LANES, S, LANES), BF16),
        scratch_shapes=[pltpu.VMEM((tm, D), BF16)],
        compiler_params=_params(40, 2),
        name="qkv",
    )(x, vec, w)


def _chunk_attn_body(q_ref, kp_ref, kc_ref, vp_ref, vc_ref, bias_ref, o_ref):
    i = pl.program_id(2)
    even = lax.broadcasted_iota(jnp.int32, (1, LANES), 1) < A_HEAD_DIM
    col = lax.broadcasted_iota(jnp.int32, (1, A_WIN), 1)
    prev_rows = A_WIN - A_SUB
    for j in range(A_QTILE // A_SUB):
        lo = j * A_SUB
        q2 = q_ref[0, 0, lo:lo + A_SUB, :]
        k2 = jnp.concatenate([kp_ref[0, 0, lo:, :], kc_ref[0, 0, :lo + A_SUB, :]], axis=0)
        v2 = jnp.concatenate([vp_ref[0, 0, lo:, :], vc_ref[0, 0, :lo + A_SUB, :]], axis=0)
        valid = (col + (i * A_QTILE + lo - prev_rows)) >= 0
        outs = []
        for hh in range(2):
            qh = jnp.where(even if hh == 0 else jnp.logical_not(even), q2, jnp.zeros_like(q2))
            s = lax.dot_general(qh, k2, (((1,), (1,)), ((), ())), preferred_element_type=F32)
            s = jnp.where(valid, s + bias_ref[hh], NEG_INF)
            m = jnp.max(s, axis=-1, keepdims=True)
            p = jnp.exp(s - m)
            l = jnp.sum(p, axis=-1, keepdims=True)
            pv = jnp.dot(p.astype(BF16), v2, preferred_element_type=F32)
            outs.append(pv / l)
        o_ref[0, 0, lo:lo + A_SUB, :] = jnp.where(even, outs[0], outs[1]).astype(BF16)


def _chunk_attn_call(qkv, bias):
    B, _, S, _ = qkv.shape
    tq = A_QTILE
    blk = (1, 1, tq, LANES)
    prev = lambda g: (lambda p, b, i: (b, g + p, jnp.maximum(i - 1, 0), 0))
    cur = lambda g: (lambda p, b, i: (b, g + p, i, 0))
    return pl.pallas_call(
        _chunk_attn_body,
        grid=(HEAD_PAIRS, B, S // tq),
        in_specs=[pl.BlockSpec(blk, cur(0)),
                  pl.BlockSpec(blk, prev(HEAD_PAIRS)),
                  pl.BlockSpec(blk, cur(HEAD_PAIRS)),
                  pl.BlockSpec(blk, prev(2 * HEAD_PAIRS)),
                  pl.BlockSpec(blk, cur(2 * HEAD_PAIRS)),
                  pl.BlockSpec((2, A_SUB, A_WIN), lambda p, b, i: (p, 0, 0))],
        out_specs=pl.BlockSpec(blk, lambda p, b, i: (b, p, i, 0)),
        out_shape=jax.ShapeDtypeStruct((B, HEAD_PAIRS, S, LANES), BF16),
        compiler_params=_params(32, 3),
        name="chunk_attn",
    )(qkv, qkv, qkv, qkv, qkv, bias)


def _ffn_body(o_ref, oh_ref, h_ref, hh_ref, vec_ref, wo_ref, win_ref, cw_ref, wout_ref,
              out_ref, hn_ref, h1_ref, acc_ref, *, final_norm):
    i = pl.program_id(1)
    tm = h_ref.shape[1]
    vec = vec_ref[0]
    g1, n2g, sh2, sc2, g2, fg = (vec[r:r + 1] for r in range(6))

    o_cat = jnp.concatenate(
        [jnp.concatenate([oh_ref[0, p] for p in range(HEAD_PAIRS)], axis=1),
         jnp.concatenate([o_ref[0, p] for p in range(HEAD_PAIRS)], axis=1)], axis=0)
    h_cat = jnp.concatenate([hh_ref[0], h_ref[0]], axis=0)
    h1 = h_cat + g1 * jnp.dot(o_cat, wo_ref[...], preferred_element_type=F32)
    hn = _norm_mod(h1, n2g, sh2, sc2)
    row = lax.broadcasted_iota(jnp.int32, (tm + HALO, 1), 0)
    hn = jnp.where(jnp.logical_and(row < HALO, i == 0), 0.0, hn)
    hn_ref[...] = hn.astype(BF16)
    h1_ref[...] = h1[HALO:]

    n_chunks = wout_ref.shape[0] // FFN_CHUNK
    for f in range(n_chunks):
        cols = slice(2 * f * FFN_CHUNK, 2 * (f + 1) * FFN_CHUNK)
        u = jnp.dot(hn_ref[...], win_ref[:, cols], preferred_element_type=F32)
        y = cw_ref[3:4, cols] + pltpu.roll(u, 2, 0) * cw_ref[0:1, cols]
        y = y + pltpu.roll(u, 1, 0) * cw_ref[1:2, cols]
        y = (y + u * cw_ref[2:3, cols])[HALO:]
        act = (_silu(y[:, :FFN_CHUNK]) * y[:, FFN_CHUNK:]).astype(BF16)
        part = jnp.dot(act, wout_ref[f * FFN_CHUNK:(f + 1) * FFN_CHUNK, :],
                       preferred_element_type=F32)
        if f == 0:
            acc_ref[...] = part
        else:
            acc_ref[...] += part

    h2 = h1_ref[...] + g2 * acc_ref[...]
    out_ref[0] = _rms(h2, fg) if final_norm else h2


def _ffn_call(o, h, vec, wo, win, cw, wout, final_norm):
    B, S, D = h.shape
    tm = ROW_TILE
    halo_blocks = tm // HALO
    halo_idx = lambda i: jnp.maximum(i * halo_blocks - 1, 0)
    return pl.pallas_call(
        functools.partial(_ffn_body, final_norm=final_norm),
        grid=(B, S // tm),
        in_specs=[pl.BlockSpec((1, HEAD_PAIRS, tm, LANES), lambda b, i: (b, 0, i, 0)),
                  pl.BlockSpec((1, HEAD_PAIRS, HALO, LANES), lambda b, i: (b, 0, halo_idx(i), 0)),
                  pl.BlockSpec((1, tm, D), lambda b, i: (b, i, 0)),
                  pl.BlockSpec((1, HALO, D), lambda b, i: (b, halo_idx(i), 0)),
                  pl.BlockSpec((1, SUBLANES_F32, D), lambda b, i: (b, 0, 0)),
                  _resident(wo.shape), _resident(win.shape), _resident(cw.shape),
                  _resident(wout.shape)],
        out_specs=pl.BlockSpec((1, tm, D), lambda b, i: (b, i, 0)),
        out_shape=jax.ShapeDtypeStruct((B, S, D), F32),
        scratch_shapes=[pltpu.VMEM((tm + HALO, D), BF16),
                        pltpu.VMEM((tm, D), F32),
                        pltpu.VMEM((tm, D), F32)],
        compiler_params=_params(56, 2),
        name="ffn_final" if final_norm else "ffn",
    )(o, o, h, h, vec, wo, win, cw, wout)


def _kv_body(h_ref, vec_ref, lg_ref, w1_ref, wu_ref, cos_ref, sin_ref, k_ref, v_ref):
    vec = vec_ref[0]
    hn = _norm_mod(h_ref[0], vec[0:1], vec[1:2], vec[2:3]).astype(BF16)
    t = jnp.dot(hn, w1_ref[...], preferred_element_type=F32)
    lat = lg_ref.shape[1]
    ckv = _rms(t[:, :lat], lg_ref[0:1]).astype(BF16)
    kr = (t[:, lat:lat + LANES] * cos_ref[0] + t[:, lat + LANES:] * sin_ref[0]).astype(BF16)
    kv = jnp.dot(ckv, wu_ref[...], preferred_element_type=F32)
    half = HEAD_PAIRS * LANES
    for p in range(HEAD_PAIRS):
        k_ref[0, p, :, :LANES] = kv[:, p * LANES:(p + 1) * LANES].astype(BF16)
        k_ref[0, p, :, LANES:] = kr
        v_ref[0, p] = kv[:, half + p * LANES:half + (p + 1) * LANES].T.astype(BF16)


def _kv_call(h, vec, lat_g, w1, wu, cos, sin):
    B, S, D = h.shape
    tm = ROW_TILE
    row = lambda b, i: (b, i, 0)
    return pl.pallas_call(
        _kv_body,
        grid=(B, S // tm),
        in_specs=[pl.BlockSpec((1, tm, D), row),
                  pl.BlockSpec((1, SUBLANES_F32, D), lambda b, i: (b, 0, 0)),
                  _resident(lat_g.shape), _resident(w1.shape), _resident(wu.shape),
                  pl.BlockSpec((1, tm, LANES), row), pl.BlockSpec((1, tm, LANES), row)],
        out_specs=[pl.BlockSpec((1, HEAD_PAIRS, tm, 2 * LANES), lambda b, i: (b, 0, i, 0)),
                   pl.BlockSpec((1, HEAD_PAIRS, LANES, tm), lambda b, i: (b, 0, 0, i))],
        out_shape=[jax.ShapeDtypeStruct((B, HEAD_PAIRS, S, 2 * LANES), BF16),
                   jax.ShapeDtypeStruct((B, HEAD_PAIRS, LANES, S), BF16)],
        compiler_params=_params(40, 2),
        name="shared_kv",
    )(h, vec, lat_g, w1, wu, cos, sin)


def _q_body(h_ref, vec_ref, qg_ref, wdq_ref, wq_ref, cos_ref, sin_ref, q_ref, *, score_scale):
    vec = vec_ref[0]
    hn = _norm_mod(h_ref[0], vec[0:1], vec[1:2], vec[2:3]).astype(BF16)
    cq = _rms(jnp.dot(hn, wdq_ref[...], preferred_element_type=F32), qg_ref[0:1]).astype(BF16)
    t = jnp.dot(cq, wq_ref[...], preferred_element_type=F32)
    width = HEAD_PAIRS * LANES
    cos = cos_ref[0]
    sin = sin_ref[0]
    for p in range(HEAD_PAIRS):
        g = slice(p * LANES, (p + 1) * LANES)
        rope = t[:, width:2 * width][:, g] * cos + t[:, 2 * width:][:, g] * sin
        qp = jnp.concatenate([t[:, g], rope], axis=1) * score_scale
        q_ref[0, p] = qp.T.astype(BF16)


def _q_call(h, vec, q_g, wdq, wq, cos, sin, score_scale):
    B, S, D = h.shape
    tm = ROW_TILE
    row = lambda b, i: (b, i, 0)
    return pl.pallas_call(
        functools.partial(_q_body, score_scale=score_scale),
        grid=(B, S // tm),
        in_specs=[pl.BlockSpec((1, tm, D), row),
                  pl.BlockSpec((1, SUBLANES_F32, D), lambda b, i: (b, 0, 0)),
                  _resident(q_g.shape), _resident(wdq.shape), _resident(wq.shape),
                  pl.BlockSpec((1, tm, LANES), row), pl.BlockSpec((1, tm, LANES), row)],
        out_specs=pl.BlockSpec((1, HEAD_PAIRS, 2 * LANES, tm), lambda b, i: (b, 0, 0, i)),
        out_shape=jax.ShapeDtypeStruct((B, HEAD_PAIRS, 2 * LANES, S), BF16),
        compiler_params=_params(40, 2),
        name="mla_q",
    )(h, vec, q_g, wdq, wq, cos, sin)


def _mla_body(qT_ref, k_ref, vT_ref, o_ref, qs_ref, sa_ref, sb_ref, m_ref, l_ref, acc_ref):
    qi = pl.program_id(2)
    tq = qT_ref.shape[3]
    tk = sa_ref.shape[0]
    feat = lax.broadcasted_iota(jnp.int32, (2 * LANES, 1), 0)
    in_even = jnp.logical_or(feat < B_NOPE_DIM,
                             jnp.logical_and(feat >= LANES, feat < LANES + B_ROPE_DIM))
    in_odd = jnp.logical_and(jnp.logical_not(in_even), feat < LANES + 2 * B_ROPE_DIM)
    qT = qT_ref[0, 0]
    qs_ref[:, :tq] = jnp.where(in_even, qT, jnp.zeros_like(qT))
    qs_ref[:, tq:] = jnp.where(in_odd, qT, jnp.zeros_like(qT))
    m_ref[...] = jnp.full(m_ref.shape, NEG_INF, F32)
    l_ref[...] = jnp.zeros(l_ref.shape, F32)
    acc_ref[...] = jnp.zeros(acc_ref.shape, F32)

    def scores(j, dst_ref):
        ks = k_ref[0, 0, pl.ds(pl.multiple_of(j * tk, tk), tk), :]
        dst_ref[...] = jnp.dot(ks, qs_ref[...], preferred_element_type=F32)

    def softmax_pv(j, src_ref, masked):
        sT = src_ref[...]
        if masked:
            kc = (lax.broadcasted_iota(jnp.int32, (tk, 1), 0) + (j * tk - qi * tq)) // CHUNK
            qc = (lax.broadcasted_iota(jnp.int32, (1, 2 * tq), 1) % tq) // CHUNK
            sT = jnp.where(kc <= qc, sT, NEG_INF)
        m_old = m_ref[...]
        m_new = jnp.maximum(m_old, jnp.max(sT, axis=0, keepdims=True))
        alpha = jnp.exp2(m_old - m_new)
        pT = jnp.exp2(sT - m_new)
        l_ref[...] = alpha * l_ref[...] + jnp.sum(pT, axis=0, keepdims=True)
        pTb = pT.astype(BF16)
        vT = vT_ref[0, 0, :, pl.ds(pl.multiple_of(j * tk, tk), tk)]
        pv = jnp.concatenate(
            [jnp.dot(vT[:B_V_DIM], pTb[:, :tq], preferred_element_type=F32),
             jnp.dot(vT[B_V_DIM:], pTb[:, tq:], preferred_element_type=F32)], axis=1)
        acc_ref[...] = alpha * acc_ref[...] + pv
        m_ref[...] = m_new

    n = (qi + 1) * (tq // tk)
    scores(0, sa_ref)

    def pair(u, carry):
        scores(2 * u + 1, sb_ref)
        softmax_pv(2 * u, sa_ref, False)
        scores(2 * u + 2, sa_ref)
        softmax_pv(2 * u + 1, sb_ref, False)
        return carry

    lax.fori_loop(0, qi, pair, 0)
    scores(n - 1, sb_ref)
    softmax_pv(n - 2, sa_ref, True)
    softmax_pv(n - 1, sb_ref, True)

    oT = acc_ref[...] / l_ref[...]
    o_ref[0, 0] = jnp.concatenate([oT[:, :tq], oT[:, tq:]], axis=0).T.astype(BF16)


def _mla_call(qT, k, vT):
    B, P, _, S = qT.shape
    tq, tk = MLA_QTILE, MLA_KTILE
    assert tq == 2 * tk
    return pl.pallas_call(
        _mla_body,
        grid=(B, P, S // tq),
        in_specs=[pl.BlockSpec((1, 1, 2 * LANES, tq), lambda b, p, i: (b, p, 0, i)),
                  pl.BlockSpec((1, 1, S, 2 * LANES), lambda b, p, i: (b, p, 0, 0)),
                  pl.BlockSpec((1, 1, LANES, S), lambda b, p, i: (b, p, 0, 0))],
        out_specs=pl.BlockSpec((1, 1, tq, LANES), lambda b, p, i: (b, p, i, 0)),
        out_shape=jax.ShapeDtypeStruct((B, P, S, LANES), BF16),
        scratch_shapes=[pltpu.VMEM((2 * LANES, 2 * tq), BF16),
                        pltpu.VMEM((tk, 2 * tq), F32),
                        pltpu.VMEM((tk, 2 * tq), F32),
                        pltpu.VMEM((1, 2 * tq), F32),
                        pltpu.VMEM((1, 2 * tq), F32),
                        pltpu.VMEM((B_V_DIM, 2 * tq), F32)],
        compiler_params=_params(40, 3),
        name="mla_attn",
    )(qT, k, vT)


def _vec_rows(rows, batch, d):
    full = [jnp.broadcast_to(r, (batch, d)) for r in rows]
    full += [jnp.zeros((batch, d), F32)] * (SUBLANES_F32 - len(full))
    return jnp.stack(full, axis=1)


def _pad_rows(rows, width):
    full = [r.reshape(1, width) for r in rows]
    full += [jnp.zeros((1, width), F32)] * (SUBLANES_F32 - len(full))
    return jnp.concatenate(full, axis=0)


def _pair_rope_cols(w, heads):
    kdim = w.shape[0]
    half = B_ROPE_DIM // 2
    wh = w.reshape(kdim, heads // 2, 2 * B_ROPE_DIM)
    pad = jnp.zeros((kdim, heads // 2, LANES - 2 * B_ROPE_DIM), w.dtype)
    plain = jnp.concatenate([wh, pad], axis=-1).reshape(kdim, -1)
    w4 = w.reshape(kdim, heads, 2, half)
    sw = jnp.concatenate([w4[:, :, 1], w4[:, :, 0]], axis=-1).reshape(kdim, heads // 2, 2 * B_ROPE_DIM)
    swapped = jnp.concatenate([sw, pad], axis=-1).reshape(kdim, -1)
    return plain, swapped


def _ffn_cols(a, ffn_dim):
    lead = a.shape[:-1]
    n = ffn_dim // FFN_CHUNK
    a = a.reshape(lead + (2, n, FFN_CHUNK))
    return jnp.swapaxes(a, -3, -2).reshape(lead + (2 * ffn_dim,))


def _bias_table(rel_bias):
    assert A_SUB == A_MAX_REL
    heads = rel_bias.shape[0]
    n, w = A_SUB, A_WIN
    far = w - 1 - A_MAX_REL
    by_dist = jnp.concatenate(
        [rel_bias[:, 1:], jnp.broadcast_to(rel_bias[:, -1:], (heads, far))], axis=1).astype(F32)
    length = n - 1 + w
    flat = jnp.tile(by_dist[:, ::-1], (1, n))[:, n - 1:n - 1 + n * (length - 1)]
    table = flat.reshape(heads, n, length - 1)[:, :, :w]
    ql = np.arange(n)[:, None]
    kl = np.arange(w)[None, :]
    band = kl // CHUNK - ql // CHUNK
    in_band = np.logical_and(band >= 0, band <= A_LEFT_CHUNKS)
    return jnp.where(in_band[None], table, NEG_INF)


def kernel(x, c, positions, mod_w, mod_b, norm1_g, norm2_g, a_wqkv, a_wo, a_rel_bias, kv_mod_w, kv_mod_b, kv_norm_g, b_wdkv, b_kv_lat_norm_g, b_wuk, b_wuv, b_wkr, b_wdq, b_q_norm_g, b_wuq, b_wqr, b_wo, f_win, f_conv_w, f_conv_b, f_wout, final_g):
    B, S, D = x.shape
    depth = mod_w.shape[0]
    ffn_dim = f_wout.shape[1]

    c_pad = jnp.pad(c, ((0, SUBLANES_F32 - B), (0, 0)))
    mod = _mod_call(c_pad, mod_w, mod_b.reshape(depth, 1, 6 * D))[:, :B]
    kv_mod = _mod_call(c_pad, kv_mod_w[None], kv_mod_b.reshape(1, 1, 2 * D))[0, :B]
    mods = [[mod[l, :, k * D:(k + 1) * D] for k in range(6)] for l in range(depth)]

    half = B_ROPE_DIM // 2
    inv_freq = jnp.power(jnp.float32(ROPE_THETA),
                         -jnp.arange(half, dtype=F32) * (2.0 / B_ROPE_DIM))
    ang = positions.astype(F32)[..., None] * inv_freq
    cos, sin = jnp.cos(ang), jnp.sin(ang)
    reps = LANES // B_ROPE_DIM
    cos_t = jnp.tile(jnp.concatenate([cos, cos], axis=-1), (1, 1, reps))
    sin_t = jnp.tile(jnp.concatenate([-sin, sin], axis=-1), (1, 1, reps))

    def ffn_weights(l):
        win = _ffn_cols(f_win[l], ffn_dim).astype(BF16)
        cw = _pad_rows([_ffn_cols(f_conv_w[l, t], ffn_dim) for t in range(CONV_WIDTH)]
                       + [_ffn_cols(f_conv_b[l], ffn_dim)], 2 * ffn_dim)
        return win, cw, f_wout[l].astype(BF16)

    sh1, sc1, g1, sh2, sc2, g2 = mods[0]
    hd = A_HEADS * A_HEAD_DIM
    wqkv = jnp.concatenate([a_wqkv[0][:, :hd] * (A_HEAD_DIM ** -0.5), a_wqkv[0][:, hd:]],
                           axis=1).astype(BF16)
    qkv = _qkv_call(x, _vec_rows([norm1_g[0], sh1, sc1], B, D), wqkv)
    o = _chunk_attn_call(qkv, _bias_table(a_rel_bias[0]))
    win, cw, wout = ffn_weights(0)
    h = _ffn_call(o, x, _vec_rows([g1, norm2_g[0], sh2, sc2, g2, final_g], B, D),
                  a_wo[0].astype(BF16), win, cw, wout, final_norm=False)

    kr_plain, kr_swapped = _pair_rope_cols(jnp.concatenate([b_wkr, b_wkr], axis=1), 2)
    w1 = jnp.concatenate([b_wdkv, kr_plain, kr_swapped], axis=1).astype(BF16)
    wu = jnp.concatenate([b_wuk, b_wuv], axis=1).astype(BF16)
    k_cat, v = _kv_call(h, _vec_rows([kv_norm_g, kv_mod[:, :D], kv_mod[:, D:]], B, D),
                        _pad_rows([b_kv_lat_norm_g], b_kv_lat_norm_g.shape[0]), w1, wu,
                        cos_t, sin_t)

    sh1, sc1, g1, sh2, sc2, g2 = mods[1]
    qr_plain, qr_swapped = _pair_rope_cols(b_wqr[0], B_HEADS)
    wq = jnp.concatenate([b_wuq[0], qr_plain, qr_swapped], axis=1).astype(BF16)
    score_scale = float((B_NOPE_DIM + B_ROPE_DIM) ** -0.5 * math.log2(math.e))
    q_cat = _q_call(h, _vec_rows([norm1_g[1], sh1, sc1], B, D),
                    _pad_rows([b_q_norm_g[0]], b_q_norm_g.shape[1]),
                    b_wdq[0].astype(BF16), wq, cos_t, sin_t, score_scale)
    o = _mla_call(q_cat, k_cat, v)
    win, cw, wout = ffn_weights(1)
    return _ffn_call(o, h, _vec_rows([g1, norm2_g[1], sh2, sc2, g2, final_g], B, D),
                     b_wo[0].astype(BF16), win, cw, wout, final_norm=True)
```

```python
import functools
import math

import jax
import jax.numpy as jnp
import numpy as np
from jax import lax
from jax.experimental import pallas as pl
from jax.experimental.pallas import tpu as pltpu

F32 = jnp.float32
BF16 = jnp.bfloat16

CHUNK = 64
A_HEADS = 16
A_HEAD_DIM = 64
A_LEFT_CHUNKS = 8
A_MAX_REL = 2 * CHUNK
B_HEADS = 16
B_NOPE_DIM = 64
B_ROPE_DIM = 32
B_V_DIM = 64
ROPE_THETA = 10000.0
CONV_WIDTH = 3
NORM_EPS = 1e-6
NEG_INF = -1e30

LANES = 128
MXU_COLS = 256
SUBLANES_F32 = 8
SUBLANES_BF16 = 16
VMEM_BYTES = 64 * 1024 * 1024

HEAD_PAIRS = 8
HALO = SUBLANES_BF16
ROW_TILE = 512
A_QTILE = 1024
A_SUB = 2 * CHUNK
A_PREV = A_LEFT_CHUNKS * CHUNK
A_WIN = A_PREV + A_SUB
MLA_QTILE = 512
MLA_KTILE = 256
FFN_CHUNK = MXU_COLS


def _params(vmem_mib, n_axes):
    return pltpu.CompilerParams(
        dimension_semantics=("arbitrary",) * n_axes,
        vmem_limit_bytes=vmem_mib * 1024 * 1024)


def _resident(shape):
    zeros = (0,) * len(shape)
    return pl.BlockSpec(shape, lambda *_: zeros, pipeline_mode=pl.Buffered(1))


def _rms(x, g):
    return x * lax.rsqrt(jnp.mean(x * x, axis=-1, keepdims=True) + NORM_EPS) * g


def _norm_mod(x, g, shift, scale):
    return _rms(x, g) * (1.0 + scale) + shift


def _silu(x):
    return x * (1.0 / (1.0 + jnp.exp(-x)))


def _mod_body(c_ref, w_ref, b_ref, o_ref):
    ca = _silu(c_ref[...]).astype(BF16)
    o_ref[0] = jnp.dot(ca, w_ref[0].astype(BF16), preferred_element_type=F32) + b_ref[0]


def _mod_call(c_pad, w, b, tn=1024):
    L, D, N = w.shape
    return pl.pallas_call(
        _mod_body,
        grid=(L, N // tn),
        in_specs=[pl.BlockSpec((SUBLANES_F32, D), lambda l, n: (0, 0)),
                  pl.BlockSpec((1, D, tn), lambda l, n: (l, 0, n)),
                  pl.BlockSpec((1, 1, tn), lambda l, n: (l, 0, n))],
        out_specs=pl.BlockSpec((1, SUBLANES_F32, tn), lambda l, n: (l, 0, n)),
        out_shape=jax.ShapeDtypeStruct((L, SUBLANES_F32, N), F32),
        compiler_params=_params(32, 2),
        name="mod",
    )(c_pad, w, b)


def _qkv_body(x_ref, vec_ref, w_ref, qT_ref, k_ref, vT_ref, hn_ref, *, score_scale):
    vec = vec_ref[0]
    hn_ref[...] = _norm_mod(x_ref[0], vec[0:1], vec[1:2], vec[2:3]).astype(BF16)
    width = HEAD_PAIRS * LANES
    for c in range(HEAD_PAIRS // 2):
        def proj(base):
            cols = slice(base + c * MXU_COLS, base + (c + 1) * MXU_COLS)
            return jnp.dot(hn_ref[...], w_ref[:, cols], preferred_element_type=F32)
        q = proj(0) * score_scale
        k = proj(width)
        v = proj(2 * width)
        for half in range(2):
            g = slice(half * LANES, (half + 1) * LANES)
            qT_ref[0, 2 * c + half] = q[:, g].T.astype(BF16)
            k_ref[0, 2 * c + half] = k[:, g].astype(BF16)
            vT_ref[0, 2 * c + half] = v[:, g].T.astype(BF16)


def _qkv_call(x, vec, w, score_scale):
    B, S, D = x.shape
    tm = ROW_TILE
    rows = pl.BlockSpec((1, HEAD_PAIRS, tm, LANES), lambda b, i: (b, 0, i, 0))
    cols = pl.BlockSpec((1, HEAD_PAIRS, LANES, tm), lambda b, i: (b, 0, 0, i))
    return pl.pallas_call(
        functools.partial(_qkv_body, score_scale=score_scale),
        grid=(B, S // tm),
        in_specs=[pl.BlockSpec((1, tm, D), lambda b, i: (b, i, 0)),
                  pl.BlockSpec((1, SUBLANES_F32, D), lambda b, i: (b, 0, 0)),
                  _resident(w.shape)],
        out_specs=[cols, rows, cols],
        out_shape=[jax.ShapeDtypeStruct((B, HEAD_PAIRS, LANES, S), BF16),
                   jax.ShapeDtypeStruct((B, HEAD_PAIRS, S, LANES), BF16),
                   jax.ShapeDtypeStruct((B, HEAD_PAIRS, LANES, S), BF16)],
        scratch_shapes=[pltpu.VMEM((tm, D), BF16)],
        compiler_params=_params(40, 2),
        name="qkv",
    )(x, vec, w)


def _chunk_attn_body(qT_ref, kp_ref, kc_ref, vTp_ref, vTc_ref, bias_ref, o_ref, qs_ref, s_ref):
    i = pl.program_id(2)
    n_sub = A_QTILE // A_SUB
    feat = lax.broadcasted_iota(jnp.int32, (LANES, 1), 0)
    qT = qT_ref[0, 0]
    q_even = jnp.where(feat < A_HEAD_DIM, qT, jnp.zeros_like(qT))
    q_odd = jnp.where(feat >= A_HEAD_DIM, qT, jnp.zeros_like(qT))
    for j in range(n_sub):
        cols = slice(j * A_SUB, (j + 1) * A_SUB)
        qs_ref[j, :, :A_SUB] = q_even[:, cols]
        qs_ref[j, :, A_SUB:] = q_odd[:, cols]

    def window(prev_ref, cur_ref, lo, axis):
        idx = lambda s: (0, 0, s, slice(None)) if axis == 0 else (0, 0, slice(None), s)
        if lo >= A_PREV:
            return cur_ref[idx(slice(lo - A_PREV, lo + A_SUB))]
        return jnp.concatenate([prev_ref[idx(slice(lo, A_PREV))],
                                cur_ref[idx(slice(0, lo + A_SUB))]], axis=axis)

    def scores(j):
        k2 = window(kp_ref, kc_ref, j * A_SUB, 0)
        s_ref[j % 2] = jnp.dot(k2, qs_ref[j], preferred_element_type=F32)

    def softmax_pv(j, first_tile):
        lo = j * A_SUB
        sT = s_ref[j % 2] + bias_ref[0]
        if first_tile and lo < A_PREV:
            row = lax.broadcasted_iota(jnp.int32, (A_WIN, 1), 0)
            sT = jnp.where(row >= A_PREV - lo, sT, NEG_INF)
        m = jnp.max(sT, axis=0, keepdims=True)
        pT = jnp.exp2(sT - m)
        l = jnp.sum(pT, axis=0, keepdims=True)
        pTb = pT.astype(BF16)
        vT = window(vTp_ref, vTc_ref, lo, 1)
        pv = jnp.concatenate(
            [jnp.dot(vT[:A_HEAD_DIM], pTb[:, :A_SUB], preferred_element_type=F32),
             jnp.dot(vT[A_HEAD_DIM:], pTb[:, A_SUB:], preferred_element_type=F32)], axis=1)
        oT = pv / l
        o_ref[0, 0, lo:lo + A_SUB, :] = jnp.concatenate(
            [oT[:, :A_SUB], oT[:, A_SUB:]], axis=0).T.astype(BF16)

    def tile(first_tile):
        scores(0)
        for j in range(n_sub):
            if j + 1 < n_sub:
                scores(j + 1)
            softmax_pv(j, first_tile)

    @pl.when(i == 0)
    def _():
        tile(True)

    @pl.when(i > 0)
    def _():
        tile(False)


def _chunk_attn_call(qT, k, vT, bias):
    B, _, _, S = qT.shape
    tq = A_QTILE
    ratio = tq // A_PREV
    prev_idx = lambda i: jnp.maximum(i * ratio - 1, 0)
    rows = lambda n: (1, 1, n, LANES)
    cols = lambda n: (1, 1, LANES, n)
    return pl.pallas_call(
        _chunk_attn_body,
        grid=(HEAD_PAIRS, B, S // tq),
        in_specs=[pl.BlockSpec(cols(tq), lambda p, b, i: (b, p, 0, i)),
                  pl.BlockSpec(rows(A_PREV), lambda p, b, i: (b, p, prev_idx(i), 0)),
                  pl.BlockSpec(rows(tq), lambda p, b, i: (b, p, i, 0)),
                  pl.BlockSpec(cols(A_PREV), lambda p, b, i: (b, p, 0, prev_idx(i))),
                  pl.BlockSpec(cols(tq), lambda p, b, i: (b, p, 0, i)),
                  pl.BlockSpec((1, A_WIN, 2 * A_SUB), lambda p, b, i: (p, 0, 0))],
        out_specs=pl.BlockSpec(rows(tq), lambda p, b, i: (b, p, i, 0)),
        out_shape=jax.ShapeDtypeStruct((B, HEAD_PAIRS, S, LANES), BF16),
        scratch_shapes=[pltpu.VMEM((tq // A_SUB, LANES, 2 * A_SUB), BF16),
                        pltpu.VMEM((2, A_WIN, 2 * A_SUB), F32)],
        compiler_params=_params(32, 3),
        name="chunk_attn",
    )(qT, k, k, vT, vT, bias)


def _ffn_body(o_ref, oh_ref, h_ref, hh_ref, vec_ref, wo_ref, win_ref, cw_ref, wout_ref,
              out_ref, hn_ref, h1_ref, acc_ref, u_ref, *, final_norm):
    i = pl.program_id(1)
    tm = h_ref.shape[1]
    vec = vec_ref[0]
    g1, n2g, sh2, sc2, g2, fg = (vec[r:r + 1] for r in range(6))

    o_cat = jnp.concatenate(
        [jnp.concatenate([oh_ref[0, p] for p in range(HEAD_PAIRS)], axis=1),
         jnp.concatenate([o_ref[0, p] for p in range(HEAD_PAIRS)], axis=1)], axis=0)
    h_cat = jnp.concatenate([hh_ref[0], h_ref[0]], axis=0)
    h1 = h_cat + g1 * jnp.dot(o_cat, wo_ref[...], preferred_element_type=F32)
    hn = _norm_mod(h1, n2g, sh2, sc2)
    row = lax.broadcasted_iota(jnp.int32, (tm + HALO, 1), 0)
    hn = jnp.where(jnp.logical_and(row < HALO, i == 0), 0.0, hn)
    hn_ref[...] = hn.astype(BF16)
    h1_ref[...] = h1[HALO:]

    ffn_dim = wout_ref.shape[0]
    groups = FFN_CHUNK // LANES

    def conv(f, base):
        u = jnp.dot(hn_ref[...], win_ref[:, base:base + FFN_CHUNK], preferred_element_type=F32)
        slab0 = (f % 2) * 2 * groups + (groups if base >= ffn_dim else 0)
        outs = []
        for g in range(groups):
            cols = slice(base + g * LANES, base + (g + 1) * LANES)
            u_ref[slab0 + g] = u[:, g * LANES:(g + 1) * LANES]
            y = cw_ref[3:4, cols] + u_ref[slab0 + g, pl.ds(HALO - 2, tm), :] * cw_ref[0:1, cols]
            y = y + u_ref[slab0 + g, pl.ds(HALO - 1, tm), :] * cw_ref[1:2, cols]
            outs.append(y + u_ref[slab0 + g, pl.ds(HALO, tm), :] * cw_ref[2:3, cols])
        return jnp.concatenate(outs, axis=1)

    for f in range(ffn_dim // FFN_CHUNK):
        gate = conv(f, f * FFN_CHUNK)
        val = conv(f, ffn_dim + f * FFN_CHUNK)
        act = (_silu(gate) * val).astype(BF16)
        part = jnp.dot(act, wout_ref[f * FFN_CHUNK:(f + 1) * FFN_CHUNK, :],
                       preferred_element_type=F32)
        if f == 0:
            acc_ref[...] = part
        else:
            acc_ref[...] += part

    h2 = h1_ref[...] + g2 * acc_ref[...]
    out_ref[0] = _rms(h2, fg) if final_norm else h2


def _ffn_call(o, h, vec, wo, win, cw, wout, final_norm):
    B, S, D = h.shape
    tm = ROW_TILE
    halo_blocks = tm // HALO
    halo_idx = lambda i: jnp.maximum(i * halo_blocks - 1, 0)
    return pl.pallas_call(
        functools.partial(_ffn_body, final_norm=final_norm),
        grid=(B, S // tm),
        in_specs=[pl.BlockSpec((1, HEAD_PAIRS, tm, LANES), lambda b, i: (b, 0, i, 0)),
                  pl.BlockSpec((1, HEAD_PAIRS, HALO, LANES), lambda b, i: (b, 0, halo_idx(i), 0)),
                  pl.BlockSpec((1, tm, D), lambda b, i: (b, i, 0)),
                  pl.BlockSpec((1, HALO, D), lambda b, i: (b, halo_idx(i), 0)),
                  pl.BlockSpec((1, SUBLANES_F32, D), lambda b, i: (b, 0, 0)),
                  _resident(wo.shape), _resident(win.shape), _resident(cw.shape),
                  _resident(wout.shape)],
        out_specs=pl.BlockSpec((1, tm, D), lambda b, i: (b, i, 0)),
        out_shape=jax.ShapeDtypeStruct((B, S, D), F32),
        scratch_shapes=[pltpu.VMEM((tm + HALO, D), BF16),
                        pltpu.VMEM((tm, D), F32),
                        pltpu.VMEM((tm, D), F32),
                        pltpu.VMEM((4 * FFN_CHUNK // LANES, tm + HALO, LANES), F32)],
        compiler_params=_params(56, 2),
        name="ffn_final" if final_norm else "ffn",
    )(o, o, h, h, vec, wo, win, cw, wout)


def _kv_body(h_ref, vec_ref, lg_ref, w1_ref, wu_ref, cos_ref, sin_ref, k_ref, v_ref):
    vec = vec_ref[0]
    hn = _norm_mod(h_ref[0], vec[0:1], vec[1:2], vec[2:3]).astype(BF16)
    t = jnp.dot(hn, w1_ref[...], preferred_element_type=F32)
    lat = lg_ref.shape[1]
    ckv = _rms(t[:, :lat], lg_ref[0:1]).astype(BF16)
    kr = (t[:, lat:lat + LANES] * cos_ref[0] + t[:, lat + LANES:] * sin_ref[0]).astype(BF16)
    kv = jnp.dot(ckv, wu_ref[...], preferred_element_type=F32)
    half = HEAD_PAIRS * LANES
    for p in range(HEAD_PAIRS):
        k_ref[0, p, :, :LANES] = kv[:, p * LANES:(p + 1) * LANES].astype(BF16)
        k_ref[0, p, :, LANES:] = kr
        v_ref[0, p] = kv[:, half + p * LANES:half + (p + 1) * LANES].T.astype(BF16)


def _kv_call(h, vec, lat_g, w1, wu, cos, sin):
    B, S, D = h.shape
    tm = ROW_TILE
    row = lambda b, i: (b, i, 0)
    return pl.pallas_call(
        _kv_body,
        grid=(B, S // tm),
        in_specs=[pl.BlockSpec((1, tm, D), row),
                  pl.BlockSpec((1, SUBLANES_F32, D), lambda b, i: (b, 0, 0)),
                  _resident(lat_g.shape), _resident(w1.shape), _resident(wu.shape),
                  pl.BlockSpec((1, tm, LANES), row), pl.BlockSpec((1, tm, LANES), row)],
        out_specs=[pl.BlockSpec((1, HEAD_PAIRS, tm, 2 * LANES), lambda b, i: (b, 0, i, 0)),
                   pl.BlockSpec((1, HEAD_PAIRS, LANES, tm), lambda b, i: (b, 0, 0, i))],
        out_shape=[jax.ShapeDtypeStruct((B, HEAD_PAIRS, S, 2 * LANES), BF16),
                   jax.ShapeDtypeStruct((B, HEAD_PAIRS, LANES, S), BF16)],
        compiler_params=_params(40, 2),
        name="shared_kv",
    )(h, vec, lat_g, w1, wu, cos, sin)


def _q_body(h_ref, vec_ref, qg_ref, wdq_ref, wq_ref, cos_ref, sin_ref, q_ref, *, score_scale):
    vec = vec_ref[0]
    hn = _norm_mod(h_ref[0], vec[0:1], vec[1:2], vec[2:3]).astype(BF16)
    cq = _rms(jnp.dot(hn, wdq_ref[...], preferred_element_type=F32), qg_ref[0:1]).astype(BF16)
    t = jnp.dot(cq, wq_ref[...], preferred_element_type=F32)
    width = HEAD_PAIRS * LANES
    cos = cos_ref[0]
    sin = sin_ref[0]
    for p in range(HEAD_PAIRS):
        g = slice(p * LANES, (p + 1) * LANES)
        rope = t[:, width:2 * width][:, g] * cos + t[:, 2 * width:][:, g] * sin
        qp = jnp.concatenate([t[:, g], rope], axis=1) * score_scale
        q_ref[0, p] = qp.T.astype(BF16)


def _q_call(h, vec, q_g, wdq, wq, cos, sin, score_scale):
    B, S, D = h.shape
    tm = ROW_TILE
    row = lambda b, i: (b, i, 0)
    return pl.pallas_call(
        functools.partial(_q_body, score_scale=score_scale),
        grid=(B, S // tm),
        in_specs=[pl.BlockSpec((1, tm, D), row),
                  pl.BlockSpec((1, SUBLANES_F32, D), lambda b, i: (b, 0, 0)),
                  _resident(q_g.shape), _resident(wdq.shape), _resident(wq.shape),
                  pl.BlockSpec((1, tm, LANES), row), pl.BlockSpec((1, tm, LANES), row)],
        out_specs=pl.BlockSpec((1, HEAD_PAIRS, 2 * LANES, tm), lambda b, i: (b, 0, 0, i)),
        out_shape=jax.ShapeDtypeStruct((B, HEAD_PAIRS, 2 * LANES, S), BF16),
        compiler_params=_params(40, 2),
        name="mla_q",
    )(h, vec, q_g, wdq, wq, cos, sin)


def _mla_body(qT_ref, k_ref, vT_ref, o_ref, qs_ref, sa_ref, sb_ref, m_ref, l_ref, acc_ref):
    qi = pl.program_id(2)
    tq = qT_ref.shape[3]
    tk = sa_ref.shape[0]
    feat = lax.broadcasted_iota(jnp.int32, (2 * LANES, 1), 0)
    in_even = jnp.logical_or(feat < B_NOPE_DIM,
                             jnp.logical_and(feat >= LANES, feat < LANES + B_ROPE_DIM))
    in_odd = jnp.logical_and(jnp.logical_not(in_even), feat < LANES + 2 * B_ROPE_DIM)
    qT = qT_ref[0, 0]
    qs_ref[:, :tq] = jnp.where(in_even, qT, jnp.zeros_like(qT))
    qs_ref[:, tq:] = jnp.where(in_odd, qT, jnp.zeros_like(qT))
    m_ref[...] = jnp.full(m_ref.shape, NEG_INF, F32)
    l_ref[...] = jnp.zeros(l_ref.shape, F32)
    acc_ref[...] = jnp.zeros(acc_ref.shape, F32)

    def scores(j, dst_ref):
        ks = k_ref[0, 0, pl.ds(pl.multiple_of(j * tk, tk), tk), :]
        dst_ref[...] = jnp.dot(ks, qs_ref[...], preferred_element_type=F32)

    def softmax_pv(j, src_ref, masked):
        sT = src_ref[...]
        if masked:
            kc = (lax.broadcasted_iota(jnp.int32, (tk, 1), 0) + (j * tk - qi * tq)) // CHUNK
            qc = (lax.broadcasted_iota(jnp.int32, (1, 2 * tq), 1) % tq) // CHUNK
            sT = jnp.where(kc <= qc, sT, NEG_INF)
        m_old = m_ref[...]
        m_new = jnp.maximum(m_old, jnp.max(sT, axis=0, keepdims=True))
        alpha = jnp.exp2(m_old - m_new)
        pT = jnp.exp2(sT - m_new)
        l_ref[...] = alpha * l_ref[...] + jnp.sum(pT, axis=0, keepdims=True)
        pTb = pT.astype(BF16)
        vT = vT_ref[0, 0, :, pl.ds(pl.multiple_of(j * tk, tk), tk)]
        pv = jnp.concatenate(
            [jnp.dot(vT[:B_V_DIM], pTb[:, :tq], preferred_element_type=F32),
             jnp.dot(vT[B_V_DIM:], pTb[:, tq:], preferred_element_type=F32)], axis=1)
        acc_ref[...] = alpha * acc_ref[...] + pv
        m_ref[...] = m_new

    n = (qi + 1) * (tq // tk)
    scores(0, sa_ref)

    def pair(u, carry):
        scores(2 * u + 1, sb_ref)
        softmax_pv(2 * u, sa_ref, False)
        scores(2 * u + 2, sa_ref)
        softmax_pv(2 * u + 1, sb_ref, False)
        return carry

    lax.fori_loop(0, qi, pair, 0)
    scores(n - 1, sb_ref)
    softmax_pv(n - 2, sa_ref, True)
    softmax_pv(n - 1, sb_ref, True)

    oT = acc_ref[...] / l_ref[...]
    o_ref[0, 0] = jnp.concatenate([oT[:, :tq], oT[:, tq:]], axis=0).T.astype(BF16)


def _mla_call(qT, k, vT):
    B, P, _, S = qT.shape
    tq, tk = MLA_QTILE, MLA_KTILE
    assert tq == 2 * tk
    return pl.pallas_call(
        _mla_body,
        grid=(B, P, S // tq),
        in_specs=[pl.BlockSpec((1, 1, 2 * LANES, tq), lambda b, p, i: (b, p, 0, i)),
                  pl.BlockSpec((1, 1, S, 2 * LANES), lambda b, p, i: (b, p, 0, 0)),
                  pl.BlockSpec((1, 1, LANES, S), lambda b, p, i: (b, p, 0, 0))],
        out_specs=pl.BlockSpec((1, 1, tq, LANES), lambda b, p, i: (b, p, i, 0)),
        out_shape=jax.ShapeDtypeStruct((B, P, S, LANES), BF16),
        scratch_shapes=[pltpu.VMEM((2 * LANES, 2 * tq), BF16),
                        pltpu.VMEM((tk, 2 * tq), F32),
                        pltpu.VMEM((tk, 2 * tq), F32),
                        pltpu.VMEM((1, 2 * tq), F32),
                        pltpu.VMEM((1, 2 * tq), F32),
                        pltpu.VMEM((B_V_DIM, 2 * tq), F32)],
        compiler_params=_params(40, 3),
        name="mla_attn",
    )(qT, k, vT)


def _vec_rows(rows, batch, d):
    full = [jnp.broadcast_to(r, (batch, d)) for r in rows]
    full += [jnp.zeros((batch, d), F32)] * (SUBLANES_F32 - len(full))
    return jnp.stack(full, axis=1)


def _pad_rows(rows, width):
    full = [r.reshape(1, width) for r in rows]
    full += [jnp.zeros((1, width), F32)] * (SUBLANES_F32 - len(full))
    return jnp.concatenate(full, axis=0)


def _pair_rope_cols(w, heads):
    kdim = w.shape[0]
    half = B_ROPE_DIM // 2
    wh = w.reshape(kdim, heads // 2, 2 * B_ROPE_DIM)
    pad = jnp.zeros((kdim, heads // 2, LANES - 2 * B_ROPE_DIM), w.dtype)
    plain = jnp.concatenate([wh, pad], axis=-1).reshape(kdim, -1)
    w4 = w.reshape(kdim, heads, 2, half)
    sw = jnp.concatenate([w4[:, :, 1], w4[:, :, 0]], axis=-1).reshape(kdim, heads // 2, 2 * B_ROPE_DIM)
    swapped = jnp.concatenate([sw, pad], axis=-1).reshape(kdim, -1)
    return plain, swapped


def _bias_table(rel_bias):
    assert A_SUB == A_MAX_REL
    heads = rel_bias.shape[0]
    n, w = A_SUB, A_WIN
    far = w - 1 - A_MAX_REL
    by_dist = jnp.concatenate(
        [rel_bias[:, 1:], jnp.broadcast_to(rel_bias[:, -1:], (heads, far))], axis=1).astype(F32)
    length = n - 1 + w
    flat = jnp.tile(by_dist[:, ::-1], (1, n))[:, n - 1:n - 1 + n * (length - 1)]
    table = flat.reshape(heads, n, length - 1)[:, :, :w]
    ql = np.arange(n)[:, None]
    kl = np.arange(w)[None, :]
    band = kl // CHUNK - ql // CHUNK
    in_band = np.logical_and(band >= 0, band <= A_LEFT_CHUNKS)
    table = jnp.where(in_band[None], table * math.log2(math.e), NEG_INF)
    return jnp.swapaxes(table, 1, 2).reshape(heads // 2, 2, w, n).transpose(0, 2, 1, 3).reshape(
        heads // 2, w, 2 * n)


def kernel(x, c, positions, mod_w, mod_b, norm1_g, norm2_g, a_wqkv, a_wo, a_rel_bias, kv_mod_w, kv_mod_b, kv_norm_g, b_wdkv, b_kv_lat_norm_g, b_wuk, b_wuv, b_wkr, b_wdq, b_q_norm_g, b_wuq, b_wqr, b_wo, f_win, f_conv_w, f_conv_b, f_wout, final_g):
    B, S, D = x.shape
    depth = mod_w.shape[0]
    ffn_dim = f_wout.shape[1]

    c_pad = jnp.pad(c, ((0, SUBLANES_F32 - B), (0, 0)))
    mod = _mod_call(c_pad, mod_w, mod_b.reshape(depth, 1, 6 * D))[:, :B]
    kv_mod = _mod_call(c_pad, kv_mod_w[None], kv_mod_b.reshape(1, 1, 2 * D))[0, :B]
    mods = [[mod[l, :, k * D:(k + 1) * D] for k in range(6)] for l in range(depth)]

    half = B_ROPE_DIM // 2
    inv_freq = jnp.power(jnp.float32(ROPE_THETA),
                         -jnp.arange(half, dtype=F32) * (2.0 / B_ROPE_DIM))
    ang = positions.astype(F32)[..., None] * inv_freq
    cos, sin = jnp.cos(ang), jnp.sin(ang)
    reps = LANES // B_ROPE_DIM
    cos_t = jnp.tile(jnp.concatenate([cos, cos], axis=-1), (1, 1, reps))
    sin_t = jnp.tile(jnp.concatenate([-sin, sin], axis=-1), (1, 1, reps))

    def ffn_weights(l):
        cw = _pad_rows([f_conv_w[l, t] for t in range(CONV_WIDTH)] + [f_conv_b[l]], 2 * ffn_dim)
        return f_win[l].astype(BF16), cw, f_wout[l].astype(BF16)

    sh1, sc1, g1, sh2, sc2, g2 = mods[0]
    qT, k, vT = _qkv_call(x, _vec_rows([norm1_g[0], sh1, sc1], B, D), a_wqkv[0].astype(BF16),
                          float(A_HEAD_DIM ** -0.5 * math.log2(math.e)))
    o = _chunk_attn_call(qT, k, vT, _bias_table(a_rel_bias[0]))
    win, cw, wout = ffn_weights(0)
    h = _ffn_call(o, x, _vec_rows([g1, norm2_g[0], sh2, sc2, g2, final_g], B, D),
                  a_wo[0].astype(BF16), win, cw, wout, final_norm=False)

    kr_plain, kr_swapped = _pair_rope_cols(jnp.concatenate([b_wkr, b_wkr], axis=1), 2)
    w1 = jnp.concatenate([b_wdkv, kr_plain, kr_swapped], axis=1).astype(BF16)
    wu = jnp.concatenate([b_wuk, b_wuv], axis=1).astype(BF16)
    k_cat, v = _kv_call(h, _vec_rows([kv_norm_g, kv_mod[:, :D], kv_mod[:, D:]], B, D),
                        _pad_rows([b_kv_lat_norm_g], b_kv_lat_norm_g.shape[0]), w1, wu,
                        cos_t, sin_t)

    sh1, sc1, g1, sh2, sc2, g2 = mods[1]
    qr_plain, qr_swapped = _pair_rope_cols(b_wqr[0], B_HEADS)
    wq = jnp.concatenate([b_wuq[0], qr_plain, qr_swapped], axis=1).astype(BF16)
    score_scale = float((B_NOPE_DIM + B_ROPE_DIM) ** -0.5 * math.log2(math.e))
    q_cat = _q_call(h, _vec_rows([norm1_g[1], sh1, sc1], B, D),
                    _pad_rows([b_q_norm_g[0]], b_q_norm_g.shape[1]),
                    b_wdq[0].astype(BF16), wq, cos_t, sin_t, score_scale)
    o = _mla_call(q_cat, k_cat, v)
    win, cw, wout = ffn_weights(1)
    return _ffn_call(o, h, _vec_rows([g1, norm2_g[1], sh2, sc2, g2, final_g], B, D),
                     b_wo[0].astype(BF16), win, cw, wout, final_norm=True)
```

```python
import functools
import math

import jax
import jax.numpy as jnp
import numpy as np
from jax import lax
from jax.experimental import pallas as pl
from jax.experimental.pallas import tpu as pltpu

F32 = jnp.float32
BF16 = jnp.bfloat16

CHUNK = 64
A_HEADS = 16
A_HEAD_DIM = 64
A_LEFT_CHUNKS = 8
A_MAX_REL = 2 * CHUNK
B_HEADS = 16
B_NOPE_DIM = 64
B_ROPE_DIM = 32
B_V_DIM = 64
ROPE_THETA = 10000.0
CONV_WIDTH = 3
NORM_EPS = 1e-6
NEG_INF = -1e30

LANES = 128
MXU_COLS = 256
SUBLANES_F32 = 8
SUBLANES_BF16 = 16
VMEM_BYTES = 64 * 1024 * 1024

HEAD_PAIRS = 8
HALO = SUBLANES_BF16
ROW_TILE = 512
A_QTILE = 1024
A_SUB = 2 * CHUNK
A_PREV = A_LEFT_CHUNKS * CHUNK
A_WIN = A_PREV + A_SUB
MLA_QTILE = 512
MLA_KTILE = 256
MLA_VROWS = 64 + SUBLANES_BF16
FFN_CHUNK = MXU_COLS


def _params(vmem_mib, n_axes):
    return pltpu.CompilerParams(
        dimension_semantics=("arbitrary",) * n_axes,
        vmem_limit_bytes=vmem_mib * 1024 * 1024)


def _resident(shape):
    zeros = (0,) * len(shape)
    return pl.BlockSpec(shape, lambda *_: zeros, pipeline_mode=pl.Buffered(1))


def _rms(x, g):
    return x * lax.rsqrt(jnp.mean(x * x, axis=-1, keepdims=True) + NORM_EPS) * g


def _norm_mod(x, g, shift, scale):
    return _rms(x, g) * (1.0 + scale) + shift


def _silu(x):
    return x * (1.0 / (1.0 + jnp.exp(-x)))


def _mod_body(c_ref, w_ref, b_ref, o_ref):
    ca = _silu(c_ref[...]).astype(BF16)
    o_ref[0] = jnp.dot(ca, w_ref[0].astype(BF16), preferred_element_type=F32) + b_ref[0]


def _mod_call(c_pad, w, b, tn=1024):
    L, D, N = w.shape
    return pl.pallas_call(
        _mod_body,
        grid=(L, N // tn),
        in_specs=[pl.BlockSpec((SUBLANES_F32, D), lambda l, n: (0, 0)),
                  pl.BlockSpec((1, D, tn), lambda l, n: (l, 0, n)),
                  pl.BlockSpec((1, 1, tn), lambda l, n: (l, 0, n))],
        out_specs=pl.BlockSpec((1, SUBLANES_F32, tn), lambda l, n: (l, 0, n)),
        out_shape=jax.ShapeDtypeStruct((L, SUBLANES_F32, N), F32),
        compiler_params=_params(32, 2),
        name="mod",
    )(c_pad, w, b)


def _qkv_body(x_ref, vec_ref, w_ref, qT_ref, k_ref, vT_ref, hn_ref, *, score_scale):
    vec = vec_ref[0]
    hn_ref[...] = _norm_mod(x_ref[0], vec[0:1], vec[1:2], vec[2:3]).astype(BF16)
    width = HEAD_PAIRS * LANES
    for c in range(HEAD_PAIRS // 2):
        def proj(base):
            cols = slice(base + c * MXU_COLS, base + (c + 1) * MXU_COLS)
            return jnp.dot(hn_ref[...], w_ref[:, cols], preferred_element_type=F32)
        q = proj(0) * score_scale
        k = proj(width)
        v = proj(2 * width)
        for half in range(2):
            g = slice(half * LANES, (half + 1) * LANES)
            qT_ref[0, 2 * c + half] = q[:, g].T.astype(BF16)
            k_ref[0, 2 * c + half] = k[:, g].astype(BF16)
            vT_ref[0, 2 * c + half] = v[:, g].T.astype(BF16)


def _qkv_call(x, vec, w, score_scale):
    B, S, D = x.shape
    tm = ROW_TILE
    rows = pl.BlockSpec((1, HEAD_PAIRS, tm, LANES), lambda b, i: (b, 0, i, 0))
    cols = pl.BlockSpec((1, HEAD_PAIRS, LANES, tm), lambda b, i: (b, 0, 0, i))
    return pl.pallas_call(
        functools.partial(_qkv_body, score_scale=score_scale),
        grid=(B, S // tm),
        in_specs=[pl.BlockSpec((1, tm, D), lambda b, i: (b, i, 0)),
                  pl.BlockSpec((1, SUBLANES_F32, D), lambda b, i: (b, 0, 0)),
                  _resident(w.shape)],
        out_specs=[cols, rows, cols],
        out_shape=[jax.ShapeDtypeStruct((B, HEAD_PAIRS, LANES, S), BF16),
                   jax.ShapeDtypeStruct((B, HEAD_PAIRS, S, LANES), BF16),
                   jax.ShapeDtypeStruct((B, HEAD_PAIRS, LANES, S), BF16)],
        scratch_shapes=[pltpu.VMEM((tm, D), BF16)],
        compiler_params=_params(40, 2),
        name="qkv",
    )(x, vec, w)


def _chunk_attn_body(qT_ref, kp_ref, kc_ref, vTp_ref, vTc_ref, bias_ref, o_ref, qs_ref, s_ref):
    i = pl.program_id(2)
    n_sub = A_QTILE // A_SUB
    feat = lax.broadcasted_iota(jnp.int32, (LANES, 1), 0)
    qT = qT_ref[0, 0]
    q_even = jnp.where(feat < A_HEAD_DIM, qT, jnp.zeros_like(qT))
    q_odd = jnp.where(feat >= A_HEAD_DIM, qT, jnp.zeros_like(qT))
    for j in range(n_sub):
        cols = slice(j * A_SUB, (j + 1) * A_SUB)
        qs_ref[j, :, :A_SUB] = q_even[:, cols]
        qs_ref[j, :, A_SUB:] = q_odd[:, cols]

    def window(prev_ref, cur_ref, lo, axis):
        idx = lambda s: (0, 0, s, slice(None)) if axis == 0 else (0, 0, slice(None), s)
        if lo >= A_PREV:
            return cur_ref[idx(slice(lo - A_PREV, lo + A_SUB))]
        return jnp.concatenate([prev_ref[idx(slice(lo, A_PREV))],
                                cur_ref[idx(slice(0, lo + A_SUB))]], axis=axis)

    def scores(j):
        k2 = window(kp_ref, kc_ref, j * A_SUB, 0)
        s_ref[j % 2] = jnp.dot(k2, qs_ref[j], preferred_element_type=F32)

    def softmax_pv(j, first_tile):
        lo = j * A_SUB
        sT = s_ref[j % 2] + bias_ref[0]
        if first_tile and lo < A_PREV:
            row = lax.broadcasted_iota(jnp.int32, (A_WIN, 1), 0)
            sT = jnp.where(row >= A_PREV - lo, sT, NEG_INF)
        m = jnp.max(sT, axis=0, keepdims=True)
        pT = jnp.exp2(sT - m)
        l = jnp.sum(pT, axis=0, keepdims=True)
        pTb = pT.astype(BF16)
        vT = window(vTp_ref, vTc_ref, lo, 1)
        pv = jnp.concatenate(
            [jnp.dot(vT[:A_HEAD_DIM], pTb[:, :A_SUB], preferred_element_type=F32),
             jnp.dot(vT[A_HEAD_DIM:], pTb[:, A_SUB:], preferred_element_type=F32)], axis=1)
        oT = pv / l
        o_ref[0, 0, lo:lo + A_SUB, :] = jnp.concatenate(
            [oT[:, :A_SUB], oT[:, A_SUB:]], axis=0).T.astype(BF16)

    def tile(first_tile):
        scores(0)
        for j in range(n_sub):
            if j + 1 < n_sub:
                scores(j + 1)
            softmax_pv(j, first_tile)

    @pl.when(i == 0)
    def _():
        tile(True)

    @pl.when(i > 0)
    def _():
        tile(False)


def _chunk_attn_call(qT, k, vT, bias):
    B, _, _, S = qT.shape
    tq = A_QTILE
    ratio = tq // A_PREV
    prev_idx = lambda i: jnp.maximum(i * ratio - 1, 0)
    rows = lambda n: (1, 1, n, LANES)
    cols = lambda n: (1, 1, LANES, n)
    return pl.pallas_call(
        _chunk_attn_body,
        grid=(HEAD_PAIRS, B, S // tq),
        in_specs=[pl.BlockSpec(cols(tq), lambda p, b, i: (b, p, 0, i)),
                  pl.BlockSpec(rows(A_PREV), lambda p, b, i: (b, p, prev_idx(i), 0)),
                  pl.BlockSpec(rows(tq), lambda p, b, i: (b, p, i, 0)),
                  pl.BlockSpec(cols(A_PREV), lambda p, b, i: (b, p, 0, prev_idx(i))),
                  pl.BlockSpec(cols(tq), lambda p, b, i: (b, p, 0, i)),
                  pl.BlockSpec((1, A_WIN, 2 * A_SUB), lambda p, b, i: (p, 0, 0))],
        out_specs=pl.BlockSpec(rows(tq), lambda p, b, i: (b, p, i, 0)),
        out_shape=jax.ShapeDtypeStruct((B, HEAD_PAIRS, S, LANES), BF16),
        scratch_shapes=[pltpu.VMEM((tq // A_SUB, LANES, 2 * A_SUB), BF16),
                        pltpu.VMEM((2, A_WIN, 2 * A_SUB), F32)],
        compiler_params=_params(32, 3),
        name="chunk_attn",
    )(qT, k, k, vT, vT, bias)


def _ffn_body(o_ref, oh_ref, h_ref, hh_ref, vec_ref, wo_ref, win_ref, cw_ref, wout_ref,
              out_ref, hn_ref, h1_ref, acc_ref, u_ref, *, final_norm):
    i = pl.program_id(1)
    tm = h_ref.shape[1]
    vec = vec_ref[0]
    g1, n2g, sh2, sc2, g2, fg = (vec[r:r + 1] for r in range(6))

    o_cat = jnp.concatenate(
        [jnp.concatenate([oh_ref[0, p] for p in range(HEAD_PAIRS)], axis=1),
         jnp.concatenate([o_ref[0, p] for p in range(HEAD_PAIRS)], axis=1)], axis=0)
    h_cat = jnp.concatenate([hh_ref[0], h_ref[0]], axis=0)
    h1 = h_cat + g1 * jnp.dot(o_cat, wo_ref[...], preferred_element_type=F32)
    hn = _norm_mod(h1, n2g, sh2, sc2)
    row = lax.broadcasted_iota(jnp.int32, (tm + HALO, 1), 0)
    hn = jnp.where(jnp.logical_and(row < HALO, i == 0), 0.0, hn)
    hn_ref[...] = hn.astype(BF16)
    h1_ref[...] = h1[HALO:]

    ffn_dim = wout_ref.shape[0]
    groups = FFN_CHUNK // LANES

    def conv(f, base):
        u = jnp.dot(hn_ref[...], win_ref[:, base:base + FFN_CHUNK], preferred_element_type=F32)
        slab0 = (f % 2) * 2 * groups + (groups if base >= ffn_dim else 0)
        outs = []
        for g in range(groups):
            cols = slice(base + g * LANES, base + (g + 1) * LANES)
            u_ref[slab0 + g] = u[:, g * LANES:(g + 1) * LANES]
            y = cw_ref[3:4, cols] + u_ref[slab0 + g, pl.ds(HALO - 2, tm), :] * cw_ref[0:1, cols]
            y = y + u_ref[slab0 + g, pl.ds(HALO - 1, tm), :] * cw_ref[1:2, cols]
            outs.append(y + u_ref[slab0 + g, pl.ds(HALO, tm), :] * cw_ref[2:3, cols])
        return jnp.concatenate(outs, axis=1)

    for f in range(ffn_dim // FFN_CHUNK):
        gate = conv(f, f * FFN_CHUNK)
        val = conv(f, ffn_dim + f * FFN_CHUNK)
        act = (_silu(gate) * val).astype(BF16)
        part = jnp.dot(act, wout_ref[f * FFN_CHUNK:(f + 1) * FFN_CHUNK, :],
                       preferred_element_type=F32)
        if f == 0:
            acc_ref[...] = part
        else:
            acc_ref[...] += part

    h2 = h1_ref[...] + g2 * acc_ref[...]
    out_ref[0] = _rms(h2, fg) if final_norm else h2


def _ffn_call(o, h, vec, wo, win, cw, wout, final_norm):
    B, S, D = h.shape
    tm = ROW_TILE
    halo_blocks = tm // HALO
    halo_idx = lambda i: jnp.maximum(i * halo_blocks - 1, 0)
    return pl.pallas_call(
        functools.partial(_ffn_body, final_norm=final_norm),
        grid=(B, S // tm),
        in_specs=[pl.BlockSpec((1, HEAD_PAIRS, tm, LANES), lambda b, i: (b, 0, i, 0)),
                  pl.BlockSpec((1, HEAD_PAIRS, HALO, LANES), lambda b, i: (b, 0, halo_idx(i), 0)),
                  pl.BlockSpec((1, tm, D), lambda b, i: (b, i, 0)),
                  pl.BlockSpec((1, HALO, D), lambda b, i: (b, halo_idx(i), 0)),
                  pl.BlockSpec((1, SUBLANES_F32, D), lambda b, i: (b, 0, 0)),
                  _resident(wo.shape), _resident(win.shape), _resident(cw.shape),
                  _resident(wout.shape)],
        out_specs=pl.BlockSpec((1, tm, D), lambda b, i: (b, i, 0)),
        out_shape=jax.ShapeDtypeStruct((B, S, D), F32),
        scratch_shapes=[pltpu.VMEM((tm + HALO, D), BF16),
                        pltpu.VMEM((tm, D), F32),
                        pltpu.VMEM((tm, D), F32),
                        pltpu.VMEM((4 * FFN_CHUNK // LANES, tm + HALO, LANES), F32)],
        compiler_params=_params(56, 2),
        name="ffn_final" if final_norm else "ffn",
    )(o, o, h, h, vec, wo, win, cw, wout)


def _kv_body(h_ref, vec_ref, lg_ref, w1_ref, wu_ref, cos_ref, sin_ref, k_ref, v_ref):
    vec = vec_ref[0]
    hn = _norm_mod(h_ref[0], vec[0:1], vec[1:2], vec[2:3]).astype(BF16)
    t = jnp.dot(hn, w1_ref[...], preferred_element_type=F32)
    lat = lg_ref.shape[1]
    ckv = _rms(t[:, :lat], lg_ref[0:1]).astype(BF16)
    kr = (t[:, lat:lat + LANES] * cos_ref[0] + t[:, lat + LANES:] * sin_ref[0]).astype(BF16)
    kv = jnp.dot(ckv, wu_ref[...], preferred_element_type=F32)
    half = HEAD_PAIRS * LANES
    for p in range(HEAD_PAIRS):
        k_ref[0, p, :, :LANES] = kv[:, p * LANES:(p + 1) * LANES].astype(BF16)
        k_ref[0, p, :, LANES:] = kr
        vt = kv[:, half + p * LANES:half + (p + 1) * LANES].T.astype(BF16)
        ones = jnp.ones((MLA_VROWS - B_V_DIM, vt.shape[1]), BF16)
        for hh in range(2):
            v_ref[0, p, hh * MLA_VROWS:hh * MLA_VROWS + B_V_DIM] = vt[hh * B_V_DIM:(hh + 1) * B_V_DIM]
            v_ref[0, p, hh * MLA_VROWS + B_V_DIM:(hh + 1) * MLA_VROWS] = ones


def _kv_call(h, vec, lat_g, w1, wu, cos, sin):
    B, S, D = h.shape
    tm = ROW_TILE
    row = lambda b, i: (b, i, 0)
    return pl.pallas_call(
        _kv_body,
        grid=(B, S // tm),
        in_specs=[pl.BlockSpec((1, tm, D), row),
                  pl.BlockSpec((1, SUBLANES_F32, D), lambda b, i: (b, 0, 0)),
                  _resident(lat_g.shape), _resident(w1.shape), _resident(wu.shape),
                  pl.BlockSpec((1, tm, LANES), row), pl.BlockSpec((1, tm, LANES), row)],
        out_specs=[pl.BlockSpec((1, HEAD_PAIRS, tm, 2 * LANES), lambda b, i: (b, 0, i, 0)),
                   pl.BlockSpec((1, HEAD_PAIRS, 2 * MLA_VROWS, tm), lambda b, i: (b, 0, 0, i))],
        out_shape=[jax.ShapeDtypeStruct((B, HEAD_PAIRS, S, 2 * LANES), BF16),
                   jax.ShapeDtypeStruct((B, HEAD_PAIRS, 2 * MLA_VROWS, S), BF16)],
        compiler_params=_params(40, 2),
        name="shared_kv",
    )(h, vec, lat_g, w1, wu, cos, sin)


def _q_body(h_ref, vec_ref, qg_ref, wdq_ref, wq_ref, cos_ref, sin_ref, q_ref, *, score_scale):
    vec = vec_ref[0]
    hn = _norm_mod(h_ref[0], vec[0:1], vec[1:2], vec[2:3]).astype(BF16)
    cq = _rms(jnp.dot(hn, wdq_ref[...], preferred_element_type=F32), qg_ref[0:1]).astype(BF16)
    t = jnp.dot(cq, wq_ref[...], preferred_element_type=F32)
    width = HEAD_PAIRS * LANES
    cos = cos_ref[0]
    sin = sin_ref[0]
    for p in range(HEAD_PAIRS):
        g = slice(p * LANES, (p + 1) * LANES)
        rope = t[:, width:2 * width][:, g] * cos + t[:, 2 * width:][:, g] * sin
        qp = jnp.concatenate([t[:, g], rope], axis=1) * score_scale
        q_ref[0, p] = qp.T.astype(BF16)


def _q_call(h, vec, q_g, wdq, wq, cos, sin, score_scale):
    B, S, D = h.shape
    tm = ROW_TILE
    row = lambda b, i: (b, i, 0)
    return pl.pallas_call(
        functools.partial(_q_body, score_scale=score_scale),
        grid=(B, S // tm),
        in_specs=[pl.BlockSpec((1, tm, D), row),
                  pl.BlockSpec((1, SUBLANES_F32, D), lambda b, i: (b, 0, 0)),
                  _resident(q_g.shape), _resident(wdq.shape), _resident(wq.shape),
                  pl.BlockSpec((1, tm, LANES), row), pl.BlockSpec((1, tm, LANES), row)],
        out_specs=pl.BlockSpec((1, HEAD_PAIRS, 2 * LANES, tm), lambda b, i: (b, 0, 0, i)),
        out_shape=jax.ShapeDtypeStruct((B, HEAD_PAIRS, 2 * LANES, S), BF16),
        compiler_params=_params(40, 2),
        name="mla_q",
    )(h, vec, q_g, wdq, wq, cos, sin)


def _mla_body(qT_ref, k_ref, vT_ref, mask_ref, o_ref,
              qs_ref, sa_ref, sb_ref, xa_ref, xb_ref, m_ref, acc_ref):
    qi = pl.program_id(2)
    tq = qT_ref.shape[3]
    tk = sa_ref.shape[0]
    n_sub = tq // tk
    width = 2 * tk
    feat = lax.broadcasted_iota(jnp.int32, (2 * LANES, 1), 0)
    in_even = jnp.logical_or(feat < B_NOPE_DIM,
                             jnp.logical_and(feat >= LANES, feat < LANES + B_ROPE_DIM))
    in_odd = jnp.logical_and(jnp.logical_not(in_even), feat < LANES + 2 * B_ROPE_DIM)
    qT = qT_ref[0, 0]
    for t in range(n_sub):
        qt = qT[:, t * tk:(t + 1) * tk]
        qs_ref[:, t * width:t * width + tk] = jnp.where(in_even, qt, jnp.zeros_like(qt))
        qs_ref[:, t * width + tk:(t + 1) * width] = jnp.where(in_odd, qt, jnp.zeros_like(qt))
    m_ref[...] = jnp.full(m_ref.shape, NEG_INF, F32)
    acc_ref[...] = jnp.zeros(acc_ref.shape, F32)

    def scores(j, dst_ref, max_ref, first_sub=0):
        c0 = first_sub * width
        ks = k_ref[0, 0, pl.ds(pl.multiple_of(j * tk, tk), tk), :]
        sT = jnp.dot(ks, qs_ref[:, c0:], preferred_element_type=F32)
        dst_ref[:, c0:] = sT
        max_ref[:, c0:] = jnp.max(sT, axis=0, keepdims=True)

    def softmax_pv(j, src_ref, max_ref, subs, diagonal=False):
        c0, c1 = subs[0] * width, (subs[-1] + 1) * width
        sT = src_ref[:, c0:c1]
        if diagonal:
            sT = sT + mask_ref[...]
            m_cur = jnp.max(sT, axis=0, keepdims=True)
        else:
            m_cur = max_ref[:, c0:c1]
        m_old = m_ref[:, c0:c1]
        m_new = jnp.maximum(m_old, m_cur)
        alpha = jnp.exp2(m_old - m_new)
        pTb = jnp.exp2(sT - m_new).astype(BF16)
        vT = vT_ref[0, 0, :, pl.ds(pl.multiple_of(j * tk, tk), tk)]
        pv = jnp.concatenate(
            [jnp.dot(vT[hh * MLA_VROWS:(hh + 1) * MLA_VROWS],
                     pTb[:, (t - subs[0]) * width + hh * tk:(t - subs[0]) * width + (hh + 1) * tk],
                     preferred_element_type=F32)
             for t in subs for hh in range(2)], axis=1)
        acc_ref[:, c0:c1] = alpha * acc_ref[:, c0:c1] + pv
        m_ref[:, c0:c1] = m_new

    all_subs = tuple(range(n_sub))
    n = (qi + 1) * n_sub
    scores(0, sa_ref, xa_ref)

    def pair(u):
        scores(2 * u + 1, sb_ref, xb_ref)
        softmax_pv(2 * u, sa_ref, xa_ref, all_subs)
        scores(2 * u + 2, sa_ref, xa_ref)
        softmax_pv(2 * u + 1, sb_ref, xb_ref, all_subs)

    def two_pairs(u, carry):
        pair(2 * u)
        pair(2 * u + 1)
        return carry

    lax.fori_loop(0, qi // 2, two_pairs, 0)

    @pl.when(qi % 2 == 1)
    def _():
        pair(qi - 1)

    scores(n - 1, sb_ref, xb_ref, first_sub=1)
    softmax_pv(n - 2, sa_ref, xa_ref, (0,), diagonal=True)
    softmax_pv(n - 2, sa_ref, xa_ref, (1,))
    softmax_pv(n - 1, sb_ref, xb_ref, (1,), diagonal=True)

    oT = acc_ref[:B_V_DIM, :] / acc_ref[B_V_DIM:B_V_DIM + 1, :]
    for t in range(n_sub):
        both = jnp.concatenate([oT[:, t * width:t * width + tk],
                                oT[:, t * width + tk:(t + 1) * width]], axis=0)
        o_ref[0, 0, t * tk:(t + 1) * tk, :] = both.T.astype(BF16)


def _mla_call(qT, k, vT, mask):
    B, P, _, S = qT.shape
    tq, tk = MLA_QTILE, MLA_KTILE
    assert tq == 2 * tk
    return pl.pallas_call(
        _mla_body,
        grid=(B, P, S // tq),
        in_specs=[pl.BlockSpec((1, 1, 2 * LANES, tq), lambda b, p, i: (b, p, 0, i)),
                  pl.BlockSpec((1, 1, S, 2 * LANES), lambda b, p, i: (b, p, 0, 0)),
                  pl.BlockSpec((1, 1, 2 * MLA_VROWS, S), lambda b, p, i: (b, p, 0, 0)),
                  _resident(mask.shape)],
        out_specs=pl.BlockSpec((1, 1, tq, LANES), lambda b, p, i: (b, p, i, 0)),
        out_shape=jax.ShapeDtypeStruct((B, P, S, LANES), BF16),
        scratch_shapes=[pltpu.VMEM((2 * LANES, 2 * tq), BF16),
                        pltpu.VMEM((tk, 2 * tq), F32),
                        pltpu.VMEM((tk, 2 * tq), F32),
                        pltpu.VMEM((1, 2 * tq), F32),
                        pltpu.VMEM((1, 2 * tq), F32),
                        pltpu.VMEM((1, 2 * tq), F32),
                        pltpu.VMEM((MLA_VROWS, 2 * tq), F32)],
        compiler_params=_params(40, 3),
        name="mla_attn",
    )(qT, k, vT, mask)


def _vec_rows(rows, batch, d):
    full = [jnp.broadcast_to(r, (batch, d)) for r in rows]
    full += [jnp.zeros((batch, d), F32)] * (SUBLANES_F32 - len(full))
    return jnp.stack(full, axis=1)


def _pad_rows(rows, width):
    full = [r.reshape(1, width) for r in rows]
    full += [jnp.zeros((1, width), F32)] * (SUBLANES_F32 - len(full))
    return jnp.concatenate(full, axis=0)


def _pair_rope_cols(w, heads):
    kdim = w.shape[0]
    half = B_ROPE_DIM // 2
    wh = w.reshape(kdim, heads // 2, 2 * B_ROPE_DIM)
    pad = jnp.zeros((kdim, heads // 2, LANES - 2 * B_ROPE_DIM), w.dtype)
    plain = jnp.concatenate([wh, pad], axis=-1).reshape(kdim, -1)
    w4 = w.reshape(kdim, heads, 2, half)
    sw = jnp.concatenate([w4[:, :, 1], w4[:, :, 0]], axis=-1).reshape(kdim, heads // 2, 2 * B_ROPE_DIM)
    swapped = jnp.concatenate([sw, pad], axis=-1).reshape(kdim, -1)
    return plain, swapped


def _diagonal_mask():
    kc = np.arange(MLA_KTILE)[:, None] // CHUNK
    qc = (np.arange(2 * MLA_KTILE)[None, :] % MLA_KTILE) // CHUNK
    return jnp.asarray(np.where(kc <= qc, 0.0, NEG_INF), F32)


def _bias_table(rel_bias):
    assert A_SUB == A_MAX_REL
    heads = rel_bias.shape[0]
    n, w = A_SUB, A_WIN
    far = w - 1 - A_MAX_REL
    by_dist = jnp.concatenate(
        [rel_bias[:, 1:], jnp.broadcast_to(rel_bias[:, -1:], (heads, far))], axis=1).astype(F32)
    length = n - 1 + w
    flat = jnp.tile(by_dist[:, ::-1], (1, n))[:, n - 1:n - 1 + n * (length - 1)]
    table = flat.reshape(heads, n, length - 1)[:, :, :w]
    ql = np.arange(n)[:, None]
    kl = np.arange(w)[None, :]
    band = kl // CHUNK - ql // CHUNK
    in_band = np.logical_and(band >= 0, band <= A_LEFT_CHUNKS)
    table = jnp.where(in_band[None], table * math.log2(math.e), NEG_INF)
    return jnp.swapaxes(table, 1, 2).reshape(heads // 2, 2, w, n).transpose(0, 2, 1, 3).reshape(
        heads // 2, w, 2 * n)


def kernel(x, c, positions, mod_w, mod_b, norm1_g, norm2_g, a_wqkv, a_wo, a_rel_bias, kv_mod_w, kv_mod_b, kv_norm_g, b_wdkv, b_kv_lat_norm_g, b_wuk, b_wuv, b_wkr, b_wdq, b_q_norm_g, b_wuq, b_wqr, b_wo, f_win, f_conv_w, f_conv_b, f_wout, final_g):
    B, S, D = x.shape
    depth = mod_w.shape[0]
    ffn_dim = f_wout.shape[1]

    c_pad = jnp.pad(c, ((0, SUBLANES_F32 - B), (0, 0)))
    mod = _mod_call(c_pad, mod_w, mod_b.reshape(depth, 1, 6 * D))[:, :B]
    kv_mod = _mod_call(c_pad, kv_mod_w[None], kv_mod_b.reshape(1, 1, 2 * D))[0, :B]
    mods = [[mod[l, :, k * D:(k + 1) * D] for k in range(6)] for l in range(depth)]

    half = B_ROPE_DIM // 2
    inv_freq = jnp.power(jnp.float32(ROPE_THETA),
                         -jnp.arange(half, dtype=F32) * (2.0 / B_ROPE_DIM))
    ang = positions.astype(F32)[..., None] * inv_freq
    cos, sin = jnp.cos(ang), jnp.sin(ang)
    reps = LANES // B_ROPE_DIM
    cos_t = jnp.tile(jnp.concatenate([cos, cos], axis=-1), (1, 1, reps))
    sin_t = jnp.tile(jnp.concatenate([-sin, sin], axis=-1), (1, 1, reps))

    def ffn_weights(l):
        cw = _pad_rows([f_conv_w[l, t] for t in range(CONV_WIDTH)] + [f_conv_b[l]], 2 * ffn_dim)
        return f_win[l].astype(BF16), cw, f_wout[l].astype(BF16)

    sh1, sc1, g1, sh2, sc2, g2 = mods[0]
    qT, k, vT = _qkv_call(x, _vec_rows([norm1_g[0], sh1, sc1], B, D), a_wqkv[0].astype(BF16),
                          float(A_HEAD_DIM ** -0.5 * math.log2(math.e)))
    o = _chunk_attn_call(qT, k, vT, _bias_table(a_rel_bias[0]))
    win, cw, wout = ffn_weights(0)
    h = _ffn_call(o, x, _vec_rows([g1, norm2_g[0], sh2, sc2, g2, final_g], B, D),
                  a_wo[0].astype(BF16), win, cw, wout, final_norm=False)

    kr_plain, kr_swapped = _pair_rope_cols(jnp.concatenate([b_wkr, b_wkr], axis=1), 2)
    w1 = jnp.concatenate([b_wdkv, kr_plain, kr_swapped], axis=1).astype(BF16)
    wu = jnp.concatenate([b_wuk, b_wuv], axis=1).astype(BF16)
    k_cat, v = _kv_call(h, _vec_rows([kv_norm_g, kv_mod[:, :D], kv_mod[:, D:]], B, D),
                        _pad_rows([b_kv_lat_norm_g], b_kv_lat_norm_g.shape[0]), w1, wu,
                        cos_t, sin_t)

    sh1, sc1, g1, sh2, sc2, g2 = mods[1]
    qr_plain, qr_swapped = _pair_rope_cols(b_wqr[0], B_HEADS)
    wq = jnp.concatenate([b_wuq[0], qr_plain, qr_swapped], axis=1).astype(BF16)
    score_scale = float((B_NOPE_DIM + B_ROPE_DIM) ** -0.5 * math.log2(math.e))
    q_cat = _q_call(h, _vec_rows([norm1_g[1], sh1, sc1], B, D),
                    _pad_rows([b_q_norm_g[0]], b_q_norm_g.shape[1]),
                    b_wdq[0].astype(BF16), wq, cos_t, sin_t, score_scale)
    o = _mla_call(q_cat, k_cat, v, _diagonal_mask())
    win, cw, wout = ffn_weights(1)
    return _ffn_call(o, h, _vec_rows([g1, norm2_g[1], sh2, sc2, g2, final_g], B, D),
                     b_wo[0].astype(BF16), win, cw, wout, final_norm=True)
```

```python
import functools
import math

import jax
import jax.numpy as jnp
import numpy as np
from jax import lax
from jax.experimental import pallas as pl
from jax.experimental.pallas import tpu as pltpu

F32 = jnp.float32
BF16 = jnp.bfloat16

CHUNK = 64
A_HEADS = 16
A_HEAD_DIM = 64
A_LEFT_CHUNKS = 8
A_MAX_REL = 2 * CHUNK
B_HEADS = 16
B_NOPE_DIM = 64
B_ROPE_DIM = 32
B_V_DIM = 64
ROPE_THETA = 10000.0
CONV_WIDTH = 3
NORM_EPS = 1e-6
NEG_INF = -1e30

LANES = 128
MXU_COLS = 256
SUBLANES_F32 = 8
SUBLANES_BF16 = 16
VMEM_BYTES = 64 * 1024 * 1024

HEAD_PAIRS = 8
HALO = SUBLANES_BF16
ROW_TILE = 512
A_QTILE = 1024
A_SUB = 2 * CHUNK
A_PREV = A_LEFT_CHUNKS * CHUNK
A_WIN = A_PREV + A_SUB
MLA_QTILE = 1024
MLA_KTILE = 256
MLA_VROWS = 64 + SUBLANES_BF16
FFN_CHUNK = MXU_COLS


def _params(vmem_mib, n_axes):
    return pltpu.CompilerParams(
        dimension_semantics=("arbitrary",) * n_axes,
        vmem_limit_bytes=vmem_mib * 1024 * 1024)


def _resident(shape):
    zeros = (0,) * len(shape)
    return pl.BlockSpec(shape, lambda *_: zeros, pipeline_mode=pl.Buffered(1))


def _resident_layer(shape, layer):
    return pl.BlockSpec((None,) + tuple(shape[1:]), lambda *_: (layer,) + (0,) * (len(shape) - 1),
                        pipeline_mode=pl.Buffered(1))


def _rms(x, g):
    return x * lax.rsqrt(jnp.mean(x * x, axis=-1, keepdims=True) + NORM_EPS) * g


def _norm_mod(x, g, shift, scale):
    return _rms(x, g) * (1.0 + scale) + shift


def _silu(x):
    return x * (1.0 / (1.0 + jnp.exp(-x)))


def _mod_body(c_ref, w_ref, b_ref, o_ref):
    ca = _silu(c_ref[...]).astype(BF16)
    o_ref[0] = jnp.dot(ca, w_ref[0].astype(BF16), preferred_element_type=F32) + b_ref[0]


def _mod_call(c_pad, w, b, tn=1024):
    L, D, N = w.shape
    return pl.pallas_call(
        _mod_body,
        grid=(L, N // tn),
        in_specs=[pl.BlockSpec((SUBLANES_F32, D), lambda l, n: (0, 0)),
                  pl.BlockSpec((1, D, tn), lambda l, n: (l, 0, n)),
                  pl.BlockSpec((1, 1, tn), lambda l, n: (l, 0, n))],
        out_specs=pl.BlockSpec((1, SUBLANES_F32, tn), lambda l, n: (l, 0, n)),
        out_shape=jax.ShapeDtypeStruct((L, SUBLANES_F32, N), F32),
        compiler_params=_params(32, 2),
        name="mod",
    )(c_pad, w, b)


def _qkv_body(x_ref, vec_ref, w_ref, qT_ref, k_ref, vT_ref, hn_ref, *, score_scale):
    vec = vec_ref[0]
    hn_ref[...] = _norm_mod(x_ref[0], vec[0:1], vec[1:2], vec[2:3]).astype(BF16)
    width = HEAD_PAIRS * LANES
    for c in range(HEAD_PAIRS // 2):
        def proj(base):
            cols = slice(base + c * MXU_COLS, base + (c + 1) * MXU_COLS)
            return jnp.dot(hn_ref[...], w_ref[:, cols], preferred_element_type=F32)
        q = proj(0) * score_scale
        k = proj(width)
        v = proj(2 * width)
        for half in range(2):
            g = slice(half * LANES, (half + 1) * LANES)
            qT_ref[0, 2 * c + half] = q[:, g].T.astype(BF16)
            k_ref[0, 2 * c + half] = k[:, g].astype(BF16)
            vT_ref[0, 2 * c + half] = v[:, g].T.astype(BF16)


def _qkv_call(x, vec, w, score_scale):
    B, S, D = x.shape
    tm = ROW_TILE
    rows = pl.BlockSpec((1, HEAD_PAIRS, tm, LANES), lambda b, i: (b, 0, i, 0))
    cols = pl.BlockSpec((1, HEAD_PAIRS, LANES, tm), lambda b, i: (b, 0, 0, i))
    return pl.pallas_call(
        functools.partial(_qkv_body, score_scale=score_scale),
        grid=(B, S // tm),
        in_specs=[pl.BlockSpec((1, tm, D), lambda b, i: (b, i, 0)),
                  pl.BlockSpec((1, SUBLANES_F32, D), lambda b, i: (b, 0, 0)),
                  _resident(w.shape)],
        out_specs=[cols, rows, cols],
        out_shape=[jax.ShapeDtypeStruct((B, HEAD_PAIRS, LANES, S), BF16),
                   jax.ShapeDtypeStruct((B, HEAD_PAIRS, S, LANES), BF16),
                   jax.ShapeDtypeStruct((B, HEAD_PAIRS, LANES, S), BF16)],
        scratch_shapes=[pltpu.VMEM((tm, D), BF16)],
        compiler_params=_params(40, 2),
        name="qkv",
    )(x, vec, w)


def _chunk_attn_body(qT_ref, kp_ref, kc_ref, vTp_ref, vTc_ref, bias_ref, o_ref, qs_ref, s_ref):
    i = pl.program_id(2)
    n_sub = A_QTILE // A_SUB
    feat = lax.broadcasted_iota(jnp.int32, (LANES, 1), 0)
    qT = qT_ref[0, 0]
    q_even = jnp.where(feat < A_HEAD_DIM, qT, jnp.zeros_like(qT))
    q_odd = jnp.where(feat >= A_HEAD_DIM, qT, jnp.zeros_like(qT))
    for j in range(n_sub):
        cols = slice(j * A_SUB, (j + 1) * A_SUB)
        qs_ref[j, :, :A_SUB] = q_even[:, cols]
        qs_ref[j, :, A_SUB:] = q_odd[:, cols]

    def window(prev_ref, cur_ref, lo, axis):
        idx = lambda s: (0, 0, s, slice(None)) if axis == 0 else (0, 0, slice(None), s)
        if lo >= A_PREV:
            return cur_ref[idx(slice(lo - A_PREV, lo + A_SUB))]
        return jnp.concatenate([prev_ref[idx(slice(lo, A_PREV))],
                                cur_ref[idx(slice(0, lo + A_SUB))]], axis=axis)

    def scores(j):
        k2 = window(kp_ref, kc_ref, j * A_SUB, 0)
        s_ref[j % 2] = jnp.dot(k2, qs_ref[j], preferred_element_type=F32)

    def softmax_pv(j, first_tile):
        lo = j * A_SUB
        sT = s_ref[j % 2] + bias_ref[0]
        if first_tile and lo < A_PREV:
            row = lax.broadcasted_iota(jnp.int32, (A_WIN, 1), 0)
            sT = jnp.where(row >= A_PREV - lo, sT, NEG_INF)
        m = jnp.max(sT, axis=0, keepdims=True)
        pT = jnp.exp2(sT - m)
        l = jnp.sum(pT, axis=0, keepdims=True)
        pTb = pT.astype(BF16)
        vT = window(vTp_ref, vTc_ref, lo, 1)
        pv = jnp.concatenate(
            [jnp.dot(vT[:A_HEAD_DIM], pTb[:, :A_SUB], preferred_element_type=F32),
             jnp.dot(vT[A_HEAD_DIM:], pTb[:, A_SUB:], preferred_element_type=F32)], axis=1)
        oT = pv / l
        o_ref[0, 0, lo:lo + A_SUB, :] = jnp.concatenate(
            [oT[:, :A_SUB], oT[:, A_SUB:]], axis=0).T.astype(BF16)

    def tile(first_tile):
        scores(0)
        for j in range(n_sub):
            if j + 1 < n_sub:
                scores(j + 1)
            softmax_pv(j, first_tile)

    @pl.when(i == 0)
    def _():
        tile(True)

    @pl.when(i > 0)
    def _():
        tile(False)


def _chunk_attn_call(qT, k, vT, bias):
    B, _, _, S = qT.shape
    tq = A_QTILE
    ratio = tq // A_PREV
    prev_idx = lambda i: jnp.maximum(i * ratio - 1, 0)
    rows = lambda n: (1, 1, n, LANES)
    cols = lambda n: (1, 1, LANES, n)
    return pl.pallas_call(
        _chunk_attn_body,
        grid=(HEAD_PAIRS, B, S // tq),
        in_specs=[pl.BlockSpec(cols(tq), lambda p, b, i: (b, p, 0, i)),
                  pl.BlockSpec(rows(A_PREV), lambda p, b, i: (b, p, prev_idx(i), 0)),
                  pl.BlockSpec(rows(tq), lambda p, b, i: (b, p, i, 0)),
                  pl.BlockSpec(cols(A_PREV), lambda p, b, i: (b, p, 0, prev_idx(i))),
                  pl.BlockSpec(cols(tq), lambda p, b, i: (b, p, 0, i)),
                  pl.BlockSpec((1, A_WIN, 2 * A_SUB), lambda p, b, i: (p, 0, 0))],
        out_specs=pl.BlockSpec(rows(tq), lambda p, b, i: (b, p, i, 0)),
        out_shape=jax.ShapeDtypeStruct((B, HEAD_PAIRS, S, LANES), BF16),
        scratch_shapes=[pltpu.VMEM((tq // A_SUB, LANES, 2 * A_SUB), BF16),
                        pltpu.VMEM((2, A_WIN, 2 * A_SUB), F32)],
        compiler_params=_params(32, 3),
        name="chunk_attn",
    )(qT, k, k, vT, vT, bias)


def _ffn_body(o_ref, oh_ref, h_ref, hh_ref, vec_ref, wo_ref, win_ref, cw_ref, wout_ref,
              out_ref, hn_ref, h1_ref, acc_ref, u_ref, *, final_norm):
    i = pl.program_id(1)
    tm = h_ref.shape[1]
    vec = vec_ref[0]
    g1, n2g, sh2, sc2, g2, fg = (vec[r:r + 1] for r in range(6))

    o_cat = jnp.concatenate(
        [jnp.concatenate([oh_ref[0, p] for p in range(HEAD_PAIRS)], axis=1),
         jnp.concatenate([o_ref[0, p] for p in range(HEAD_PAIRS)], axis=1)], axis=0)
    h_cat = jnp.concatenate([hh_ref[0], h_ref[0]], axis=0)
    h1 = h_cat + g1 * jnp.dot(o_cat, wo_ref[...], preferred_element_type=F32)
    hn = _norm_mod(h1, n2g, sh2, sc2)
    row = lax.broadcasted_iota(jnp.int32, (tm + HALO, 1), 0)
    hn = jnp.where(jnp.logical_and(row < HALO, i == 0), 0.0, hn)
    hn_ref[...] = hn.astype(BF16)
    h1_ref[...] = h1[HALO:]

    ffn_dim = wout_ref.shape[0]
    groups = FFN_CHUNK // LANES

    def slabs(f, base):
        return (f % 2) * 2 * groups + (groups if base >= ffn_dim else 0)

    def up(f, base):
        u = jnp.dot(hn_ref[...], win_ref[:, base:base + FFN_CHUNK], preferred_element_type=F32)
        for g in range(groups):
            u_ref[slabs(f, base) + g] = u[:, g * LANES:(g + 1) * LANES]

    def conv(f, base):
        outs = []
        for g in range(groups):
            cols = slice(base + g * LANES, base + (g + 1) * LANES)
            slab = slabs(f, base) + g
            y = cw_ref[3:4, cols] + u_ref[slab, pl.ds(HALO - 2, tm), :] * cw_ref[0:1, cols]
            y = y + u_ref[slab, pl.ds(HALO - 1, tm), :] * cw_ref[1:2, cols]
            outs.append(y + u_ref[slab, pl.ds(HALO, tm), :] * cw_ref[2:3, cols])
        return jnp.concatenate(outs, axis=1)

    def down(f, act):
        part = jnp.dot(act, wout_ref[f * FFN_CHUNK:(f + 1) * FFN_CHUNK, :],
                       preferred_element_type=F32)
        if f == 0:
            acc_ref[...] = part
        else:
            acc_ref[...] += part

    n_chunks = ffn_dim // FFN_CHUNK
    up(0, 0)
    up(0, ffn_dim)
    act = None
    for f in range(n_chunks):
        if f + 1 < n_chunks:
            up(f + 1, (f + 1) * FFN_CHUNK)
            up(f + 1, ffn_dim + (f + 1) * FFN_CHUNK)
        if f > 0:
            down(f - 1, act)
        act = (_silu(conv(f, f * FFN_CHUNK)) * conv(f, ffn_dim + f * FFN_CHUNK)).astype(BF16)
    down(n_chunks - 1, act)

    h2 = h1_ref[...] + g2 * acc_ref[...]
    out_ref[0] = _rms(h2, fg) if final_norm else h2


def _ffn_call(o, h, vec, wo, win, cw, wout, layer, final_norm):
    B, S, D = h.shape
    tm = ROW_TILE
    halo_blocks = tm // HALO
    halo_idx = lambda i: jnp.maximum(i * halo_blocks - 1, 0)
    return pl.pallas_call(
        functools.partial(_ffn_body, final_norm=final_norm),
        grid=(B, S // tm),
        in_specs=[pl.BlockSpec((1, HEAD_PAIRS, tm, LANES), lambda b, i: (b, 0, i, 0)),
                  pl.BlockSpec((1, HEAD_PAIRS, HALO, LANES), lambda b, i: (b, 0, halo_idx(i), 0)),
                  pl.BlockSpec((1, tm, D), lambda b, i: (b, i, 0)),
                  pl.BlockSpec((1, HALO, D), lambda b, i: (b, halo_idx(i), 0)),
                  pl.BlockSpec((1, SUBLANES_F32, D), lambda b, i: (b, 0, 0)),
                  _resident(wo.shape), _resident_layer(win.shape, layer), _resident(cw.shape),
                  _resident_layer(wout.shape, layer)],
        out_specs=pl.BlockSpec((1, tm, D), lambda b, i: (b, i, 0)),
        out_shape=jax.ShapeDtypeStruct((B, S, D), F32),
        scratch_shapes=[pltpu.VMEM((tm + HALO, D), BF16),
                        pltpu.VMEM((tm, D), F32),
                        pltpu.VMEM((tm, D), F32),
                        pltpu.VMEM((4 * FFN_CHUNK // LANES, tm + HALO, LANES), F32)],
        compiler_params=_params(56, 2),
        name="ffn_final" if final_norm else "ffn",
    )(o, o, h, h, vec, wo, win, cw, wout)


def _kv_body(h_ref, vec_ref, lg_ref, w1_ref, wu_ref, cos_ref, sin_ref, k_ref, v_ref):
    vec = vec_ref[0]
    hn = _norm_mod(h_ref[0], vec[0:1], vec[1:2], vec[2:3]).astype(BF16)
    t = jnp.dot(hn, w1_ref[...], preferred_element_type=F32)
    lat = lg_ref.shape[1]
    ckv = _rms(t[:, :lat], lg_ref[0:1]).astype(BF16)
    kr = (t[:, lat:lat + LANES] * cos_ref[0] + t[:, lat + LANES:] * sin_ref[0]).astype(BF16)
    kv = jnp.dot(ckv, wu_ref[...], preferred_element_type=F32)
    half = HEAD_PAIRS * LANES
    for p in range(HEAD_PAIRS):
        k_ref[0, p, :, :LANES] = kv[:, p * LANES:(p + 1) * LANES].astype(BF16)
        k_ref[0, p, :, LANES:] = kr
        vt = kv[:, half + p * LANES:half + (p + 1) * LANES].T.astype(BF16)
        ones = jnp.ones((MLA_VROWS - B_V_DIM, vt.shape[1]), BF16)
        for hh in range(2):
            v_ref[0, p, hh * MLA_VROWS:hh * MLA_VROWS + B_V_DIM] = vt[hh * B_V_DIM:(hh + 1) * B_V_DIM]
            v_ref[0, p, hh * MLA_VROWS + B_V_DIM:(hh + 1) * MLA_VROWS] = ones


def _kv_call(h, vec, lat_g, w1, wu, cos, sin):
    B, S, D = h.shape
    tm = ROW_TILE
    row = lambda b, i: (b, i, 0)
    return pl.pallas_call(
        _kv_body,
        grid=(B, S // tm),
        in_specs=[pl.BlockSpec((1, tm, D), row),
                  pl.BlockSpec((1, SUBLANES_F32, D), lambda b, i: (b, 0, 0)),
                  _resident(lat_g.shape), _resident(w1.shape), _resident(wu.shape),
                  pl.BlockSpec((1, tm, LANES), row), pl.BlockSpec((1, tm, LANES), row)],
        out_specs=[pl.BlockSpec((1, HEAD_PAIRS, tm, 2 * LANES), lambda b, i: (b, 0, i, 0)),
                   pl.BlockSpec((1, HEAD_PAIRS, 2 * MLA_VROWS, tm), lambda b, i: (b, 0, 0, i))],
        out_shape=[jax.ShapeDtypeStruct((B, HEAD_PAIRS, S, 2 * LANES), BF16),
                   jax.ShapeDtypeStruct((B, HEAD_PAIRS, 2 * MLA_VROWS, S), BF16)],
        compiler_params=_params(40, 2),
        name="shared_kv",
    )(h, vec, lat_g, w1, wu, cos, sin)


def _q_body(h_ref, vec_ref, qg_ref, wdq_ref, wq_ref, cos_ref, sin_ref, q_ref, *, score_scale):
    vec = vec_ref[0]
    hn = _norm_mod(h_ref[0], vec[0:1], vec[1:2], vec[2:3]).astype(BF16)
    cq = _rms(jnp.dot(hn, wdq_ref[...], preferred_element_type=F32), qg_ref[0:1]).astype(BF16)
    t = jnp.dot(cq, wq_ref[...], preferred_element_type=F32)
    width = HEAD_PAIRS * LANES
    cos = cos_ref[0]
    sin = sin_ref[0]
    for p in range(HEAD_PAIRS):
        g = slice(p * LANES, (p + 1) * LANES)
        rope = t[:, width:2 * width][:, g] * cos + t[:, 2 * width:][:, g] * sin
        qp = jnp.concatenate([t[:, g], rope], axis=1) * score_scale
        q_ref[0, p] = qp.T.astype(BF16)


def _q_call(h, vec, q_g, wdq, wq, cos, sin, score_scale):
    B, S, D = h.shape
    tm = ROW_TILE
    row = lambda b, i: (b, i, 0)
    return pl.pallas_call(
        functools.partial(_q_body, score_scale=score_scale),
        grid=(B, S // tm),
        in_specs=[pl.BlockSpec((1, tm, D), row),
                  pl.BlockSpec((1, SUBLANES_F32, D), lambda b, i: (b, 0, 0)),
                  _resident(q_g.shape), _resident(wdq.shape), _resident(wq.shape),
                  pl.BlockSpec((1, tm, LANES), row), pl.BlockSpec((1, tm, LANES), row)],
        out_specs=pl.BlockSpec((1, HEAD_PAIRS, 2 * LANES, tm), lambda b, i: (b, 0, 0, i)),
        out_shape=jax.ShapeDtypeStruct((B, HEAD_PAIRS, 2 * LANES, S), BF16),
        compiler_params=_params(40, 2),
        name="mla_q",
    )(h, vec, q_g, wdq, wq, cos, sin)


def _mla_body(qT_ref, k_ref, vT_ref, mask_ref, o_ref,
              qs_ref, sa_ref, sb_ref, xa_ref, xb_ref, m_ref, acc_ref):
    qi = pl.program_id(2)
    tq = qT_ref.shape[3]
    tk = sa_ref.shape[0]
    n_sub = tq // tk
    width = 2 * tk
    feat = lax.broadcasted_iota(jnp.int32, (2 * LANES, 1), 0)
    in_even = jnp.logical_or(feat < B_NOPE_DIM,
                             jnp.logical_and(feat >= LANES, feat < LANES + B_ROPE_DIM))
    in_odd = jnp.logical_and(jnp.logical_not(in_even), feat < LANES + 2 * B_ROPE_DIM)
    qT = qT_ref[0, 0]
    for t in range(n_sub):
        qt = qT[:, t * tk:(t + 1) * tk]
        qs_ref[:, t * width:t * width + tk] = jnp.where(in_even, qt, jnp.zeros_like(qt))
        qs_ref[:, t * width + tk:(t + 1) * width] = jnp.where(in_odd, qt, jnp.zeros_like(qt))
    m_ref[...] = jnp.full(m_ref.shape, NEG_INF, F32)
    acc_ref[...] = jnp.zeros(acc_ref.shape, F32)

    def scores(j, dst_ref, max_ref, first_sub=0):
        c0 = first_sub * width
        ks = k_ref[0, 0, pl.ds(pl.multiple_of(j * tk, tk), tk), :]
        sT = jnp.dot(ks, qs_ref[:, c0:], preferred_element_type=F32)
        dst_ref[:, c0:] = sT
        max_ref[:, c0:] = jnp.max(sT, axis=0, keepdims=True)

    def softmax_pv(j, src_ref, max_ref, subs, diagonal=False):
        c0, c1 = subs[0] * width, (subs[-1] + 1) * width
        sT = src_ref[:, c0:c1]
        if diagonal:
            sT = sT + mask_ref[...]
            m_cur = jnp.max(sT, axis=0, keepdims=True)
        else:
            m_cur = max_ref[:, c0:c1]
        m_old = m_ref[:, c0:c1]
        m_new = jnp.maximum(m_old, m_cur)
        alpha = jnp.exp2(m_old - m_new)
        pTb = jnp.exp2(sT - m_new).astype(BF16)
        vT = vT_ref[0, 0, :, pl.ds(pl.multiple_of(j * tk, tk), tk)]
        pv = jnp.concatenate(
            [jnp.dot(vT[hh * MLA_VROWS:(hh + 1) * MLA_VROWS],
                     pTb[:, (t - subs[0]) * width + hh * tk:(t - subs[0]) * width + (hh + 1) * tk],
                     preferred_element_type=F32)
             for t in subs for hh in range(2)], axis=1)
        acc_ref[:, c0:c1] = alpha * acc_ref[:, c0:c1] + pv
        m_ref[:, c0:c1] = m_new

    all_subs = tuple(range(n_sub))
    n = (qi + 1) * n_sub
    scores(0, sa_ref, xa_ref)

    def pair(u):
        scores(2 * u + 1, sb_ref, xb_ref)
        softmax_pv(2 * u, sa_ref, xa_ref, all_subs)
        scores(2 * u + 2, sa_ref, xa_ref)
        softmax_pv(2 * u + 1, sb_ref, xb_ref, all_subs)

    def two_pairs(u, carry):
        pair(2 * u)
        pair(2 * u + 1)
        return carry

    pairs = qi * (n_sub // 2)
    lax.fori_loop(0, pairs // 2, two_pairs, 0)
    if n_sub % 4:
        @pl.when(pairs % 2 == 1)
        def _():
            pair(pairs - 1)

    bufs = ((sa_ref, xa_ref), (sb_ref, xb_ref))
    for r in range(n_sub):
        j = n - n_sub + r
        if r + 1 < n_sub:
            scores(j + 1, *bufs[(r + 1) % 2], first_sub=r + 1)
        softmax_pv(j, *bufs[r % 2], (r,), diagonal=True)
        if r + 1 < n_sub:
            softmax_pv(j, *bufs[r % 2], tuple(range(r + 1, n_sub)))

    oT = acc_ref[:B_V_DIM, :] / acc_ref[B_V_DIM:B_V_DIM + 1, :]
    for t in range(n_sub):
        both = jnp.concatenate([oT[:, t * width:t * width + tk],
                                oT[:, t * width + tk:(t + 1) * width]], axis=0)
        o_ref[0, 0, t * tk:(t + 1) * tk, :] = both.T.astype(BF16)


def _mla_call(qT, k, vT, mask):
    B, P, _, S = qT.shape
    tq, tk = MLA_QTILE, MLA_KTILE
    assert tq % (2 * tk) == 0
    return pl.pallas_call(
        _mla_body,
        grid=(B, P, S // tq),
        in_specs=[pl.BlockSpec((1, 1, 2 * LANES, tq), lambda b, p, i: (b, p, 0, i)),
                  pl.BlockSpec((1, 1, S, 2 * LANES), lambda b, p, i: (b, p, 0, 0)),
                  pl.BlockSpec((1, 1, 2 * MLA_VROWS, S), lambda b, p, i: (b, p, 0, 0)),
                  _resident(mask.shape)],
        out_specs=pl.BlockSpec((1, 1, tq, LANES), lambda b, p, i: (b, p, i, 0)),
        out_shape=jax.ShapeDtypeStruct((B, P, S, LANES), BF16),
        scratch_shapes=[pltpu.VMEM((2 * LANES, 2 * tq), BF16),
                        pltpu.VMEM((tk, 2 * tq), F32),
                        pltpu.VMEM((tk, 2 * tq), F32),
                        pltpu.VMEM((1, 2 * tq), F32),
                        pltpu.VMEM((1, 2 * tq), F32),
                        pltpu.VMEM((1, 2 * tq), F32),
                        pltpu.VMEM((MLA_VROWS, 2 * tq), F32)],
        compiler_params=_params(40, 3),
        name="mla_attn",
    )(qT, k, vT, mask)


def _vec_rows(rows, batch, d):
    full = [jnp.broadcast_to(r, (batch, d)) for r in rows]
    full += [jnp.zeros((batch, d), F32)] * (SUBLANES_F32 - len(full))
    return jnp.stack(full, axis=1)


def _pad_rows(rows, width):
    full = [r.reshape(1, width) for r in rows]
    full += [jnp.zeros((1, width), F32)] * (SUBLANES_F32 - len(full))
    return jnp.concatenate(full, axis=0)


def _pair_rope_cols(w, heads):
    kdim = w.shape[0]
    half = B_ROPE_DIM // 2
    wh = w.reshape(kdim, heads // 2, 2 * B_ROPE_DIM)
    pad = jnp.zeros((kdim, heads // 2, LANES - 2 * B_ROPE_DIM), w.dtype)
    plain = jnp.concatenate([wh, pad], axis=-1).reshape(kdim, -1)
    w4 = w.reshape(kdim, heads, 2, half)
    sw = jnp.concatenate([w4[:, :, 1], w4[:, :, 0]], axis=-1).reshape(kdim, heads // 2, 2 * B_ROPE_DIM)
    swapped = jnp.concatenate([sw, pad], axis=-1).reshape(kdim, -1)
    return plain, swapped


def _diagonal_mask():
    kc = np.arange(MLA_KTILE)[:, None] // CHUNK
    qc = (np.arange(2 * MLA_KTILE)[None, :] % MLA_KTILE) // CHUNK
    return jnp.asarray(np.where(kc <= qc, 0.0, NEG_INF), F32)


def _bias_table(rel_bias):
    assert A_SUB == A_MAX_REL
    heads = rel_bias.shape[0]
    n, w = A_SUB, A_WIN
    far = w - 1 - A_MAX_REL
    by_dist = jnp.concatenate(
        [rel_bias[:, 1:], jnp.broadcast_to(rel_bias[:, -1:], (heads, far))], axis=1).astype(F32)
    length = n - 1 + w
    flat = jnp.tile(by_dist[:, ::-1], (1, n))[:, n - 1:n - 1 + n * (length - 1)]
    table = flat.reshape(heads, n, length - 1)[:, :, :w]
    ql = np.arange(n)[:, None]
    kl = np.arange(w)[None, :]
    band = kl // CHUNK - ql // CHUNK
    in_band = np.logical_and(band >= 0, band <= A_LEFT_CHUNKS)
    table = jnp.where(in_band[None], table * math.log2(math.e), NEG_INF)
    return jnp.swapaxes(table, 1, 2).reshape(heads // 2, 2, w, n).transpose(0, 2, 1, 3).reshape(
        heads // 2, w, 2 * n)


def kernel(x, c, positions, mod_w, mod_b, norm1_g, norm2_g, a_wqkv, a_wo, a_rel_bias, kv_mod_w, kv_mod_b, kv_norm_g, b_wdkv, b_kv_lat_norm_g, b_wuk, b_wuv, b_wkr, b_wdq, b_q_norm_g, b_wuq, b_wqr, b_wo, f_win, f_conv_w, f_conv_b, f_wout, final_g):
    B, S, D = x.shape
    depth = mod_w.shape[0]
    ffn_dim = f_wout.shape[1]

    c_pad = jnp.pad(c, ((0, SUBLANES_F32 - B), (0, 0)))
    mod = _mod_call(c_pad, mod_w, mod_b.reshape(depth, 1, 6 * D))[:, :B]
    kv_mod = _mod_call(c_pad, kv_mod_w[None], kv_mod_b.reshape(1, 1, 2 * D))[0, :B]
    mods = [[mod[l, :, k * D:(k + 1) * D] for k in range(6)] for l in range(depth)]

    half = B_ROPE_DIM // 2
    inv_freq = jnp.power(jnp.float32(ROPE_THETA),
                         -jnp.arange(half, dtype=F32) * (2.0 / B_ROPE_DIM))
    ang = positions.astype(F32)[..., None] * inv_freq
    cos, sin = jnp.cos(ang), jnp.sin(ang)
    reps = LANES // B_ROPE_DIM
    cos_t = jnp.concatenate([cos, cos] * reps, axis=-1)
    sin_t = jnp.concatenate([-sin, sin] * reps, axis=-1)

    win_all = f_win.astype(BF16)
    wout_all = f_wout.astype(BF16)

    def conv_rows(l):
        return _pad_rows([f_conv_w[l, t] for t in range(CONV_WIDTH)] + [f_conv_b[l]], 2 * ffn_dim)

    sh1, sc1, g1, sh2, sc2, g2 = mods[0]
    qT, k, vT = _qkv_call(x, _vec_rows([norm1_g[0], sh1, sc1], B, D), a_wqkv[0].astype(BF16),
                          float(A_HEAD_DIM ** -0.5 * math.log2(math.e)))
    o = _chunk_attn_call(qT, k, vT, _bias_table(a_rel_bias[0]))
    h = _ffn_call(o, x, _vec_rows([g1, norm2_g[0], sh2, sc2, g2, final_g], B, D),
                  a_wo[0].astype(BF16), win_all, conv_rows(0), wout_all, 0, final_norm=False)

    kr_plain, kr_swapped = _pair_rope_cols(jnp.concatenate([b_wkr, b_wkr], axis=1), 2)
    w1 = jnp.concatenate([b_wdkv, kr_plain, kr_swapped], axis=1).astype(BF16)
    wu = jnp.concatenate([b_wuk, b_wuv], axis=1).astype(BF16)
    k_cat, v = _kv_call(h, _vec_rows([kv_norm_g, kv_mod[:, :D], kv_mod[:, D:]], B, D),
                        _pad_rows([b_kv_lat_norm_g], b_kv_lat_norm_g.shape[0]), w1, wu,
                        cos_t, sin_t)

    sh1, sc1, g1, sh2, sc2, g2 = mods[1]
    qr_plain, qr_swapped = _pair_rope_cols(b_wqr[0], B_HEADS)
    wq = jnp.concatenate([b_wuq[0], qr_plain, qr_swapped], axis=1).astype(BF16)
    score_scale = float((B_NOPE_DIM + B_ROPE_DIM) ** -0.5 * math.log2(math.e))
    q_cat = _q_call(h, _vec_rows([norm1_g[1], sh1, sc1], B, D),
                    _pad_rows([b_q_norm_g[0]], b_q_norm_g.shape[1]),
                    b_wdq[0].astype(BF16), wq, cos_t, sin_t, score_scale)
    o = _mla_call(q_cat, k_cat, v, _diagonal_mask())
    return _ffn_call(o, h, _vec_rows([g1, norm2_g[1], sh2, sc2, g2, final_g], B, D),
                     b_wo[0].astype(BF16), win_all, conv_rows(1), wout_all, 1, final_norm=True)
```

```python
import functools
import math

import jax
import jax.numpy as jnp
import numpy as np
from jax import lax
from jax.experimental import pallas as pl
from jax.experimental.pallas import tpu as pltpu

F32 = jnp.float32
BF16 = jnp.bfloat16

CHUNK = 64
A_HEADS = 16
A_HEAD_DIM = 64
A_LEFT_CHUNKS = 8
A_MAX_REL = 2 * CHUNK
B_HEADS = 16
B_NOPE_DIM = 64
B_ROPE_DIM = 32
B_V_DIM = 64
ROPE_THETA = 10000.0
CONV_WIDTH = 3
NORM_EPS = 1e-6
NEG_INF = -1e30

LANES = 128
MXU_COLS = 256
SUBLANES_F32 = 8
SUBLANES_BF16 = 16
VMEM_BYTES = 64 * 1024 * 1024

HEAD_PAIRS = 8
HALO = SUBLANES_BF16
ROW_TILE = 512
A_QTILE = 1024
A_SUB = 2 * CHUNK
A_PREV = A_LEFT_CHUNKS * CHUNK
A_WIN = A_PREV + A_SUB
MLA_QTILE = 1024
MLA_KTILE = 256
MLA_VROWS = 64 + SUBLANES_BF16
FFN_CHUNK = MXU_COLS


def _params(vmem_mib, n_axes):
    return pltpu.CompilerParams(
        dimension_semantics=("arbitrary",) * n_axes,
        vmem_limit_bytes=vmem_mib * 1024 * 1024)


def _resident(shape):
    zeros = (0,) * len(shape)
    return pl.BlockSpec(shape, lambda *_: zeros, pipeline_mode=pl.Buffered(1))


def _resident_layer(shape, layer):
    return pl.BlockSpec((None,) + tuple(shape[1:]), lambda *_: (layer,) + (0,) * (len(shape) - 1),
                        pipeline_mode=pl.Buffered(1))


def _rms(x, g):
    return x * lax.rsqrt(jnp.mean(x * x, axis=-1, keepdims=True) + NORM_EPS) * g


def _norm_mod(x, g, shift, scale):
    return _rms(x, g) * (1.0 + scale) + shift


def _silu(x):
    return x * (1.0 / (1.0 + jnp.exp(-x)))


def _mod_body(c_ref, w_ref, b_ref, o_ref):
    ca = _silu(c_ref[...]).astype(BF16)
    o_ref[0] = jnp.dot(ca, w_ref[0].astype(BF16), preferred_element_type=F32) + b_ref[0]


def _mod_call(c_pad, w, b, tn=1024):
    L, D, N = w.shape
    return pl.pallas_call(
        _mod_body,
        grid=(L, N // tn),
        in_specs=[pl.BlockSpec((SUBLANES_F32, D), lambda l, n: (0, 0)),
                  pl.BlockSpec((1, D, tn), lambda l, n: (l, 0, n)),
                  pl.BlockSpec((1, 1, tn), lambda l, n: (l, 0, n))],
        out_specs=pl.BlockSpec((1, SUBLANES_F32, tn), lambda l, n: (l, 0, n)),
        out_shape=jax.ShapeDtypeStruct((L, SUBLANES_F32, N), F32),
        compiler_params=_params(32, 2),
        name="mod",
    )(c_pad, w, b)


def _qkv_body(x_ref, vec_ref, w_ref, qT_ref, k_ref, vT_ref, hn_ref, *, score_scale):
    vec = vec_ref[0]
    hn_ref[...] = _norm_mod(x_ref[0], vec[0:1], vec[1:2], vec[2:3]).astype(BF16)
    width = HEAD_PAIRS * LANES
    for c in range(HEAD_PAIRS // 2):
        def proj(base):
            cols = slice(base + c * MXU_COLS, base + (c + 1) * MXU_COLS)
            return jnp.dot(hn_ref[...], w_ref[:, cols], preferred_element_type=F32)
        q = proj(0) * score_scale
        k = proj(width)
        v = proj(2 * width)
        for half in range(2):
            g = slice(half * LANES, (half + 1) * LANES)
            qT_ref[0, 2 * c + half] = q[:, g].T.astype(BF16)
            k_ref[0, 2 * c + half] = k[:, g].astype(BF16)
            vT_ref[0, 2 * c + half] = v[:, g].T.astype(BF16)


def _qkv_call(x, vec, w, score_scale):
    B, S, D = x.shape
    tm = ROW_TILE
    rows = pl.BlockSpec((1, HEAD_PAIRS, tm, LANES), lambda b, i: (b, 0, i, 0))
    cols = pl.BlockSpec((1, HEAD_PAIRS, LANES, tm), lambda b, i: (b, 0, 0, i))
    return pl.pallas_call(
        functools.partial(_qkv_body, score_scale=score_scale),
        grid=(B, S // tm),
        in_specs=[pl.BlockSpec((1, tm, D), lambda b, i: (b, i, 0)),
                  pl.BlockSpec((1, SUBLANES_F32, D), lambda b, i: (b, 0, 0)),
                  _resident(w.shape)],
        out_specs=[cols, rows, cols],
        out_shape=[jax.ShapeDtypeStruct((B, HEAD_PAIRS, LANES, S), BF16),
                   jax.ShapeDtypeStruct((B, HEAD_PAIRS, S, LANES), BF16),
                   jax.ShapeDtypeStruct((B, HEAD_PAIRS, LANES, S), BF16)],
        scratch_shapes=[pltpu.VMEM((tm, D), BF16)],
        compiler_params=_params(40, 2),
        name="qkv",
    )(x, vec, w)


def _chunk_attn_body(qT_ref, kp_ref, kc_ref, vTp_ref, vTc_ref, bias_ref, o_ref, qs_ref, s_ref):
    i = pl.program_id(2)
    n_sub = A_QTILE // A_SUB
    feat = lax.broadcasted_iota(jnp.int32, (LANES, 1), 0)
    qT = qT_ref[0, 0]
    q_even = jnp.where(feat < A_HEAD_DIM, qT, jnp.zeros_like(qT))
    q_odd = jnp.where(feat >= A_HEAD_DIM, qT, jnp.zeros_like(qT))
    for j in range(n_sub):
        cols = slice(j * A_SUB, (j + 1) * A_SUB)
        qs_ref[j, :, :A_SUB] = q_even[:, cols]
        qs_ref[j, :, A_SUB:] = q_odd[:, cols]

    def window(prev_ref, cur_ref, lo, axis):
        idx = lambda s: (0, 0, s, slice(None)) if axis == 0 else (0, 0, slice(None), s)
        if lo >= A_PREV:
            return cur_ref[idx(slice(lo - A_PREV, lo + A_SUB))]
        return jnp.concatenate([prev_ref[idx(slice(lo, A_PREV))],
                                cur_ref[idx(slice(0, lo + A_SUB))]], axis=axis)

    def scores(j):
        k2 = window(kp_ref, kc_ref, j * A_SUB, 0)
        half = A_WIN // 2
        s_ref[j % 2, :half] = jnp.dot(k2[:half], qs_ref[j], preferred_element_type=F32)
        s_ref[j % 2, half:] = jnp.dot(k2[half:], qs_ref[j], preferred_element_type=F32)

    def softmax_pv(j, first_tile):
        lo = j * A_SUB
        sT = s_ref[j % 2] + bias_ref[0]
        if first_tile and lo < A_PREV:
            row = lax.broadcasted_iota(jnp.int32, (A_WIN, 1), 0)
            sT = jnp.where(row >= A_PREV - lo, sT, NEG_INF)
        m = jnp.max(sT, axis=0, keepdims=True)
        pT = jnp.exp2(sT - m)
        l = jnp.sum(pT, axis=0, keepdims=True)
        pTb = pT.astype(BF16)
        vT = window(vTp_ref, vTc_ref, lo, 1)
        pv = jnp.concatenate(
            [jnp.dot(vT[:A_HEAD_DIM], pTb[:, :A_SUB], preferred_element_type=F32),
             jnp.dot(vT[A_HEAD_DIM:], pTb[:, A_SUB:], preferred_element_type=F32)], axis=1)
        oT = pv / l
        o_ref[0, 0, lo:lo + A_SUB, :] = jnp.concatenate(
            [oT[:, :A_SUB], oT[:, A_SUB:]], axis=0).T.astype(BF16)

    def tile(first_tile):
        scores(0)
        for j in range(n_sub):
            if j + 1 < n_sub:
                scores(j + 1)
            softmax_pv(j, first_tile)

    @pl.when(i == 0)
    def _():
        tile(True)

    @pl.when(i > 0)
    def _():
        tile(False)


def _chunk_attn_call(qT, k, vT, bias):
    B, _, _, S = qT.shape
    tq = A_QTILE
    ratio = tq // A_PREV
    prev_idx = lambda i: jnp.maximum(i * ratio - 1, 0)
    rows = lambda n: (1, 1, n, LANES)
    cols = lambda n: (1, 1, LANES, n)
    return pl.pallas_call(
        _chunk_attn_body,
        grid=(HEAD_PAIRS, B, S // tq),
        in_specs=[pl.BlockSpec(cols(tq), lambda p, b, i: (b, p, 0, i)),
                  pl.BlockSpec(rows(A_PREV), lambda p, b, i: (b, p, prev_idx(i), 0)),
                  pl.BlockSpec(rows(tq), lambda p, b, i: (b, p, i, 0)),
                  pl.BlockSpec(cols(A_PREV), lambda p, b, i: (b, p, 0, prev_idx(i))),
                  pl.BlockSpec(cols(tq), lambda p, b, i: (b, p, 0, i)),
                  pl.BlockSpec((1, A_WIN, 2 * A_SUB), lambda p, b, i: (p, 0, 0))],
        out_specs=pl.BlockSpec(rows(tq), lambda p, b, i: (b, p, i, 0)),
        out_shape=jax.ShapeDtypeStruct((B, HEAD_PAIRS, S, LANES), BF16),
        scratch_shapes=[pltpu.VMEM((tq // A_SUB, LANES, 2 * A_SUB), BF16),
                        pltpu.VMEM((2, A_WIN, 2 * A_SUB), F32)],
        compiler_params=_params(32, 3),
        name="chunk_attn",
    )(qT, k, k, vT, vT, bias)


def _ffn_body(o_ref, oh_ref, h_ref, hh_ref, vec_ref, wo_ref, win_ref, cw_ref, wout_ref,
              out_ref, hn_ref, h1_ref, acc_ref, u_ref, *, final_norm):
    i = pl.program_id(1)
    tm = h_ref.shape[1]
    vec = vec_ref[0]
    g1, n2g, sh2, sc2, g2, fg = (vec[r:r + 1] for r in range(6))

    o_cat = jnp.concatenate(
        [jnp.concatenate([oh_ref[0, p] for p in range(HEAD_PAIRS)], axis=1),
         jnp.concatenate([o_ref[0, p] for p in range(HEAD_PAIRS)], axis=1)], axis=0)
    h_cat = jnp.concatenate([hh_ref[0], h_ref[0]], axis=0)
    h1 = h_cat + g1 * jnp.dot(o_cat, wo_ref[...], preferred_element_type=F32)
    hn = _norm_mod(h1, n2g, sh2, sc2)
    row = lax.broadcasted_iota(jnp.int32, (tm + HALO, 1), 0)
    hn = jnp.where(jnp.logical_and(row < HALO, i == 0), 0.0, hn)
    hn_ref[...] = hn.astype(BF16)
    h1_ref[...] = h1[HALO:]

    ffn_dim = wout_ref.shape[0]
    groups = FFN_CHUNK // LANES

    def slabs(f, base):
        return (f % 2) * 2 * groups + (groups if base >= ffn_dim else 0)

    def up(f, base):
        u = jnp.dot(hn_ref[...], win_ref[:, base:base + FFN_CHUNK], preferred_element_type=F32)
        for g in range(groups):
            u_ref[slabs(f, base) + g] = u[:, g * LANES:(g + 1) * LANES]

    def conv(f, base):
        outs = []
        for g in range(groups):
            cols = slice(base + g * LANES, base + (g + 1) * LANES)
            slab = slabs(f, base) + g
            y = cw_ref[3:4, cols] + u_ref[slab, pl.ds(HALO - 2, tm), :] * cw_ref[0:1, cols]
            y = y + u_ref[slab, pl.ds(HALO - 1, tm), :] * cw_ref[1:2, cols]
            outs.append(y + u_ref[slab, pl.ds(HALO, tm), :] * cw_ref[2:3, cols])
        return jnp.concatenate(outs, axis=1)

    def down(f, act):
        part = jnp.dot(act, wout_ref[f * FFN_CHUNK:(f + 1) * FFN_CHUNK, :],
                       preferred_element_type=F32)
        if f == 0:
            acc_ref[...] = part
        else:
            acc_ref[...] += part

    n_chunks = ffn_dim // FFN_CHUNK
    up(0, 0)
    up(0, ffn_dim)
    act = None
    for f in range(n_chunks):
        if f + 1 < n_chunks:
            up(f + 1, (f + 1) * FFN_CHUNK)
            up(f + 1, ffn_dim + (f + 1) * FFN_CHUNK)
        if f > 0:
            down(f - 1, act)
        act = (_silu(conv(f, f * FFN_CHUNK)) * conv(f, ffn_dim + f * FFN_CHUNK)).astype(BF16)
    down(n_chunks - 1, act)

    h2 = h1_ref[...] + g2 * acc_ref[...]
    out_ref[0] = _rms(h2, fg) if final_norm else h2


def _ffn_call(o, h, vec, wo, win, cw, wout, layer, final_norm):
    B, S, D = h.shape
    tm = ROW_TILE
    halo_blocks = tm // HALO
    halo_idx = lambda i: jnp.maximum(i * halo_blocks - 1, 0)
    return pl.pallas_call(
        functools.partial(_ffn_body, final_norm=final_norm),
        grid=(B, S // tm),
        in_specs=[pl.BlockSpec((1, HEAD_PAIRS, tm, LANES), lambda b, i: (b, 0, i, 0)),
                  pl.BlockSpec((1, HEAD_PAIRS, HALO, LANES), lambda b, i: (b, 0, halo_idx(i), 0)),
                  pl.BlockSpec((1, tm, D), lambda b, i: (b, i, 0)),
                  pl.BlockSpec((1, HALO, D), lambda b, i: (b, halo_idx(i), 0)),
                  pl.BlockSpec((1, SUBLANES_F32, D), lambda b, i: (b, 0, 0)),
                  _resident(wo.shape), _resident_layer(win.shape, layer), _resident(cw.shape),
                  _resident_layer(wout.shape, layer)],
        out_specs=pl.BlockSpec((1, tm, D), lambda b, i: (b, i, 0)),
        out_shape=jax.ShapeDtypeStruct((B, S, D), F32),
        scratch_shapes=[pltpu.VMEM((tm + HALO, D), BF16),
                        pltpu.VMEM((tm, D), F32),
                        pltpu.VMEM((tm, D), F32),
                        pltpu.VMEM((4 * FFN_CHUNK // LANES, tm + HALO, LANES), F32)],
        compiler_params=_params(56, 2),
        name="ffn_final" if final_norm else "ffn",
    )(o, o, h, h, vec, wo, win, cw, wout)


def _kv_body(h_ref, vec_ref, lg_ref, w1_ref, wu_ref, cos_ref, sin_ref, k_ref, v_ref):
    vec = vec_ref[0]
    hn = _norm_mod(h_ref[0], vec[0:1], vec[1:2], vec[2:3]).astype(BF16)
    t = jnp.dot(hn, w1_ref[...], preferred_element_type=F32)
    lat = lg_ref.shape[1]
    ckv = _rms(t[:, :lat], lg_ref[0:1]).astype(BF16)
    kr = (t[:, lat:lat + LANES] * cos_ref[0] + t[:, lat + LANES:] * sin_ref[0]).astype(BF16)
    kv = jnp.dot(ckv, wu_ref[...], preferred_element_type=F32)
    half = HEAD_PAIRS * LANES
    for p in range(HEAD_PAIRS):
        k_ref[0, p, :, :LANES] = kv[:, p * LANES:(p + 1) * LANES].astype(BF16)
        k_ref[0, p, :, LANES:] = kr
        vt = kv[:, half + p * LANES:half + (p + 1) * LANES].T.astype(BF16)
        ones = jnp.ones((MLA_VROWS - B_V_DIM, vt.shape[1]), BF16)
        for hh in range(2):
            v_ref[0, p, hh * MLA_VROWS:hh * MLA_VROWS + B_V_DIM] = vt[hh * B_V_DIM:(hh + 1) * B_V_DIM]
            v_ref[0, p, hh * MLA_VROWS + B_V_DIM:(hh + 1) * MLA_VROWS] = ones


def _kv_call(h, vec, lat_g, w1, wu, cos, sin):
    B, S, D = h.shape
    tm = ROW_TILE
    row = lambda b, i: (b, i, 0)
    return pl.pallas_call(
        _kv_body,
        grid=(B, S // tm),
        in_specs=[pl.BlockSpec((1, tm, D), row),
                  pl.BlockSpec((1, SUBLANES_F32, D), lambda b, i: (b, 0, 0)),
                  _resident(lat_g.shape), _resident(w1.shape), _resident(wu.shape),
                  pl.BlockSpec((1, tm, LANES), row), pl.BlockSpec((1, tm, LANES), row)],
        out_specs=[pl.BlockSpec((1, HEAD_PAIRS, tm, 2 * LANES), lambda b, i: (b, 0, i, 0)),
                   pl.BlockSpec((1, HEAD_PAIRS, 2 * MLA_VROWS, tm), lambda b, i: (b, 0, 0, i))],
        out_shape=[jax.ShapeDtypeStruct((B, HEAD_PAIRS, S, 2 * LANES), BF16),
                   jax.ShapeDtypeStruct((B, HEAD_PAIRS, 2 * MLA_VROWS, S), BF16)],
        compiler_params=_params(40, 2),
        name="shared_kv",
    )(h, vec, lat_g, w1, wu, cos, sin)


def _q_body(h_ref, vec_ref, qg_ref, wdq_ref, wq_ref, cos_ref, sin_ref, q_ref, *, score_scale):
    vec = vec_ref[0]
    hn = _norm_mod(h_ref[0], vec[0:1], vec[1:2], vec[2:3]).astype(BF16)
    cq = _rms(jnp.dot(hn, wdq_ref[...], preferred_element_type=F32), qg_ref[0:1]).astype(BF16)
    t = jnp.dot(cq, wq_ref[...], preferred_element_type=F32)
    width = HEAD_PAIRS * LANES
    cos = cos_ref[0]
    sin = sin_ref[0]
    for p in range(HEAD_PAIRS):
        g = slice(p * LANES, (p + 1) * LANES)
        rope = t[:, width:2 * width][:, g] * cos + t[:, 2 * width:][:, g] * sin
        qp = jnp.concatenate([t[:, g], rope], axis=1) * score_scale
        q_ref[0, p] = qp.T.astype(BF16)


def _q_call(h, vec, q_g, wdq, wq, cos, sin, score_scale):
    B, S, D = h.shape
    tm = ROW_TILE
    row = lambda b, i: (b, i, 0)
    return pl.pallas_call(
        functools.partial(_q_body, score_scale=score_scale),
        grid=(B, S // tm),
        in_specs=[pl.BlockSpec((1, tm, D), row),
                  pl.BlockSpec((1, SUBLANES_F32, D), lambda b, i: (b, 0, 0)),
                  _resident(q_g.shape), _resident(wdq.shape), _resident(wq.shape),
                  pl.BlockSpec((1, tm, LANES), row), pl.BlockSpec((1, tm, LANES), row)],
        out_specs=pl.BlockSpec((1, HEAD_PAIRS, 2 * LANES, tm), lambda b, i: (b, 0, 0, i)),
        out_shape=jax.ShapeDtypeStruct((B, HEAD_PAIRS, 2 * LANES, S), BF16),
        compiler_params=_params(40, 2),
        name="mla_q",
    )(h, vec, q_g, wdq, wq, cos, sin)


def _mla_body(qT_ref, k_ref, vT_ref, mask_ref, o_ref,
              qs_ref, sa_ref, sb_ref, xa_ref, xb_ref, m_ref, acc_ref):
    qi = pl.program_id(2)
    tq = qT_ref.shape[3]
    tk = sa_ref.shape[0]
    n_sub = tq // tk
    width = 2 * tk
    feat = lax.broadcasted_iota(jnp.int32, (2 * LANES, 1), 0)
    in_even = jnp.logical_or(feat < B_NOPE_DIM,
                             jnp.logical_and(feat >= LANES, feat < LANES + B_ROPE_DIM))
    in_odd = jnp.logical_and(jnp.logical_not(in_even), feat < LANES + 2 * B_ROPE_DIM)
    qT = qT_ref[0, 0]
    for t in range(n_sub):
        qt = qT[:, t * tk:(t + 1) * tk]
        qs_ref[:, t * width:t * width + tk] = jnp.where(in_even, qt, jnp.zeros_like(qt))
        qs_ref[:, t * width + tk:(t + 1) * width] = jnp.where(in_odd, qt, jnp.zeros_like(qt))
    m_ref[...] = jnp.full(m_ref.shape, NEG_INF, F32)
    acc_ref[...] = jnp.zeros(acc_ref.shape, F32)

    def scores(j, dst_ref, max_ref, first_sub=0):
        c0 = first_sub * width
        ks = k_ref[0, 0, pl.ds(pl.multiple_of(j * tk, tk), tk), :]
        sT = jnp.dot(ks, qs_ref[:, c0:], preferred_element_type=F32)
        dst_ref[:, c0:] = sT
        max_ref[:, c0:] = jnp.max(sT, axis=0, keepdims=True)

    def softmax_pv(j, src_ref, max_ref, subs, diagonal=False):
        c0, c1 = subs[0] * width, (subs[-1] + 1) * width
        sT = src_ref[:, c0:c1]
        if diagonal:
            sT = sT + mask_ref[...]
            m_cur = jnp.max(sT, axis=0, keepdims=True)
        else:
            m_cur = max_ref[:, c0:c1]
        m_old = m_ref[:, c0:c1]
        m_new = jnp.maximum(m_old, m_cur)
        alpha = jnp.exp2(m_old - m_new)
        pTb = jnp.exp2(sT - m_new).astype(BF16)
        vT = vT_ref[0, 0, :, pl.ds(pl.multiple_of(j * tk, tk), tk)]
        pv = jnp.concatenate(
            [jnp.dot(vT[hh * MLA_VROWS:(hh + 1) * MLA_VROWS],
                     pTb[:, (t - subs[0]) * width + hh * tk:(t - subs[0]) * width + (hh + 1) * tk],
                     preferred_element_type=F32)
             for t in subs for hh in range(2)], axis=1)
        acc_ref[:, c0:c1] = alpha * acc_ref[:, c0:c1] + pv
        m_ref[:, c0:c1] = m_new

    all_subs = tuple(range(n_sub))
    n = (qi + 1) * n_sub
    scores(0, sa_ref, xa_ref)

    def pair(u):
        scores(2 * u + 1, sb_ref, xb_ref)
        softmax_pv(2 * u, sa_ref, xa_ref, all_subs)
        scores(2 * u + 2, sa_ref, xa_ref)
        softmax_pv(2 * u + 1, sb_ref, xb_ref, all_subs)

    def two_pairs(u, carry):
        pair(2 * u)
        pair(2 * u + 1)
        return carry

    pairs = qi * (n_sub // 2)
    lax.fori_loop(0, pairs // 2, two_pairs, 0)
    if n_sub % 4:
        @pl.when(pairs % 2 == 1)
        def _():
            pair(pairs - 1)

    bufs = ((sa_ref, xa_ref), (sb_ref, xb_ref))
    for r in range(n_sub):
        j = n - n_sub + r
        if r + 1 < n_sub:
            scores(j + 1, *bufs[(r + 1) % 2], first_sub=r + 1)
        softmax_pv(j, *bufs[r % 2], (r,), diagonal=True)
        if r + 1 < n_sub:
            softmax_pv(j, *bufs[r % 2], tuple(range(r + 1, n_sub)))

    oT = acc_ref[:B_V_DIM, :] / acc_ref[B_V_DIM:B_V_DIM + 1, :]
    for t in range(n_sub):
        both = jnp.concatenate([oT[:, t * width:t * width + tk],
                                oT[:, t * width + tk:(t + 1) * width]], axis=0)
        o_ref[0, 0, t * tk:(t + 1) * tk, :] = both.T.astype(BF16)


def _mla_call(qT, k, vT, mask):
    B, P, _, S = qT.shape
    tq, tk = MLA_QTILE, MLA_KTILE
    assert tq % (2 * tk) == 0
    return pl.pallas_call(
        _mla_body,
        grid=(B, P, S // tq),
        in_specs=[pl.BlockSpec((1, 1, 2 * LANES, tq), lambda b, p, i: (b, p, 0, i)),
                  pl.BlockSpec((1, 1, S, 2 * LANES), lambda b, p, i: (b, p, 0, 0)),
                  pl.BlockSpec((1, 1, 2 * MLA_VROWS, S), lambda b, p, i: (b, p, 0, 0)),
                  _resident(mask.shape)],
        out_specs=pl.BlockSpec((1, 1, tq, LANES), lambda b, p, i: (b, p, i, 0)),
        out_shape=jax.ShapeDtypeStruct((B, P, S, LANES), BF16),
        scratch_shapes=[pltpu.VMEM((2 * LANES, 2 * tq), BF16),
                        pltpu.VMEM((tk, 2 * tq), F32),
                        pltpu.VMEM((tk, 2 * tq), F32),
                        pltpu.VMEM((1, 2 * tq), F32),
                        pltpu.VMEM((1, 2 * tq), F32),
                        pltpu.VMEM((1, 2 * tq), F32),
                        pltpu.VMEM((MLA_VROWS, 2 * tq), F32)],
        compiler_params=_params(40, 3),
        name="mla_attn",
    )(qT, k, vT, mask)


def _vec_rows(rows, batch, d):
    full = [jnp.broadcast_to(r, (batch, d)) for r in rows]
    full += [jnp.zeros((batch, d), F32)] * (SUBLANES_F32 - len(full))
    return jnp.stack(full, axis=1)


def _pad_rows(rows, width):
    full = [r.reshape(1, width) for r in rows]
    full += [jnp.zeros((1, width), F32)] * (SUBLANES_F32 - len(full))
    return jnp.concatenate(full, axis=0)


def _pair_rope_cols(w, heads):
    kdim = w.shape[0]
    half = B_ROPE_DIM // 2
    wh = w.reshape(kdim, heads // 2, 2 * B_ROPE_DIM)
    pad = jnp.zeros((kdim, heads // 2, LANES - 2 * B_ROPE_DIM), w.dtype)
    plain = jnp.concatenate([wh, pad], axis=-1).reshape(kdim, -1)
    w4 = w.reshape(kdim, heads, 2, half)
    sw = jnp.concatenate([w4[:, :, 1], w4[:, :, 0]], axis=-1).reshape(kdim, heads // 2, 2 * B_ROPE_DIM)
    swapped = jnp.concatenate([sw, pad], axis=-1).reshape(kdim, -1)
    return plain, swapped


def _diagonal_mask():
    kc = np.arange(MLA_KTILE)[:, None] // CHUNK
    qc = (np.arange(2 * MLA_KTILE)[None, :] % MLA_KTILE) // CHUNK
    return jnp.asarray(np.where(kc <= qc, 0.0, NEG_INF), F32)


def _bias_table(rel_bias):
    assert A_SUB == A_MAX_REL
    heads = rel_bias.shape[0]
    n, w = A_SUB, A_WIN
    far = w - 1 - A_MAX_REL
    by_dist = jnp.concatenate(
        [rel_bias[:, 1:], jnp.broadcast_to(rel_bias[:, -1:], (heads, far))], axis=1).astype(F32)
    length = n - 1 + w
    flat = jnp.tile(by_dist[:, ::-1], (1, n))[:, n - 1:n - 1 + n * (length - 1)]
    table = flat.reshape(heads, n, length - 1)[:, :, :w]
    ql = np.arange(n)[:, None]
    kl = np.arange(w)[None, :]
    band = kl // CHUNK - ql // CHUNK
    in_band = np.logical_and(band >= 0, band <= A_LEFT_CHUNKS)
    table = jnp.where(in_band[None], table * math.log2(math.e), NEG_INF)
    return jnp.swapaxes(table, 1, 2).reshape(heads // 2, 2, w, n).transpose(0, 2, 1, 3).reshape(
        heads // 2, w, 2 * n)


def kernel(x, c, positions, mod_w, mod_b, norm1_g, norm2_g, a_wqkv, a_wo, a_rel_bias, kv_mod_w, kv_mod_b, kv_norm_g, b_wdkv, b_kv_lat_norm_g, b_wuk, b_wuv, b_wkr, b_wdq, b_q_norm_g, b_wuq, b_wqr, b_wo, f_win, f_conv_w, f_conv_b, f_wout, final_g):
    B, S, D = x.shape
    depth = mod_w.shape[0]
    ffn_dim = f_wout.shape[1]

    c_pad = jnp.pad(c, ((0, SUBLANES_F32 - B), (0, 0)))
    mod = _mod_call(c_pad, mod_w, mod_b.reshape(depth, 1, 6 * D))[:, :B]
    kv_mod = _mod_call(c_pad, kv_mod_w[None], kv_mod_b.reshape(1, 1, 2 * D))[0, :B]
    mods = [[mod[l, :, k * D:(k + 1) * D] for k in range(6)] for l in range(depth)]

    half = B_ROPE_DIM // 2
    inv_freq = jnp.power(jnp.float32(ROPE_THETA),
                         -jnp.arange(half, dtype=F32) * (2.0 / B_ROPE_DIM))
    ang = positions.astype(F32)[..., None] * inv_freq
    cos, sin = jnp.cos(ang), jnp.sin(ang)
    reps = LANES // B_ROPE_DIM
    cos_t = jnp.tile(jnp.concatenate([cos, cos], axis=-1), (1, 1, reps))
    sin_t = jnp.tile(jnp.concatenate([-sin, sin], axis=-1), (1, 1, reps))

    win_all = f_win.astype(BF16)
    wout_all = f_wout.astype(BF16)

    def conv_rows(l):
        return _pad_rows([f_conv_w[l, t] for t in range(CONV_WIDTH)] + [f_conv_b[l]], 2 * ffn_dim)

    sh1, sc1, g1, sh2, sc2, g2 = mods[0]
    qT, k, vT = _qkv_call(x, _vec_rows([norm1_g[0], sh1, sc1], B, D), a_wqkv[0].astype(BF16),
                          float(A_HEAD_DIM ** -0.5 * math.log2(math.e)))
    o = _chunk_attn_call(qT, k, vT, _bias_table(a_rel_bias[0]))
    h = _ffn_call(o, x, _vec_rows([g1, norm2_g[0], sh2, sc2, g2, final_g], B, D),
                  a_wo[0].astype(BF16), win_all, conv_rows(0), wout_all, 0, final_norm=False)

    kr_plain, kr_swapped = _pair_rope_cols(jnp.concatenate([b_wkr, b_wkr], axis=1), 2)
    w1 = jnp.concatenate([b_wdkv, kr_plain, kr_swapped], axis=1).astype(BF16)
    wu = jnp.concatenate([b_wuk, b_wuv], axis=1).astype(BF16)
    k_cat, v = _kv_call(h, _vec_rows([kv_norm_g, kv_mod[:, :D], kv_mod[:, D:]], B, D),
                        _pad_rows([b_kv_lat_norm_g], b_kv_lat_norm_g.shape[0]), w1, wu,
                        cos_t, sin_t)

    sh1, sc1, g1, sh2, sc2, g2 = mods[1]
    qr_plain, qr_swapped = _pair_rope_cols(b_wqr[0], B_HEADS)
    wq = jnp.concatenate([b_wuq[0], qr_plain, qr_swapped], axis=1).astype(BF16)
    score_scale = float((B_NOPE_DIM + B_ROPE_DIM) ** -0.5 * math.log2(math.e))
    q_cat = _q_call(h, _vec_rows([norm1_g[1], sh1, sc1], B, D),
                    _pad_rows([b_q_norm_g[0]], b_q_norm_g.shape[1]),
                    b_wdq[0].astype(BF16), wq, cos_t, sin_t, score_scale)
    o = _mla_call(q_cat, k_cat, v, _diagonal_mask())
    return _ffn_call(o, h, _vec_rows([g1, norm2_g[1], sh2, sc2, g2, final_g], B, D),
                     b_wo[0].astype(BF16), win_all, conv_rows(1), wout_all, 1, final_norm=True)
```

```python
import functools
import math

import jax
import jax.numpy as jnp
import numpy as np
from jax import lax
from jax.experimental import pallas as pl
from jax.experimental.pallas import tpu as pltpu

F32 = jnp.float32
BF16 = jnp.bfloat16

CHUNK = 64
A_HEADS = 16
A_HEAD_DIM = 64
A_LEFT_CHUNKS = 8
A_MAX_REL = 2 * CHUNK
B_HEADS = 16
B_NOPE_DIM = 64
B_ROPE_DIM = 32
B_V_DIM = 64
ROPE_THETA = 10000.0
CONV_WIDTH = 3
NORM_EPS = 1e-6
NEG_INF = -1e30

LANES = 128
MXU_COLS = 256
SUBLANES_F32 = 8
SUBLANES_BF16 = 16
VMEM_BYTES = 64 * 1024 * 1024

HEAD_PAIRS = 8
HALO = SUBLANES_BF16
ROW_TILE = 512
A_QTILE = 1024
A_SUB = 2 * CHUNK
A_PREV = A_LEFT_CHUNKS * CHUNK
A_WIN = A_PREV + A_SUB
MLA_QTILE = 1024
MLA_KTILE = 256
MLA_MAX_JUMP = 64.0
MLA_VROWS = 64 + SUBLANES_BF16
FFN_CHUNK = MXU_COLS


def _params(vmem_mib, n_axes):
    return pltpu.CompilerParams(
        dimension_semantics=("arbitrary",) * n_axes,
        vmem_limit_bytes=vmem_mib * 1024 * 1024)


def _resident(shape):
    zeros = (0,) * len(shape)
    return pl.BlockSpec(shape, lambda *_: zeros, pipeline_mode=pl.Buffered(1))


def _resident_layer(shape, layer):
    return pl.BlockSpec((None,) + tuple(shape[1:]), lambda *_: (layer,) + (0,) * (len(shape) - 1),
                        pipeline_mode=pl.Buffered(1))


def _rms(x, g):
    return x * lax.rsqrt(jnp.mean(x * x, axis=-1, keepdims=True) + NORM_EPS) * g


def _norm_mod(x, g, shift, scale):
    return _rms(x, g) * (1.0 + scale) + shift


def _silu(x):
    return x * (1.0 / (1.0 + jnp.exp(-x)))


def _mod_body(c_ref, w_ref, b_ref, o_ref):
    ca = _silu(c_ref[...]).astype(BF16)
    o_ref[0] = jnp.dot(ca, w_ref[0].astype(BF16), preferred_element_type=F32) + b_ref[0]


def _mod_call(c_pad, w, b, tn=1024):
    L, D, N = w.shape
    return pl.pallas_call(
        _mod_body,
        grid=(L, N // tn),
        in_specs=[pl.BlockSpec((SUBLANES_F32, D), lambda l, n: (0, 0)),
                  pl.BlockSpec((1, D, tn), lambda l, n: (l, 0, n)),
                  pl.BlockSpec((1, 1, tn), lambda l, n: (l, 0, n))],
        out_specs=pl.BlockSpec((1, SUBLANES_F32, tn), lambda l, n: (l, 0, n)),
        out_shape=jax.ShapeDtypeStruct((L, SUBLANES_F32, N), F32),
        compiler_params=_params(32, 2),
        name="mod",
    )(c_pad, w, b)


def _qkv_body(x_ref, vec_ref, w_ref, qT_ref, k_ref, vT_ref, hn_ref, *, score_scale):
    vec = vec_ref[0]
    hn_ref[...] = _norm_mod(x_ref[0], vec[0:1], vec[1:2], vec[2:3]).astype(BF16)
    width = HEAD_PAIRS * LANES
    for c in range(HEAD_PAIRS // 2):
        def proj(base):
            cols = slice(base + c * MXU_COLS, base + (c + 1) * MXU_COLS)
            return jnp.dot(hn_ref[...], w_ref[:, cols], preferred_element_type=F32)
        q = proj(0) * score_scale
        k = proj(width)
        v = proj(2 * width)
        for half in range(2):
            g = slice(half * LANES, (half + 1) * LANES)
            qT_ref[0, 2 * c + half] = q[:, g].T.astype(BF16)
            k_ref[0, 2 * c + half] = k[:, g].astype(BF16)
            vT_ref[0, 2 * c + half] = v[:, g].T.astype(BF16)


def _qkv_call(x, vec, w, score_scale):
    B, S, D = x.shape
    tm = ROW_TILE
    rows = pl.BlockSpec((1, HEAD_PAIRS, tm, LANES), lambda b, i: (b, 0, i, 0))
    cols = pl.BlockSpec((1, HEAD_PAIRS, LANES, tm), lambda b, i: (b, 0, 0, i))
    return pl.pallas_call(
        functools.partial(_qkv_body, score_scale=score_scale),
        grid=(B, S // tm),
        in_specs=[pl.BlockSpec((1, tm, D), lambda b, i: (b, i, 0)),
                  pl.BlockSpec((1, SUBLANES_F32, D), lambda b, i: (b, 0, 0)),
                  _resident(w.shape)],
        out_specs=[cols, rows, cols],
        out_shape=[jax.ShapeDtypeStruct((B, HEAD_PAIRS, LANES, S), BF16),
                   jax.ShapeDtypeStruct((B, HEAD_PAIRS, S, LANES), BF16),
                   jax.ShapeDtypeStruct((B, HEAD_PAIRS, LANES, S), BF16)],
        scratch_shapes=[pltpu.VMEM((tm, D), BF16)],
        compiler_params=_params(40, 2),
        name="qkv",
    )(x, vec, w)


def _chunk_attn_body(qT_ref, kp_ref, kc_ref, vTp_ref, vTc_ref, bias_ref, o_ref, qs_ref, s_ref):
    i = pl.program_id(2)
    n_sub = A_QTILE // A_SUB
    feat = lax.broadcasted_iota(jnp.int32, (LANES, 1), 0)
    qT = qT_ref[0, 0]
    q_even = jnp.where(feat < A_HEAD_DIM, qT, jnp.zeros_like(qT))
    q_odd = jnp.where(feat >= A_HEAD_DIM, qT, jnp.zeros_like(qT))
    for j in range(n_sub):
        cols = slice(j * A_SUB, (j + 1) * A_SUB)
        qs_ref[j, :, :A_SUB] = q_even[:, cols]
        qs_ref[j, :, A_SUB:] = q_odd[:, cols]

    def window(prev_ref, cur_ref, lo, axis):
        idx = lambda s: (0, 0, s, slice(None)) if axis == 0 else (0, 0, slice(None), s)
        if lo >= A_PREV:
            return cur_ref[idx(slice(lo - A_PREV, lo + A_SUB))]
        return jnp.concatenate([prev_ref[idx(slice(lo, A_PREV))],
                                cur_ref[idx(slice(0, lo + A_SUB))]], axis=axis)

    def scores(j):
        k2 = window(kp_ref, kc_ref, j * A_SUB, 0)
        half = A_WIN // 2
        s_ref[j % 2, :half] = jnp.dot(k2[:half], qs_ref[j], preferred_element_type=F32)
        s_ref[j % 2, half:] = jnp.dot(k2[half:], qs_ref[j], preferred_element_type=F32)

    def softmax_pv(j, first_tile):
        lo = j * A_SUB
        sT = s_ref[j % 2] + bias_ref[0]
        if first_tile and lo < A_PREV:
            row = lax.broadcasted_iota(jnp.int32, (A_WIN, 1), 0)
            sT = jnp.where(row >= A_PREV - lo, sT, NEG_INF)
        m = jnp.max(sT, axis=0, keepdims=True)
        pT = jnp.exp2(sT - m)
        l = jnp.sum(pT, axis=0, keepdims=True)
        pTb = pT.astype(BF16)
        vT = window(vTp_ref, vTc_ref, lo, 1)
        pv = jnp.concatenate(
            [jnp.dot(vT[:A_HEAD_DIM], pTb[:, :A_SUB], preferred_element_type=F32),
             jnp.dot(vT[A_HEAD_DIM:], pTb[:, A_SUB:], preferred_element_type=F32)], axis=1)
        oT = pv / l
        o_ref[0, 0, lo:lo + A_SUB, :] = jnp.concatenate(
            [oT[:, :A_SUB], oT[:, A_SUB:]], axis=0).T.astype(BF16)

    def tile(first_tile):
        scores(0)
        for j in range(n_sub):
            if j + 1 < n_sub:
                scores(j + 1)
            softmax_pv(j, first_tile)

    @pl.when(i == 0)
    def _():
        tile(True)

    @pl.when(i > 0)
    def _():
        tile(False)


def _chunk_attn_call(qT, k, vT, bias):
    B, _, _, S = qT.shape
    tq = A_QTILE
    ratio = tq // A_PREV
    prev_idx = lambda i: jnp.maximum(i * ratio - 1, 0)
    rows = lambda n: (1, 1, n, LANES)
    cols = lambda n: (1, 1, LANES, n)
    return pl.pallas_call(
        _chunk_attn_body,
        grid=(HEAD_PAIRS, B, S // tq),
        in_specs=[pl.BlockSpec(cols(tq), lambda p, b, i: (b, p, 0, i)),
                  pl.BlockSpec(rows(A_PREV), lambda p, b, i: (b, p, prev_idx(i), 0)),
                  pl.BlockSpec(rows(tq), lambda p, b, i: (b, p, i, 0)),
                  pl.BlockSpec(cols(A_PREV), lambda p, b, i: (b, p, 0, prev_idx(i))),
                  pl.BlockSpec(cols(tq), lambda p, b, i: (b, p, 0, i)),
                  pl.BlockSpec((1, A_WIN, 2 * A_SUB), lambda p, b, i: (p, 0, 0))],
        out_specs=pl.BlockSpec(rows(tq), lambda p, b, i: (b, p, i, 0)),
        out_shape=jax.ShapeDtypeStruct((B, HEAD_PAIRS, S, LANES), BF16),
        scratch_shapes=[pltpu.VMEM((tq // A_SUB, LANES, 2 * A_SUB), BF16),
                        pltpu.VMEM((2, A_WIN, 2 * A_SUB), F32)],
        compiler_params=_params(32, 3),
        name="chunk_attn",
    )(qT, k, k, vT, vT, bias)


def _ffn_body(o_ref, oh_ref, h_ref, hh_ref, vec_ref, wo_ref, win_ref, cw_ref, wout_ref,
              out_ref, hn_ref, h1_ref, acc_ref, u_ref, *, final_norm):
    i = pl.program_id(1)
    tm = h_ref.shape[1]
    vec = vec_ref[0]
    g1, n2g, sh2, sc2, g2, fg = (vec[r:r + 1] for r in range(6))

    o_cat = jnp.concatenate(
        [jnp.concatenate([oh_ref[0, p] for p in range(HEAD_PAIRS)], axis=1),
         jnp.concatenate([o_ref[0, p] for p in range(HEAD_PAIRS)], axis=1)], axis=0)
    h_cat = jnp.concatenate([hh_ref[0], h_ref[0]], axis=0)
    h1 = h_cat + g1 * jnp.dot(o_cat, wo_ref[...], preferred_element_type=F32)
    hn = _norm_mod(h1, n2g, sh2, sc2)
    row = lax.broadcasted_iota(jnp.int32, (tm + HALO, 1), 0)
    hn = jnp.where(jnp.logical_and(row < HALO, i == 0), 0.0, hn)
    hn_ref[...] = hn.astype(BF16)
    h1_ref[...] = h1[HALO:]

    ffn_dim = wout_ref.shape[0]
    groups = FFN_CHUNK // LANES

    def slabs(f, base):
        return (f % 2) * 2 * groups + (groups if base >= ffn_dim else 0)

    def up(f, base):
        u = jnp.dot(hn_ref[...], win_ref[:, base:base + FFN_CHUNK], preferred_element_type=F32)
        for g in range(groups):
            u_ref[slabs(f, base) + g] = u[:, g * LANES:(g + 1) * LANES]

    def conv(f, base):
        outs = []
        for g in range(groups):
            cols = slice(base + g * LANES, base + (g + 1) * LANES)
            slab = slabs(f, base) + g
            y = cw_ref[3:4, cols] + u_ref[slab, pl.ds(HALO - 2, tm), :] * cw_ref[0:1, cols]
            y = y + u_ref[slab, pl.ds(HALO - 1, tm), :] * cw_ref[1:2, cols]
            outs.append(y + u_ref[slab, pl.ds(HALO, tm), :] * cw_ref[2:3, cols])
        return jnp.concatenate(outs, axis=1)

    def down(f, act):
        part = jnp.dot(act, wout_ref[f * FFN_CHUNK:(f + 1) * FFN_CHUNK, :],
                       preferred_element_type=F32)
        if f == 0:
            acc_ref[...] = part
        else:
            acc_ref[...] += part

    n_chunks = ffn_dim // FFN_CHUNK
    up(0, 0)
    up(0, ffn_dim)
    act = None
    for f in range(n_chunks):
        if f + 1 < n_chunks:
            up(f + 1, (f + 1) * FFN_CHUNK)
            up(f + 1, ffn_dim + (f + 1) * FFN_CHUNK)
        if f > 0:
            down(f - 1, act)
        act = (_silu(conv(f, f * FFN_CHUNK)) * conv(f, ffn_dim + f * FFN_CHUNK)).astype(BF16)
    down(n_chunks - 1, act)

    h2 = h1_ref[...] + g2 * acc_ref[...]
    out_ref[0] = _rms(h2, fg) if final_norm else h2


def _ffn_call(o, h, vec, wo, win, cw, wout, layer, final_norm):
    B, S, D = h.shape
    tm = ROW_TILE
    halo_blocks = tm // HALO
    halo_idx = lambda i: jnp.maximum(i * halo_blocks - 1, 0)
    return pl.pallas_call(
        functools.partial(_ffn_body, final_norm=final_norm),
        grid=(B, S // tm),
        in_specs=[pl.BlockSpec((1, HEAD_PAIRS, tm, LANES), lambda b, i: (b, 0, i, 0)),
                  pl.BlockSpec((1, HEAD_PAIRS, HALO, LANES), lambda b, i: (b, 0, halo_idx(i), 0)),
                  pl.BlockSpec((1, tm, D), lambda b, i: (b, i, 0)),
                  pl.BlockSpec((1, HALO, D), lambda b, i: (b, halo_idx(i), 0)),
                  pl.BlockSpec((1, SUBLANES_F32, D), lambda b, i: (b, 0, 0)),
                  _resident(wo.shape), _resident_layer(win.shape, layer), _resident(cw.shape),
                  _resident_layer(wout.shape, layer)],
        out_specs=pl.BlockSpec((1, tm, D), lambda b, i: (b, i, 0)),
        out_shape=jax.ShapeDtypeStruct((B, S, D), F32),
        scratch_shapes=[pltpu.VMEM((tm + HALO, D), BF16),
                        pltpu.VMEM((tm, D), F32),
                        pltpu.VMEM((tm, D), F32),
                        pltpu.VMEM((4 * FFN_CHUNK // LANES, tm + HALO, LANES), F32)],
        compiler_params=_params(56, 2),
        name="ffn_final" if final_norm else "ffn",
    )(o, o, h, h, vec, wo, win, cw, wout)


def _kv_body(h_ref, vec_ref, lg_ref, w1_ref, wu_ref, cos_ref, sin_ref, k_ref, v_ref):
    vec = vec_ref[0]
    hn = _norm_mod(h_ref[0], vec[0:1], vec[1:2], vec[2:3]).astype(BF16)
    t = jnp.dot(hn, w1_ref[...], preferred_element_type=F32)
    lat = lg_ref.shape[1]
    ckv = _rms(t[:, :lat], lg_ref[0:1]).astype(BF16)
    kr = (t[:, lat:lat + LANES] * cos_ref[0] + t[:, lat + LANES:] * sin_ref[0]).astype(BF16)
    kv = jnp.dot(ckv, wu_ref[...], preferred_element_type=F32)
    half = HEAD_PAIRS * LANES
    for p in range(HEAD_PAIRS):
        k_ref[0, p, :, :LANES] = kv[:, p * LANES:(p + 1) * LANES].astype(BF16)
        k_ref[0, p, :, LANES:] = kr
        vt = kv[:, half + p * LANES:half + (p + 1) * LANES].T.astype(BF16)
        ones = jnp.ones((MLA_VROWS - B_V_DIM, vt.shape[1]), BF16)
        for hh in range(2):
            v_ref[0, p, hh * MLA_VROWS:hh * MLA_VROWS + B_V_DIM] = vt[hh * B_V_DIM:(hh + 1) * B_V_DIM]
            v_ref[0, p, hh * MLA_VROWS + B_V_DIM:(hh + 1) * MLA_VROWS] = ones


def _kv_call(h, vec, lat_g, w1, wu, cos, sin):
    B, S, D = h.shape
    tm = ROW_TILE
    row = lambda b, i: (b, i, 0)
    return pl.pallas_call(
        _kv_body,
        grid=(B, S // tm),
        in_specs=[pl.BlockSpec((1, tm, D), row),
                  pl.BlockSpec((1, SUBLANES_F32, D), lambda b, i: (b, 0, 0)),
                  _resident(lat_g.shape), _resident(w1.shape), _resident(wu.shape),
                  pl.BlockSpec((1, tm, LANES), row), pl.BlockSpec((1, tm, LANES), row)],
        out_specs=[pl.BlockSpec((1, HEAD_PAIRS, tm, 2 * LANES), lambda b, i: (b, 0, i, 0)),
                   pl.BlockSpec((1, HEAD_PAIRS, 2 * MLA_VROWS, tm), lambda b, i: (b, 0, 0, i))],
        out_shape=[jax.ShapeDtypeStruct((B, HEAD_PAIRS, S, 2 * LANES), BF16),
                   jax.ShapeDtypeStruct((B, HEAD_PAIRS, 2 * MLA_VROWS, S), BF16)],
        compiler_params=_params(40, 2),
        name="shared_kv",
    )(h, vec, lat_g, w1, wu, cos, sin)


def _q_body(h_ref, vec_ref, qg_ref, wdq_ref, wq_ref, cos_ref, sin_ref, q_ref, *, score_scale):
    vec = vec_ref[0]
    hn = _norm_mod(h_ref[0], vec[0:1], vec[1:2], vec[2:3]).astype(BF16)
    cq = _rms(jnp.dot(hn, wdq_ref[...], preferred_element_type=F32), qg_ref[0:1]).astype(BF16)
    t = jnp.dot(cq, wq_ref[...], preferred_element_type=F32)
    width = HEAD_PAIRS * LANES
    cos = cos_ref[0]
    sin = sin_ref[0]
    for p in range(HEAD_PAIRS):
        g = slice(p * LANES, (p + 1) * LANES)
        rope = t[:, width:2 * width][:, g] * cos + t[:, 2 * width:][:, g] * sin
        qp = jnp.concatenate([t[:, g], rope], axis=1) * score_scale
        q_ref[0, p] = qp.T.astype(BF16)


def _q_call(h, vec, q_g, wdq, wq, cos, sin, score_scale):
    B, S, D = h.shape
    tm = ROW_TILE
    row = lambda b, i: (b, i, 0)
    return pl.pallas_call(
        functools.partial(_q_body, score_scale=score_scale),
        grid=(B, S // tm),
        in_specs=[pl.BlockSpec((1, tm, D), row),
                  pl.BlockSpec((1, SUBLANES_F32, D), lambda b, i: (b, 0, 0)),
                  _resident(q_g.shape), _resident(wdq.shape), _resident(wq.shape),
                  pl.BlockSpec((1, tm, LANES), row), pl.BlockSpec((1, tm, LANES), row)],
        out_specs=pl.BlockSpec((1, HEAD_PAIRS, 2 * LANES, tm), lambda b, i: (b, 0, 0, i)),
        out_shape=jax.ShapeDtypeStruct((B, HEAD_PAIRS, 2 * LANES, S), BF16),
        compiler_params=_params(40, 2),
        name="mla_q",
    )(h, vec, q_g, wdq, wq, cos, sin)


def _mla_body(qT_ref, k_ref, vT_ref, mask_ref, o_ref, qs_ref, sa_ref, xa_ref,
              pa_ref, pb_ref, aa_ref, ab_ref, m_ref, jump_ref, acc_ref):
    qi = pl.program_id(2)
    tq = qT_ref.shape[3]
    tk = sa_ref.shape[0]
    n_sub = tq // tk
    width = 2 * tk
    feat = lax.broadcasted_iota(jnp.int32, (2 * LANES, 1), 0)
    in_even = jnp.logical_or(feat < B_NOPE_DIM,
                             jnp.logical_and(feat >= LANES, feat < LANES + B_ROPE_DIM))
    in_odd = jnp.logical_and(jnp.logical_not(in_even), feat < LANES + 2 * B_ROPE_DIM)
    qT = qT_ref[0, 0]
    for t in range(n_sub):
        qt = qT[:, t * tk:(t + 1) * tk]
        qs_ref[:, t * width:t * width + tk] = jnp.where(in_even, qt, jnp.zeros_like(qt))
        qs_ref[:, t * width + tk:(t + 1) * width] = jnp.where(in_odd, qt, jnp.zeros_like(qt))
    n_full = qi * n_sub

    def keys(j):
        return k_ref[0, 0, pl.ds(pl.multiple_of(j * tk, tk), tk), :]

    def values(j):
        return vT_ref[0, 0, :, pl.ds(pl.multiple_of(j * tk, tk), tk)]

    def pv_dots(vT, p_of, subs):
        return jnp.concatenate(
            [jnp.dot(vT[hh * MLA_VROWS:(hh + 1) * MLA_VROWS], p_of(t, hh),
                     preferred_element_type=F32)
             for t in subs for hh in range(2)], axis=1)

    pbuf = ((pa_ref, aa_ref), (pb_ref, ab_ref))

    def probs(j, slot, first_sub=0, diagonal=False):
        p_ref, a_ref = pbuf[slot]
        c0 = first_sub * width
        sT = jnp.dot(keys(j), qs_ref[:, c0:], preferred_element_type=F32)
        if diagonal:
            masked = sT[:, :width] + mask_ref[...]
            sT = masked if first_sub == n_sub - 1 else jnp.concatenate([masked, sT[:, width:]], axis=1)
        m_old = m_ref[:, c0:]
        m_cur = jnp.max(sT, axis=0, keepdims=True)
        p_ref[:, c0:] = jnp.exp2(sT - m_old).astype(BF16)
        m_new = jnp.maximum(m_old, m_cur)
        a_ref[:, c0:] = jnp.exp2(m_old - m_new)
        jump_ref[:, c0:] = jnp.maximum(jump_ref[:, c0:], m_cur - m_old)
        m_ref[:, c0:] = m_new

    def pv(j, slot, first_sub=0):
        p_ref, a_ref = pbuf[slot]
        c0 = first_sub * width
        out = pv_dots(values(jnp.maximum(j, 0)),
                      lambda t, hh: p_ref[:, t * width + hh * tk:t * width + (hh + 1) * tk],
                      range(first_sub, n_sub))
        acc_ref[:, c0:] = (acc_ref[:, c0:] + out) * a_ref[:, c0:]

    acc_ref[...] = jnp.zeros(acc_ref.shape, F32)
    jump_ref[...] = jnp.zeros(jump_ref.shape, F32)
    m_ref[...] = jnp.dot(k_ref[0, 0, :SUBLANES_BF16, :], qs_ref[...],
                         preferred_element_type=F32)[:1]
    pb_ref[...] = jnp.zeros(pb_ref.shape, BF16)
    ab_ref[...] = jnp.ones(ab_ref.shape, F32)

    def four_blocks(u, carry):
        for r in range(4):
            probs(4 * u + r, r % 2)
            pv(4 * u + r - 1, (r + 1) % 2)
        return carry

    lax.fori_loop(0, n_full // 4, four_blocks, 0)
    for r in range(n_sub):
        probs(n_full + r, r % 2, first_sub=r, diagonal=True)
        pv(n_full + r - 1, (r + 1) % 2, first_sub=max(r - 1, 0))
    pv(n_full + n_sub - 1, (n_sub - 1) % 2, first_sub=n_sub - 1)

    def scores(j, first_sub=0):
        c0 = first_sub * width
        sT = jnp.dot(keys(j), qs_ref[:, c0:], preferred_element_type=F32)
        sa_ref[:, c0:] = sT
        xa_ref[:, c0:] = jnp.max(sT, axis=0, keepdims=True)

    def softmax_pv(j, subs, diagonal=False):
        c0, c1 = subs[0] * width, (subs[-1] + 1) * width
        sT = sa_ref[:, c0:c1]
        if diagonal:
            sT = sT + mask_ref[...]
            m_cur = jnp.max(sT, axis=0, keepdims=True)
        else:
            m_cur = xa_ref[:, c0:c1]
        m_old = m_ref[:, c0:c1]
        m_new = jnp.maximum(m_old, m_cur)
        alpha = jnp.exp2(m_old - m_new)
        pTb = jnp.exp2(sT - m_new).astype(BF16)
        out = pv_dots(values(j), lambda t, hh: pTb[:, (t - subs[0]) * width + hh * tk:
                                                   (t - subs[0]) * width + (hh + 1) * tk], subs)
        acc_ref[:, c0:c1] = alpha * acc_ref[:, c0:c1] + out
        m_ref[:, c0:c1] = m_new

    @pl.when(jnp.max(jump_ref[...]) > MLA_MAX_JUMP)
    def _():
        m_ref[...] = jnp.full(m_ref.shape, NEG_INF, F32)
        acc_ref[...] = jnp.zeros(acc_ref.shape, F32)

        def one_block(j, carry):
            scores(j)
            softmax_pv(j, tuple(range(n_sub)))
            return carry

        lax.fori_loop(0, n_full, one_block, 0)
        for r in range(n_sub):
            scores(n_full + r, first_sub=r)
            softmax_pv(n_full + r, (r,), diagonal=True)
            if r + 1 < n_sub:
                softmax_pv(n_full + r, tuple(range(r + 1, n_sub)))

    oT = acc_ref[:B_V_DIM, :] / acc_ref[B_V_DIM:B_V_DIM + 1, :]
    for t in range(n_sub):
        both = jnp.concatenate([oT[:, t * width:t * width + tk],
                                oT[:, t * width + tk:(t + 1) * width]], axis=0)
        o_ref[0, 0, t * tk:(t + 1) * tk, :] = both.T.astype(BF16)


def _mla_call(qT, k, vT, mask):
    B, P, _, S = qT.shape
    tq, tk = MLA_QTILE, MLA_KTILE
    assert tq % (4 * tk) == 0
    return pl.pallas_call(
        _mla_body,
        grid=(B, P, S // tq),
        in_specs=[pl.BlockSpec((1, 1, 2 * LANES, tq), lambda b, p, i: (b, p, 0, i)),
                  pl.BlockSpec((1, 1, S, 2 * LANES), lambda b, p, i: (b, p, 0, 0)),
                  pl.BlockSpec((1, 1, 2 * MLA_VROWS, S), lambda b, p, i: (b, p, 0, 0)),
                  _resident(mask.shape)],
        out_specs=pl.BlockSpec((1, 1, tq, LANES), lambda b, p, i: (b, p, i, 0)),
        out_shape=jax.ShapeDtypeStruct((B, P, S, LANES), BF16),
        scratch_shapes=[pltpu.VMEM((2 * LANES, 2 * tq), BF16),
                        pltpu.VMEM((tk, 2 * tq), F32),
                        pltpu.VMEM((1, 2 * tq), F32),
                        pltpu.VMEM((tk, 2 * tq), BF16),
                        pltpu.VMEM((tk, 2 * tq), BF16),
                        pltpu.VMEM((1, 2 * tq), F32),
                        pltpu.VMEM((1, 2 * tq), F32),
                        pltpu.VMEM((1, 2 * tq), F32),
                        pltpu.VMEM((1, 2 * tq), F32),
                        pltpu.VMEM((MLA_VROWS, 2 * tq), F32)],
        compiler_params=_params(40, 3),
        name="mla_attn",
    )(qT, k, vT, mask)


def _vec_rows(rows, batch, d):
    full = [jnp.broadcast_to(r, (batch, d)) for r in rows]
    full += [jnp.zeros((batch, d), F32)] * (SUBLANES_F32 - len(full))
    return jnp.stack(full, axis=1)


def _pad_rows(rows, width):
    full = [r.reshape(1, width) for r in rows]
    full += [jnp.zeros((1, width), F32)] * (SUBLANES_F32 - len(full))
    return jnp.concatenate(full, axis=0)


def _pair_rope_cols(w, heads):
    kdim = w.shape[0]
    half = B_ROPE_DIM // 2
    wh = w.reshape(kdim, heads // 2, 2 * B_ROPE_DIM)
    pad = jnp.zeros((kdim, heads // 2, LANES - 2 * B_ROPE_DIM), w.dtype)
    plain = jnp.concatenate([wh, pad], axis=-1).reshape(kdim, -1)
    w4 = w.reshape(kdim, heads, 2, half)
    sw = jnp.concatenate([w4[:, :, 1], w4[:, :, 0]], axis=-1).reshape(kdim, heads // 2, 2 * B_ROPE_DIM)
    swapped = jnp.concatenate([sw, pad], axis=-1).reshape(kdim, -1)
    return plain, swapped


def _diagonal_mask():
    kc = np.arange(MLA_KTILE)[:, None] // CHUNK
    qc = (np.arange(2 * MLA_KTILE)[None, :] % MLA_KTILE) // CHUNK
    return jnp.asarray(np.where(kc <= qc, 0.0, NEG_INF), F32)


def _bias_table(rel_bias):
    assert A_SUB == A_MAX_REL
    heads = rel_bias.shape[0]
    n, w = A_SUB, A_WIN
    far = w - 1 - A_MAX_REL
    by_dist = jnp.concatenate(
        [rel_bias[:, 1:], jnp.broadcast_to(rel_bias[:, -1:], (heads, far))], axis=1).astype(F32)
    length = n - 1 + w
    flat = jnp.tile(by_dist[:, ::-1], (1, n))[:, n - 1:n - 1 + n * (length - 1)]
    table = flat.reshape(heads, n, length - 1)[:, :, :w]
    ql = np.arange(n)[:, None]
    kl = np.arange(w)[None, :]
    band = kl // CHUNK - ql // CHUNK
    in_band = np.logical_and(band >= 0, band <= A_LEFT_CHUNKS)
    table = jnp.where(in_band[None], table * math.log2(math.e), NEG_INF)
    return jnp.swapaxes(table, 1, 2).reshape(heads // 2, 2, w, n).transpose(0, 2, 1, 3).reshape(
        heads // 2, w, 2 * n)


def kernel(x, c, positions, mod_w, mod_b, norm1_g, norm2_g, a_wqkv, a_wo, a_rel_bias, kv_mod_w, kv_mod_b, kv_norm_g, b_wdkv, b_kv_lat_norm_g, b_wuk, b_wuv, b_wkr, b_wdq, b_q_norm_g, b_wuq, b_wqr, b_wo, f_win, f_conv_w, f_conv_b, f_wout, final_g):
    B, S, D = x.shape
    depth = mod_w.shape[0]
    ffn_dim = f_wout.shape[1]

    c_pad = jnp.pad(c, ((0, SUBLANES_F32 - B), (0, 0)))
    mod = _mod_call(c_pad, mod_w, mod_b.reshape(depth, 1, 6 * D))[:, :B]
    kv_mod = _mod_call(c_pad, kv_mod_w[None], kv_mod_b.reshape(1, 1, 2 * D))[0, :B]
    mods = [[mod[l, :, k * D:(k + 1) * D] for k in range(6)] for l in range(depth)]

    half = B_ROPE_DIM // 2
    inv_freq = jnp.power(jnp.float32(ROPE_THETA),
                         -jnp.arange(half, dtype=F32) * (2.0 / B_ROPE_DIM))
    ang = positions.astype(F32)[..., None] * inv_freq
    cos, sin = jnp.cos(ang), jnp.sin(ang)
    reps = LANES // B_ROPE_DIM
    cos_t = jnp.tile(jnp.concatenate([cos, cos], axis=-1), (1, 1, reps))
    sin_t = jnp.tile(jnp.concatenate([-sin, sin], axis=-1), (1, 1, reps))

    win_all = f_win.astype(BF16)
    wout_all = f_wout.astype(BF16)

    def conv_rows(l):
        return _pad_rows([f_conv_w[l, t] for t in range(CONV_WIDTH)] + [f_conv_b[l]], 2 * ffn_dim)

    sh1, sc1, g1, sh2, sc2, g2 = mods[0]
    qT, k, vT = _qkv_call(x, _vec_rows([norm1_g[0], sh1, sc1], B, D), a_wqkv[0].astype(BF16),
                          float(A_HEAD_DIM ** -0.5 * math.log2(math.e)))
    o = _chunk_attn_call(qT, k, vT, _bias_table(a_rel_bias[0]))
    h = _ffn_call(o, x, _vec_rows([g1, norm2_g[0], sh2, sc2, g2, final_g], B, D),
                  a_wo[0].astype(BF16), win_all, conv_rows(0), wout_all, 0, final_norm=False)

    kr_plain, kr_swapped = _pair_rope_cols(jnp.concatenate([b_wkr, b_wkr], axis=1), 2)
    w1 = jnp.concatenate([b_wdkv, kr_plain, kr_swapped], axis=1).astype(BF16)
    wu = jnp.concatenate([b_wuk, b_wuv], axis=1).astype(BF16)
    k_cat, v = _kv_call(h, _vec_rows([kv_norm_g, kv_mod[:, :D], kv_mod[:, D:]], B, D),
                        _pad_rows([b_kv_lat_norm_g], b_kv_lat_norm_g.shape[0]), w1, wu,
                        cos_t, sin_t)

    sh1, sc1, g1, sh2, sc2, g2 = mods[1]
    qr_plain, qr_swapped = _pair_rope_cols(b_wqr[0], B_HEADS)
    wq = jnp.concatenate([b_wuq[0], qr_plain, qr_swapped], axis=1).astype(BF16)
    score_scale = float((B_NOPE_DIM + B_ROPE_DIM) ** -0.5 * math.log2(math.e))
    q_cat = _q_call(h, _vec_rows([norm1_g[1], sh1, sc1], B, D),
                    _pad_rows([b_q_norm_g[0]], b_q_norm_g.shape[1]),
                    b_wdq[0].astype(BF16), wq, cos_t, sin_t, score_scale)
    o = _mla_call(q_cat, k_cat, v, _diagonal_mask())
    return _ffn_call(o, h, _vec_rows([g1, norm2_g[1], sh2, sc2, g2, final_g], B, D),
                     b_wo[0].astype(BF16), win_all, conv_rows(1), wout_all, 1, final_norm=True)
```

```python
import functools
import math

import jax
import jax.numpy as jnp
import numpy as np
from jax import lax
from jax.experimental import pallas as pl
from jax.experimental.pallas import tpu as pltpu

F32 = jnp.float32
BF16 = jnp.bfloat16

CHUNK = 64
A_HEADS = 16
A_HEAD_DIM = 64
A_LEFT_CHUNKS = 8
A_MAX_REL = 2 * CHUNK
B_HEADS = 16
B_NOPE_DIM = 64
B_ROPE_DIM = 32
B_V_DIM = 64
ROPE_THETA = 10000.0
CONV_WIDTH = 3
NORM_EPS = 1e-6
NEG_INF = -1e30

LANES = 128
MXU_COLS = 256
SUBLANES_F32 = 8
SUBLANES_BF16 = 16
VMEM_BYTES = 64 * 1024 * 1024

HEAD_PAIRS = 8
HALO = SUBLANES_BF16
ROW_TILE = 512
PROJ_TILE = 1024
A_QTILE = 1024
A_SUB = 4 * CHUNK
A_PREV = A_LEFT_CHUNKS * CHUNK
A_WIN = A_PREV + A_SUB
MLA_QTILE = 2048
MLA_KTILE = 512
MLA_MAX_JUMP = 64.0
MLA_VROWS = 64 + SUBLANES_BF16
FFN_CHUNK = MXU_COLS


def _params(vmem_mib, n_axes):
    return pltpu.CompilerParams(
        dimension_semantics=("arbitrary",) * n_axes,
        vmem_limit_bytes=vmem_mib * 1024 * 1024)


def _resident(shape):
    zeros = (0,) * len(shape)
    return pl.BlockSpec(shape, lambda *_: zeros, pipeline_mode=pl.Buffered(1))


def _resident_layer(shape, layer):
    return pl.BlockSpec((None,) + tuple(shape[1:]), lambda *_: (layer,) + (0,) * (len(shape) - 1),
                        pipeline_mode=pl.Buffered(1))


def _rms(x, g):
    return x * lax.rsqrt(jnp.mean(x * x, axis=-1, keepdims=True) + NORM_EPS) * g


def _norm_mod(x, g, shift, scale):
    return _rms(x, g) * (1.0 + scale) + shift


def _silu(x):
    return x * (1.0 / (1.0 + jnp.exp(-x)))


def _mod_body(c_ref, w_ref, b_ref, o_ref):
    ca = _silu(c_ref[...]).astype(BF16)
    o_ref[0] = jnp.dot(ca, w_ref[0].astype(BF16), preferred_element_type=F32) + b_ref[0]


def _mod_call(c_pad, w, b, tn=1024):
    L, D, N = w.shape
    return pl.pallas_call(
        _mod_body,
        grid=(L, N // tn),
        in_specs=[pl.BlockSpec((SUBLANES_F32, D), lambda l, n: (0, 0)),
                  pl.BlockSpec((1, D, tn), lambda l, n: (l, 0, n)),
                  pl.BlockSpec((1, 1, tn), lambda l, n: (l, 0, n))],
        out_specs=pl.BlockSpec((1, SUBLANES_F32, tn), lambda l, n: (l, 0, n)),
        out_shape=jax.ShapeDtypeStruct((L, SUBLANES_F32, N), F32),
        compiler_params=_params(32, 2),
        name="mod",
    )(c_pad, w, b)


def _qkv_body(x_ref, vec_ref, w_ref, qT_ref, k_ref, vT_ref, hn_ref, *, score_scale):
    vec = vec_ref[0]
    hn_ref[...] = _norm_mod(x_ref[0], vec[0:1], vec[1:2], vec[2:3]).astype(BF16)
    width = HEAD_PAIRS * LANES
    for c in range(HEAD_PAIRS // 2):
        def proj(base):
            cols = slice(base + c * MXU_COLS, base + (c + 1) * MXU_COLS)
            return jnp.dot(hn_ref[...], w_ref[:, cols], preferred_element_type=F32)
        q = proj(0) * score_scale
        k = proj(width)
        v = proj(2 * width)
        for half in range(2):
            g = slice(half * LANES, (half + 1) * LANES)
            qT_ref[0, 2 * c + half] = q[:, g].T.astype(BF16)
            k_ref[0, 2 * c + half] = k[:, g].astype(BF16)
            vT_ref[0, 2 * c + half] = v[:, g].T.astype(BF16)


def _qkv_call(x, vec, w, score_scale):
    B, S, D = x.shape
    tm = PROJ_TILE
    rows = pl.BlockSpec((1, HEAD_PAIRS, tm, LANES), lambda b, i: (b, 0, i, 0))
    cols = pl.BlockSpec((1, HEAD_PAIRS, LANES, tm), lambda b, i: (b, 0, 0, i))
    return pl.pallas_call(
        functools.partial(_qkv_body, score_scale=score_scale),
        grid=(B, S // tm),
        in_specs=[pl.BlockSpec((1, tm, D), lambda b, i: (b, i, 0)),
                  pl.BlockSpec((1, SUBLANES_F32, D), lambda b, i: (b, 0, 0)),
                  _resident(w.shape)],
        out_specs=[cols, rows, cols],
        out_shape=[jax.ShapeDtypeStruct((B, HEAD_PAIRS, LANES, S), BF16),
                   jax.ShapeDtypeStruct((B, HEAD_PAIRS, S, LANES), BF16),
                   jax.ShapeDtypeStruct((B, HEAD_PAIRS, LANES, S), BF16)],
        scratch_shapes=[pltpu.VMEM((tm, D), BF16)],
        compiler_params=_params(40, 2),
        name="qkv",
    )(x, vec, w)


def _chunk_attn_body(qT_ref, kp_ref, kc_ref, vTp_ref, vTc_ref, bias_ref, o_ref, qs_ref, s_ref):
    i = pl.program_id(2)
    n_sub = A_QTILE // A_SUB
    feat = lax.broadcasted_iota(jnp.int32, (LANES, 1), 0)
    qT = qT_ref[0, 0]
    q_even = jnp.where(feat < A_HEAD_DIM, qT, jnp.zeros_like(qT))
    q_odd = jnp.where(feat >= A_HEAD_DIM, qT, jnp.zeros_like(qT))
    for j in range(n_sub):
        cols = slice(j * A_SUB, (j + 1) * A_SUB)
        qs_ref[j, :, :A_SUB] = q_even[:, cols]
        qs_ref[j, :, A_SUB:] = q_odd[:, cols]

    def window(prev_ref, cur_ref, lo, axis):
        idx = lambda s: (0, 0, s, slice(None)) if axis == 0 else (0, 0, slice(None), s)
        if lo >= A_PREV:
            return cur_ref[idx(slice(lo - A_PREV, lo + A_SUB))]
        return jnp.concatenate([prev_ref[idx(slice(lo, A_PREV))],
                                cur_ref[idx(slice(0, lo + A_SUB))]], axis=axis)

    def scores(j):
        k2 = window(kp_ref, kc_ref, j * A_SUB, 0)
        half = A_WIN // 2
        s_ref[j % 2, :half] = jnp.dot(k2[:half], qs_ref[j], preferred_element_type=F32)
        s_ref[j % 2, half:] = jnp.dot(k2[half:], qs_ref[j], preferred_element_type=F32)

    def softmax_pv(j, first_tile):
        lo = j * A_SUB
        sT = s_ref[j % 2] + bias_ref[0]
        if first_tile and lo < A_PREV:
            row = lax.broadcasted_iota(jnp.int32, (A_WIN, 1), 0)
            sT = jnp.where(row >= A_PREV - lo, sT, NEG_INF)
        m = jnp.max(sT, axis=0, keepdims=True)
        pT = jnp.exp2(sT - m)
        l = jnp.sum(pT, axis=0, keepdims=True)
        pTb = pT.astype(BF16)
        vT = window(vTp_ref, vTc_ref, lo, 1)
        pv = jnp.concatenate(
            [jnp.dot(vT[:A_HEAD_DIM], pTb[:, :A_SUB], preferred_element_type=F32),
             jnp.dot(vT[A_HEAD_DIM:], pTb[:, A_SUB:], preferred_element_type=F32)], axis=1)
        oT = pv / l
        o_ref[0, 0, lo:lo + A_SUB, :] = jnp.concatenate(
            [oT[:, :A_SUB], oT[:, A_SUB:]], axis=0).T.astype(BF16)

    def tile(first_tile):
        scores(0)
        for j in range(n_sub):
            if j + 1 < n_sub:
                scores(j + 1)
            softmax_pv(j, first_tile)

    @pl.when(i == 0)
    def _():
        tile(True)

    @pl.when(i > 0)
    def _():
        tile(False)


def _chunk_attn_call(qT, k, vT, bias):
    B, _, _, S = qT.shape
    tq = A_QTILE
    ratio = tq // A_PREV
    prev_idx = lambda i: jnp.maximum(i * ratio - 1, 0)
    rows = lambda n: (1, 1, n, LANES)
    cols = lambda n: (1, 1, LANES, n)
    return pl.pallas_call(
        _chunk_attn_body,
        grid=(HEAD_PAIRS, B, S // tq),
        in_specs=[pl.BlockSpec(cols(tq), lambda p, b, i: (b, p, 0, i)),
                  pl.BlockSpec(rows(A_PREV), lambda p, b, i: (b, p, prev_idx(i), 0)),
                  pl.BlockSpec(rows(tq), lambda p, b, i: (b, p, i, 0)),
                  pl.BlockSpec(cols(A_PREV), lambda p, b, i: (b, p, 0, prev_idx(i))),
                  pl.BlockSpec(cols(tq), lambda p, b, i: (b, p, 0, i)),
                  pl.BlockSpec((1, A_WIN, 2 * A_SUB), lambda p, b, i: (p, 0, 0))],
        out_specs=pl.BlockSpec(rows(tq), lambda p, b, i: (b, p, i, 0)),
        out_shape=jax.ShapeDtypeStruct((B, HEAD_PAIRS, S, LANES), BF16),
        scratch_shapes=[pltpu.VMEM((tq // A_SUB, LANES, 2 * A_SUB), BF16),
                        pltpu.VMEM((2, A_WIN, 2 * A_SUB), F32)],
        compiler_params=_params(32, 3),
        name="chunk_attn",
    )(qT, k, k, vT, vT, bias)


def _ffn_body(o_ref, oh_ref, h_ref, hh_ref, vec_ref, wo_ref, win_ref, cw_ref, wout_ref,
              out_ref, hn_ref, h1_ref, acc_ref, u_ref, *, final_norm):
    i = pl.program_id(1)
    tm = h_ref.shape[1]
    vec = vec_ref[0]
    g1, n2g, sh2, sc2, g2, fg = (vec[r:r + 1] for r in range(6))

    o_cat = jnp.concatenate(
        [jnp.concatenate([oh_ref[0, p] for p in range(HEAD_PAIRS)], axis=1),
         jnp.concatenate([o_ref[0, p] for p in range(HEAD_PAIRS)], axis=1)], axis=0)
    h_cat = jnp.concatenate([hh_ref[0], h_ref[0]], axis=0)
    h1 = h_cat + g1 * jnp.dot(o_cat, wo_ref[...], preferred_element_type=F32)
    hn = _norm_mod(h1, n2g, sh2, sc2)
    row = lax.broadcasted_iota(jnp.int32, (tm + HALO, 1), 0)
    hn = jnp.where(jnp.logical_and(row < HALO, i == 0), 0.0, hn)
    hn_ref[...] = hn.astype(BF16)
    h1_ref[...] = h1[HALO:]

    ffn_dim = wout_ref.shape[0]
    groups = FFN_CHUNK // LANES

    def slabs(f, base):
        return (f % 2) * 2 * groups + (groups if base >= ffn_dim else 0)

    def up(f, base):
        u = jnp.dot(hn_ref[...], win_ref[:, base:base + FFN_CHUNK], preferred_element_type=F32)
        for g in range(groups):
            u_ref[slabs(f, base) + g] = u[:, g * LANES:(g + 1) * LANES]

    def conv(f, base):
        outs = []
        for g in range(groups):
            cols = slice(base + g * LANES, base + (g + 1) * LANES)
            slab = slabs(f, base) + g
            y = cw_ref[3:4, cols] + u_ref[slab, pl.ds(HALO - 2, tm), :] * cw_ref[0:1, cols]
            y = y + u_ref[slab, pl.ds(HALO - 1, tm), :] * cw_ref[1:2, cols]
            outs.append(y + u_ref[slab, pl.ds(HALO, tm), :] * cw_ref[2:3, cols])
        return jnp.concatenate(outs, axis=1)

    def down(f, act):
        part = jnp.dot(act, wout_ref[f * FFN_CHUNK:(f + 1) * FFN_CHUNK, :],
                       preferred_element_type=F32)
        if f == 0:
            acc_ref[...] = part
        else:
            acc_ref[...] += part

    n_chunks = ffn_dim // FFN_CHUNK
    up(0, 0)
    up(0, ffn_dim)
    act = None
    for f in range(n_chunks):
        if f + 1 < n_chunks:
            up(f + 1, (f + 1) * FFN_CHUNK)
            up(f + 1, ffn_dim + (f + 1) * FFN_CHUNK)
        if f > 0:
            down(f - 1, act)
        act = (_silu(conv(f, f * FFN_CHUNK)) * conv(f, ffn_dim + f * FFN_CHUNK)).astype(BF16)
    down(n_chunks - 1, act)

    h2 = h1_ref[...] + g2 * acc_ref[...]
    out_ref[0] = _rms(h2, fg) if final_norm else h2


def _ffn_call(o, h, vec, wo, win, cw, wout, layer, final_norm):
    B, S, D = h.shape
    tm = ROW_TILE
    halo_blocks = tm // HALO
    halo_idx = lambda i: jnp.maximum(i * halo_blocks - 1, 0)
    return pl.pallas_call(
        functools.partial(_ffn_body, final_norm=final_norm),
        grid=(B, S // tm),
        in_specs=[pl.BlockSpec((1, HEAD_PAIRS, tm, LANES), lambda b, i: (b, 0, i, 0)),
                  pl.BlockSpec((1, HEAD_PAIRS, HALO, LANES), lambda b, i: (b, 0, halo_idx(i), 0)),
                  pl.BlockSpec((1, tm, D), lambda b, i: (b, i, 0)),
                  pl.BlockSpec((1, HALO, D), lambda b, i: (b, halo_idx(i), 0)),
                  pl.BlockSpec((1, SUBLANES_F32, D), lambda b, i: (b, 0, 0)),
                  _resident(wo.shape), _resident_layer(win.shape, layer), _resident(cw.shape),
                  _resident_layer(wout.shape, layer)],
        out_specs=pl.BlockSpec((1, tm, D), lambda b, i: (b, i, 0)),
        out_shape=jax.ShapeDtypeStruct((B, S, D), F32),
        scratch_shapes=[pltpu.VMEM((tm + HALO, D), BF16),
                        pltpu.VMEM((tm, D), F32),
                        pltpu.VMEM((tm, D), F32),
                        pltpu.VMEM((4 * FFN_CHUNK // LANES, tm + HALO, LANES), F32)],
        compiler_params=_params(56, 2),
        name="ffn_final" if final_norm else "ffn",
    )(o, o, h, h, vec, wo, win, cw, wout)


def _kv_body(h_ref, vec_ref, lg_ref, w1_ref, wu_ref, cos_ref, sin_ref, k_ref, v_ref):
    vec = vec_ref[0]
    hn = _norm_mod(h_ref[0], vec[0:1], vec[1:2], vec[2:3]).astype(BF16)
    t = jnp.dot(hn, w1_ref[...], preferred_element_type=F32)
    lat = lg_ref.shape[1]
    ckv = _rms(t[:, :lat], lg_ref[0:1]).astype(BF16)
    kr = (t[:, lat:lat + LANES] * cos_ref[0] + t[:, lat + LANES:] * sin_ref[0]).astype(BF16)
    kv = jnp.dot(ckv, wu_ref[...], preferred_element_type=F32)
    half = HEAD_PAIRS * LANES
    for p in range(HEAD_PAIRS):
        k_ref[0, p, :, :LANES] = kv[:, p * LANES:(p + 1) * LANES].astype(BF16)
        k_ref[0, p, :, LANES:] = kr
        vt = kv[:, half + p * LANES:half + (p + 1) * LANES].T.astype(BF16)
        ones = jnp.ones((MLA_VROWS - B_V_DIM, vt.shape[1]), BF16)
        for hh in range(2):
            v_ref[0, p, hh * MLA_VROWS:hh * MLA_VROWS + B_V_DIM] = vt[hh * B_V_DIM:(hh + 1) * B_V_DIM]
            v_ref[0, p, hh * MLA_VROWS + B_V_DIM:(hh + 1) * MLA_VROWS] = ones


def _kv_call(h, vec, lat_g, w1, wu, cos, sin):
    B, S, D = h.shape
    tm = PROJ_TILE
    row = lambda b, i: (b, i, 0)
    return pl.pallas_call(
        _kv_body,
        grid=(B, S // tm),
        in_specs=[pl.BlockSpec((1, tm, D), row),
                  pl.BlockSpec((1, SUBLANES_F32, D), lambda b, i: (b, 0, 0)),
                  _resident(lat_g.shape), _resident(w1.shape), _resident(wu.shape),
                  pl.BlockSpec((1, tm, LANES), row), pl.BlockSpec((1, tm, LANES), row)],
        out_specs=[pl.BlockSpec((1, HEAD_PAIRS, tm, 2 * LANES), lambda b, i: (b, 0, i, 0)),
                   pl.BlockSpec((1, HEAD_PAIRS, 2 * MLA_VROWS, tm), lambda b, i: (b, 0, 0, i))],
        out_shape=[jax.ShapeDtypeStruct((B, HEAD_PAIRS, S, 2 * LANES), BF16),
                   jax.ShapeDtypeStruct((B, HEAD_PAIRS, 2 * MLA_VROWS, S), BF16)],
        compiler_params=_params(48, 2),
        name="shared_kv",
    )(h, vec, lat_g, w1, wu, cos, sin)


def _q_body(h_ref, vec_ref, qg_ref, wdq_ref, wq_ref, cos_ref, sin_ref, q_ref, *, score_scale):
    vec = vec_ref[0]
    hn = _norm_mod(h_ref[0], vec[0:1], vec[1:2], vec[2:3]).astype(BF16)
    cq = _rms(jnp.dot(hn, wdq_ref[...], preferred_element_type=F32), qg_ref[0:1]).astype(BF16)
    t = jnp.dot(cq, wq_ref[...], preferred_element_type=F32)
    width = HEAD_PAIRS * LANES
    cos = cos_ref[0]
    sin = sin_ref[0]
    for p in range(HEAD_PAIRS):
        g = slice(p * LANES, (p + 1) * LANES)
        rope = t[:, width:2 * width][:, g] * cos + t[:, 2 * width:][:, g] * sin
        qp = jnp.concatenate([t[:, g], rope], axis=1) * score_scale
        q_ref[0, p] = qp.T.astype(BF16)


def _q_call(h, vec, q_g, wdq, wq, cos, sin, score_scale):
    B, S, D = h.shape
    tm = PROJ_TILE
    row = lambda b, i: (b, i, 0)
    return pl.pallas_call(
        functools.partial(_q_body, score_scale=score_scale),
        grid=(B, S // tm),
        in_specs=[pl.BlockSpec((1, tm, D), row),
                  pl.BlockSpec((1, SUBLANES_F32, D), lambda b, i: (b, 0, 0)),
                  _resident(q_g.shape), _resident(wdq.shape), _resident(wq.shape),
                  pl.BlockSpec((1, tm, LANES), row), pl.BlockSpec((1, tm, LANES), row)],
        out_specs=pl.BlockSpec((1, HEAD_PAIRS, 2 * LANES, tm), lambda b, i: (b, 0, 0, i)),
        out_shape=jax.ShapeDtypeStruct((B, HEAD_PAIRS, 2 * LANES, S), BF16),
        compiler_params=_params(52, 2),
        name="mla_q",
    )(h, vec, q_g, wdq, wq, cos, sin)


def _mla_body(qT_ref, k_ref, vT_ref, mask_ref, o_ref, qs_ref, sa_ref, xa_ref,
              pa_ref, pb_ref, aa_ref, ab_ref, m_ref, jump_ref, acc_ref):
    qi = pl.program_id(2)
    tq = qT_ref.shape[3]
    tk = sa_ref.shape[0]
    n_sub = tq // tk
    width = 2 * tk
    feat = lax.broadcasted_iota(jnp.int32, (2 * LANES, 1), 0)
    in_even = jnp.logical_or(feat < B_NOPE_DIM,
                             jnp.logical_and(feat >= LANES, feat < LANES + B_ROPE_DIM))
    in_odd = jnp.logical_and(jnp.logical_not(in_even), feat < LANES + 2 * B_ROPE_DIM)
    qT = qT_ref[0, 0]
    for t in range(n_sub):
        qt = qT[:, t * tk:(t + 1) * tk]
        qs_ref[:, t * width:t * width + tk] = jnp.where(in_even, qt, jnp.zeros_like(qt))
        qs_ref[:, t * width + tk:(t + 1) * width] = jnp.where(in_odd, qt, jnp.zeros_like(qt))
    n_full = qi * n_sub

    def keys(j):
        return k_ref[0, 0, pl.ds(pl.multiple_of(j * tk, tk), tk), :]

    def values(j):
        return vT_ref[0, 0, :, pl.ds(pl.multiple_of(j * tk, tk), tk)]

    def pv_dots(vT, p_of, subs):
        return jnp.concatenate(
            [jnp.dot(vT[hh * MLA_VROWS:(hh + 1) * MLA_VROWS], p_of(t, hh),
                     preferred_element_type=F32)
             for t in subs for hh in range(2)], axis=1)

    pbuf = ((pa_ref, aa_ref), (pb_ref, ab_ref))

    def probs(j, slot, first_sub=0, diagonal=False):
        p_ref, a_ref = pbuf[slot]
        c0 = first_sub * width
        sT = jnp.dot(keys(j), qs_ref[:, c0:], preferred_element_type=F32)
        if diagonal:
            masked = sT[:, :width] + mask_ref[...]
            sT = masked if first_sub == n_sub - 1 else jnp.concatenate([masked, sT[:, width:]], axis=1)
        m_old = m_ref[:, c0:]
        m_cur = jnp.max(sT, axis=0, keepdims=True)
        p_ref[:, c0:] = jnp.exp2(sT - m_old).astype(BF16)
        m_new = jnp.maximum(m_old, m_cur)
        a_ref[:, c0:] = jnp.exp2(m_old - m_new)
        jump_ref[:, c0:] = jnp.maximum(jump_ref[:, c0:], m_cur - m_old)
        m_ref[:, c0:] = m_new

    def pv(j, slot, first_sub=0):
        p_ref, a_ref = pbuf[slot]
        c0 = first_sub * width
        out = pv_dots(values(jnp.maximum(j, 0)),
                      lambda t, hh: p_ref[:, t * width + hh * tk:t * width + (hh + 1) * tk],
                      range(first_sub, n_sub))
        acc_ref[:, c0:] = (acc_ref[:, c0:] + out) * a_ref[:, c0:]

    acc_ref[...] = jnp.zeros(acc_ref.shape, F32)
    jump_ref[...] = jnp.zeros(jump_ref.shape, F32)
    m_ref[...] = jnp.dot(k_ref[0, 0, :SUBLANES_BF16, :], qs_ref[...],
                         preferred_element_type=F32)[:1]
    pb_ref[...] = jnp.zeros(pb_ref.shape, BF16)
    ab_ref[...] = jnp.ones(ab_ref.shape, F32)

    def four_blocks(u, carry):
        for r in range(4):
            probs(4 * u + r, r % 2)
            pv(4 * u + r - 1, (r + 1) % 2)
        return carry

    lax.fori_loop(0, n_full // 4, four_blocks, 0)
    for r in range(n_sub):
        probs(n_full + r, r % 2, first_sub=r, diagonal=True)
        pv(n_full + r - 1, (r + 1) % 2, first_sub=max(r - 1, 0))
    pv(n_full + n_sub - 1, (n_sub - 1) % 2, first_sub=n_sub - 1)

    def scores(j, first_sub=0):
        c0 = first_sub * width
        sT = jnp.dot(keys(j), qs_ref[:, c0:], preferred_element_type=F32)
        sa_ref[:, c0:] = sT
        xa_ref[:, c0:] = jnp.max(sT, axis=0, keepdims=True)

    def softmax_pv(j, subs, diagonal=False):
        c0, c1 = subs[0] * width, (subs[-1] + 1) * width
        sT = sa_ref[:, c0:c1]
        if diagonal:
            sT = sT + mask_ref[...]
            m_cur = jnp.max(sT, axis=0, keepdims=True)
        else:
            m_cur = xa_ref[:, c0:c1]
        m_old = m_ref[:, c0:c1]
        m_new = jnp.maximum(m_old, m_cur)
        alpha = jnp.exp2(m_old - m_new)
        pTb = jnp.exp2(sT - m_new).astype(BF16)
        out = pv_dots(values(j), lambda t, hh: pTb[:, (t - subs[0]) * width + hh * tk:
                                                   (t - subs[0]) * width + (hh + 1) * tk], subs)
        acc_ref[:, c0:c1] = alpha * acc_ref[:, c0:c1] + out
        m_ref[:, c0:c1] = m_new

    @pl.when(jnp.max(jump_ref[...]) > MLA_MAX_JUMP)
    def _():
        m_ref[...] = jnp.full(m_ref.shape, NEG_INF, F32)
        acc_ref[...] = jnp.zeros(acc_ref.shape, F32)

        def one_block(j, carry):
            scores(j)
            softmax_pv(j, tuple(range(n_sub)))
            return carry

        lax.fori_loop(0, n_full, one_block, 0)
        for r in range(n_sub):
            scores(n_full + r, first_sub=r)
            softmax_pv(n_full + r, (r,), diagonal=True)
            if r + 1 < n_sub:
                softmax_pv(n_full + r, tuple(range(r + 1, n_sub)))

    oT = acc_ref[:B_V_DIM, :] / acc_ref[B_V_DIM:B_V_DIM + 1, :]
    for t in range(n_sub):
        both = jnp.concatenate([oT[:, t * width:t * width + tk],
                                oT[:, t * width + tk:(t + 1) * width]], axis=0)
        o_ref[0, 0, t * tk:(t + 1) * tk, :] = both.T.astype(BF16)


def _mla_call(qT, k, vT, mask):
    B, P, _, S = qT.shape
    tq, tk = MLA_QTILE, MLA_KTILE
    assert tq % (4 * tk) == 0
    return pl.pallas_call(
        _mla_body,
        grid=(B, P, S // tq),
        in_specs=[pl.BlockSpec((1, 1, 2 * LANES, tq), lambda b, p, i: (b, p, 0, i)),
                  pl.BlockSpec((1, 1, S, 2 * LANES), lambda b, p, i: (b, p, 0, 0)),
                  pl.BlockSpec((1, 1, 2 * MLA_VROWS, S), lambda b, p, i: (b, p, 0, 0)),
                  _resident(mask.shape)],
        out_specs=pl.BlockSpec((1, 1, tq, LANES), lambda b, p, i: (b, p, i, 0)),
        out_shape=jax.ShapeDtypeStruct((B, P, S, LANES), BF16),
        scratch_shapes=[pltpu.VMEM((2 * LANES, 2 * tq), BF16),
                        pltpu.VMEM((tk, 2 * tq), F32),
                        pltpu.VMEM((1, 2 * tq), F32),
                        pltpu.VMEM((tk, 2 * tq), BF16),
                        pltpu.VMEM((tk, 2 * tq), BF16),
                        pltpu.VMEM((1, 2 * tq), F32),
                        pltpu.VMEM((1, 2 * tq), F32),
                        pltpu.VMEM((1, 2 * tq), F32),
                        pltpu.VMEM((1, 2 * tq), F32),
                        pltpu.VMEM((MLA_VROWS, 2 * tq), F32)],
        compiler_params=_params(52, 3),
        name="mla_attn",
    )(qT, k, vT, mask)


def _vec_rows(rows, batch, d):
    full = [jnp.broadcast_to(r, (batch, d)) for r in rows]
    full += [jnp.zeros((batch, d), F32)] * (SUBLANES_F32 - len(full))
    return jnp.stack(full, axis=1)


def _pad_rows(rows, width):
    full = [r.reshape(1, width) for r in rows]
    full += [jnp.zeros((1, width), F32)] * (SUBLANES_F32 - len(full))
    return jnp.concatenate(full, axis=0)


def _pair_rope_cols(w, heads):
    kdim = w.shape[0]
    half = B_ROPE_DIM // 2
    wh = w.reshape(kdim, heads // 2, 2 * B_ROPE_DIM)
    pad = jnp.zeros((kdim, heads // 2, LANES - 2 * B_ROPE_DIM), w.dtype)
    plain = jnp.concatenate([wh, pad], axis=-1).reshape(kdim, -1)
    w4 = w.reshape(kdim, heads, 2, half)
    sw = jnp.concatenate([w4[:, :, 1], w4[:, :, 0]], axis=-1).reshape(kdim, heads // 2, 2 * B_ROPE_DIM)
    swapped = jnp.concatenate([sw, pad], axis=-1).reshape(kdim, -1)
    return plain, swapped


def _diagonal_mask():
    kc = np.arange(MLA_KTILE)[:, None] // CHUNK
    qc = (np.arange(2 * MLA_KTILE)[None, :] % MLA_KTILE) // CHUNK
    return jnp.asarray(np.where(kc <= qc, 0.0, NEG_INF), F32)


def _bias_table(rel_bias):
    heads = rel_bias.shape[0]
    n, w = A_SUB, A_WIN
    near, far = n - 1 - A_MAX_REL, w - 1 - A_MAX_REL
    by_dist = jnp.concatenate(
        [jnp.broadcast_to(rel_bias[:, :1], (heads, max(near, 0))),
         rel_bias[:, max(-near, 0):],
         jnp.broadcast_to(rel_bias[:, -1:], (heads, far))], axis=1).astype(F32)
    length = n - 1 + w
    flat = jnp.tile(by_dist[:, ::-1], (1, n))[:, n - 1:n - 1 + n * (length - 1)]
    table = flat.reshape(heads, n, length - 1)[:, :, :w]
    ql = np.arange(n)[:, None]
    kl = np.arange(w)[None, :]
    band = kl // CHUNK - ql // CHUNK
    in_band = np.logical_and(band >= 0, band <= A_LEFT_CHUNKS)
    table = jnp.where(in_band[None], table * math.log2(math.e), NEG_INF)
    return jnp.swapaxes(table, 1, 2).reshape(heads // 2, 2, w, n).transpose(0, 2, 1, 3).reshape(
        heads // 2, w, 2 * n)


def kernel(x, c, positions, mod_w, mod_b, norm1_g, norm2_g, a_wqkv, a_wo, a_rel_bias, kv_mod_w, kv_mod_b, kv_norm_g, b_wdkv, b_kv_lat_norm_g, b_wuk, b_wuv, b_wkr, b_wdq, b_q_norm_g, b_wuq, b_wqr, b_wo, f_win, f_conv_w, f_conv_b, f_wout, final_g):
    B, S, D = x.shape
    depth = mod_w.shape[0]
    ffn_dim = f_wout.shape[1]

    c_pad = jnp.pad(c, ((0, SUBLANES_F32 - B), (0, 0)))
    mod = _mod_call(c_pad, mod_w, mod_b.reshape(depth, 1, 6 * D))[:, :B]
    kv_mod = _mod_call(c_pad, kv_mod_w[None], kv_mod_b.reshape(1, 1, 2 * D))[0, :B]
    mods = [[mod[l, :, k * D:(k + 1) * D] for k in range(6)] for l in range(depth)]

    half = B_ROPE_DIM // 2
    inv_freq = jnp.power(jnp.float32(ROPE_THETA),
                         -jnp.arange(half, dtype=F32) * (2.0 / B_ROPE_DIM))
    ang = positions.astype(F32)[..., None] * inv_freq
    cos, sin = jnp.cos(ang), jnp.sin(ang)
    reps = LANES // B_ROPE_DIM
    cos_t = jnp.tile(jnp.concatenate([cos, cos], axis=-1), (1, 1, reps))
    sin_t = jnp.tile(jnp.concatenate([-sin, sin], axis=-1), (1, 1, reps))

    win_all = f_win.astype(BF16)
    wout_all = f_wout.astype(BF16)

    def conv_rows(l):
        return _pad_rows([f_conv_w[l, t] for t in range(CONV_WIDTH)] + [f_conv_b[l]], 2 * ffn_dim)

    sh1, sc1, g1, sh2, sc2, g2 = mods[0]
    qT, k, vT = _qkv_call(x, _vec_rows([norm1_g[0], sh1, sc1], B, D), a_wqkv[0].astype(BF16),
                          float(A_HEAD_DIM ** -0.5 * math.log2(math.e)))
    o = _chunk_attn_call(qT, k, vT, _bias_table(a_rel_bias[0]))
    h = _ffn_call(o, x, _vec_rows([g1, norm2_g[0], sh2, sc2, g2, final_g], B, D),
                  a_wo[0].astype(BF16), win_all, conv_rows(0), wout_all, 0, final_norm=False)

    kr_plain, kr_swapped = _pair_rope_cols(jnp.concatenate([b_wkr, b_wkr], axis=1), 2)
    w1 = jnp.concatenate([b_wdkv, kr_plain, kr_swapped], axis=1).astype(BF16)
    wu = jnp.concatenate([b_wuk, b_wuv], axis=1).astype(BF16)
    k_cat, v = _kv_call(h, _vec_rows([kv_norm_g, kv_mod[:, :D], kv_mod[:, D:]], B, D),
                        _pad_rows([b_kv_lat_norm_g], b_kv_lat_norm_g.shape[0]), w1, wu,
                        cos_t, sin_t)

    sh1, sc1, g1, sh2, sc2, g2 = mods[1]
    qr_plain, qr_swapped = _pair_rope_cols(b_wqr[0], B_HEADS)
    wq = jnp.concatenate([b_wuq[0], qr_plain, qr_swapped], axis=1).astype(BF16)
    score_scale = float((B_NOPE_DIM + B_ROPE_DIM) ** -0.5 * math.log2(math.e))
    q_cat = _q_call(h, _vec_rows([norm1_g[1], sh1, sc1], B, D),
                    _pad_rows([b_q_norm_g[0]], b_q_norm_g.shape[1]),
                    b_wdq[0].astype(BF16), wq, cos_t, sin_t, score_scale)
    o = _mla_call(q_cat, k_cat, v, _diagonal_mask())
    return _ffn_call(o, h, _vec_rows([g1, norm2_g[1], sh2, sc2, g2, final_g], B, D),
                     b_wo[0].astype(BF16), win_all, conv_rows(1), wout_all, 1, final_norm=True)
```

```python
import functools
import math

import jax
import jax.numpy as jnp
import numpy as np
from jax import lax
from jax.experimental import pallas as pl
from jax.experimental.pallas import tpu as pltpu

F32 = jnp.float32
BF16 = jnp.bfloat16

CHUNK = 64
A_HEADS = 16
A_HEAD_DIM = 64
A_LEFT_CHUNKS = 8
A_MAX_REL = 2 * CHUNK
B_HEADS = 16
B_NOPE_DIM = 64
B_ROPE_DIM = 32
B_V_DIM = 64
ROPE_THETA = 10000.0
CONV_WIDTH = 3
NORM_EPS = 1e-6
NEG_INF = -1e30

LANES = 128
MXU_COLS = 256
SUBLANES_F32 = 8
SUBLANES_BF16 = 16
VMEM_BYTES = 64 * 1024 * 1024

HEAD_PAIRS = 8
HALO = SUBLANES_BF16
ROW_TILE = 512
PROJ_TILE = 1024
A_QTILE = 1024
A_SUB = 2 * CHUNK
A_PREV = A_LEFT_CHUNKS * CHUNK
A_WIN = A_PREV + A_SUB
MLA_QTILE = 2048
MLA_KTILE = 512
MLA_MAX_JUMP = 64.0
MLA_VROWS = 64 + SUBLANES_BF16
FFN_CHUNK = MXU_COLS


def _params(vmem_mib, n_axes):
    return pltpu.CompilerParams(
        dimension_semantics=("arbitrary",) * n_axes,
        vmem_limit_bytes=vmem_mib * 1024 * 1024)


def _resident(shape):
    zeros = (0,) * len(shape)
    return pl.BlockSpec(shape, lambda *_: zeros, pipeline_mode=pl.Buffered(1))


def _resident_layer(shape, layer):
    return pl.BlockSpec((None,) + tuple(shape[1:]), lambda *_: (layer,) + (0,) * (len(shape) - 1),
                        pipeline_mode=pl.Buffered(1))


def _rms(x, g):
    return x * lax.rsqrt(jnp.mean(x * x, axis=-1, keepdims=True) + NORM_EPS) * g


def _norm_mod(x, g, shift, scale):
    return _rms(x, g) * (1.0 + scale) + shift


def _silu(x):
    return x * (1.0 / (1.0 + jnp.exp(-x)))


def _mod_body(c_ref, w_ref, b_ref, o_ref):
    ca = _silu(c_ref[...]).astype(BF16)
    o_ref[0] = jnp.dot(ca, w_ref[0].astype(BF16), preferred_element_type=F32) + b_ref[0]


def _mod_call(c_pad, w, b, tn=1024):
    L, D, N = w.shape
    return pl.pallas_call(
        _mod_body,
        grid=(L, N // tn),
        in_specs=[pl.BlockSpec((SUBLANES_F32, D), lambda l, n: (0, 0)),
                  pl.BlockSpec((1, D, tn), lambda l, n: (l, 0, n)),
                  pl.BlockSpec((1, 1, tn), lambda l, n: (l, 0, n))],
        out_specs=pl.BlockSpec((1, SUBLANES_F32, tn), lambda l, n: (l, 0, n)),
        out_shape=jax.ShapeDtypeStruct((L, SUBLANES_F32, N), F32),
        compiler_params=_params(32, 2),
        name="mod",
    )(c_pad, w, b)


def _qkv_body(x_ref, vec_ref, w_ref, qT_ref, k_ref, vT_ref, hn_ref, *, score_scale):
    vec = vec_ref[0]
    hn_ref[...] = _norm_mod(x_ref[0], vec[0:1], vec[1:2], vec[2:3]).astype(BF16)
    width = HEAD_PAIRS * LANES
    for c in range(HEAD_PAIRS // 2):
        def proj(base):
            cols = slice(base + c * MXU_COLS, base + (c + 1) * MXU_COLS)
            return jnp.dot(hn_ref[...], w_ref[:, cols], preferred_element_type=F32)
        q = proj(0) * score_scale
        k = proj(width)
        v = proj(2 * width)
        for half in range(2):
            g = slice(half * LANES, (half + 1) * LANES)
            qT_ref[0, 2 * c + half] = q[:, g].T.astype(BF16)
            k_ref[0, 2 * c + half] = k[:, g].astype(BF16)
            vT_ref[0, 2 * c + half] = v[:, g].T.astype(BF16)


def _qkv_call(x, vec, w, score_scale):
    B, S, D = x.shape
    tm = PROJ_TILE
    rows = pl.BlockSpec((1, HEAD_PAIRS, tm, LANES), lambda b, i: (b, 0, i, 0))
    cols = pl.BlockSpec((1, HEAD_PAIRS, LANES, tm), lambda b, i: (b, 0, 0, i))
    return pl.pallas_call(
        functools.partial(_qkv_body, score_scale=score_scale),
        grid=(B, S // tm),
        in_specs=[pl.BlockSpec((1, tm, D), lambda b, i: (b, i, 0)),
                  pl.BlockSpec((1, SUBLANES_F32, D), lambda b, i: (b, 0, 0)),
                  _resident(w.shape)],
        out_specs=[cols, rows, cols],
        out_shape=[jax.ShapeDtypeStruct((B, HEAD_PAIRS, LANES, S), BF16),
                   jax.ShapeDtypeStruct((B, HEAD_PAIRS, S, LANES), BF16),
                   jax.ShapeDtypeStruct((B, HEAD_PAIRS, LANES, S), BF16)],
        scratch_shapes=[pltpu.VMEM((tm, D), BF16)],
        compiler_params=_params(40, 2),
        name="qkv",
    )(x, vec, w)


def _chunk_attn_body(qT_ref, kp_ref, kc_ref, vTp_ref, vTc_ref, bias_ref, o_ref, qs_ref, s_ref):
    i = pl.program_id(2)
    n_sub = A_QTILE // A_SUB
    feat = lax.broadcasted_iota(jnp.int32, (LANES, 1), 0)
    qT = qT_ref[0, 0]
    q_even = jnp.where(feat < A_HEAD_DIM, qT, jnp.zeros_like(qT))
    q_odd = jnp.where(feat >= A_HEAD_DIM, qT, jnp.zeros_like(qT))
    for j in range(n_sub):
        cols = slice(j * A_SUB, (j + 1) * A_SUB)
        qs_ref[j, :, :A_SUB] = q_even[:, cols]
        qs_ref[j, :, A_SUB:] = q_odd[:, cols]

    def window(prev_ref, cur_ref, lo, axis):
        idx = lambda s: (0, 0, s, slice(None)) if axis == 0 else (0, 0, slice(None), s)
        if lo >= A_PREV:
            return cur_ref[idx(slice(lo - A_PREV, lo + A_SUB))]
        return jnp.concatenate([prev_ref[idx(slice(lo, A_PREV))],
                                cur_ref[idx(slice(0, lo + A_SUB))]], axis=axis)

    def scores(j):
        k2 = window(kp_ref, kc_ref, j * A_SUB, 0)
        half = A_WIN // 2
        s_ref[j % 2, :half] = jnp.dot(k2[:half], qs_ref[j], preferred_element_type=F32)
        s_ref[j % 2, half:] = jnp.dot(k2[half:], qs_ref[j], preferred_element_type=F32)

    def softmax_pv(j, first_tile):
        lo = j * A_SUB
        sT = s_ref[j % 2] + bias_ref[0]
        if first_tile and lo < A_PREV:
            row = lax.broadcasted_iota(jnp.int32, (A_WIN, 1), 0)
            sT = jnp.where(row >= A_PREV - lo, sT, NEG_INF)
        m = jnp.max(sT, axis=0, keepdims=True)
        pT = jnp.exp2(sT - m)
        l = jnp.sum(pT, axis=0, keepdims=True)
        pTb = pT.astype(BF16)
        vT = window(vTp_ref, vTc_ref, lo, 1)
        pv = jnp.concatenate(
            [jnp.dot(vT[:A_HEAD_DIM], pTb[:, :A_SUB], preferred_element_type=F32),
             jnp.dot(vT[A_HEAD_DIM:], pTb[:, A_SUB:], preferred_element_type=F32)], axis=1)
        oT = pv / l
        o_ref[0, 0, lo:lo + A_SUB, :] = jnp.concatenate(
            [oT[:, :A_SUB], oT[:, A_SUB:]], axis=0).T.astype(BF16)

    def tile(first_tile):
        scores(0)
        for j in range(n_sub):
            if j + 1 < n_sub:
                scores(j + 1)
            softmax_pv(j, first_tile)

    @pl.when(i == 0)
    def _():
        tile(True)

    @pl.when(i > 0)
    def _():
        tile(False)


def _chunk_attn_call(qT, k, vT, bias):
    B, _, _, S = qT.shape
    tq = A_QTILE
    ratio = tq // A_PREV
    prev_idx = lambda i: jnp.maximum(i * ratio - 1, 0)
    rows = lambda n: (1, 1, n, LANES)
    cols = lambda n: (1, 1, LANES, n)
    return pl.pallas_call(
        _chunk_attn_body,
        grid=(HEAD_PAIRS, B, S // tq),
        in_specs=[pl.BlockSpec(cols(tq), lambda p, b, i: (b, p, 0, i)),
                  pl.BlockSpec(rows(A_PREV), lambda p, b, i: (b, p, prev_idx(i), 0)),
                  pl.BlockSpec(rows(tq), lambda p, b, i: (b, p, i, 0)),
                  pl.BlockSpec(cols(A_PREV), lambda p, b, i: (b, p, 0, prev_idx(i))),
                  pl.BlockSpec(cols(tq), lambda p, b, i: (b, p, 0, i)),
                  pl.BlockSpec((1, A_WIN, 2 * A_SUB), lambda p, b, i: (p, 0, 0))],
        out_specs=pl.BlockSpec(rows(tq), lambda p, b, i: (b, p, i, 0)),
        out_shape=jax.ShapeDtypeStruct((B, HEAD_PAIRS, S, LANES), BF16),
        scratch_shapes=[pltpu.VMEM((tq // A_SUB, LANES, 2 * A_SUB), BF16),
                        pltpu.VMEM((2, A_WIN, 2 * A_SUB), F32)],
        compiler_params=_params(32, 3),
        name="chunk_attn",
    )(qT, k, k, vT, vT, bias)


def _ffn_body(o_ref, oh_ref, h_ref, hh_ref, vec_ref, wo_ref, win_ref, cw_ref, wout_ref,
              out_ref, hn_ref, h1_ref, acc_ref, u_ref, *, final_norm):
    i = pl.program_id(1)
    tm = h_ref.shape[1]
    vec = vec_ref[0]
    g1, n2g, sh2, sc2, g2, fg = (vec[r:r + 1] for r in range(6))

    o_cat = jnp.concatenate(
        [jnp.concatenate([oh_ref[0, p] for p in range(HEAD_PAIRS)], axis=1),
         jnp.concatenate([o_ref[0, p] for p in range(HEAD_PAIRS)], axis=1)], axis=0)
    h_cat = jnp.concatenate([hh_ref[0], h_ref[0]], axis=0)
    h1 = h_cat + g1 * jnp.dot(o_cat, wo_ref[...], preferred_element_type=F32)
    hn = _norm_mod(h1, n2g, sh2, sc2)
    row = lax.broadcasted_iota(jnp.int32, (tm + HALO, 1), 0)
    hn = jnp.where(jnp.logical_and(row < HALO, i == 0), 0.0, hn)
    hn_ref[...] = hn.astype(BF16)
    h1_ref[...] = h1[HALO:]

    ffn_dim = wout_ref.shape[0]
    groups = FFN_CHUNK // LANES

    def slabs(f, base):
        return (f % 2) * 2 * groups + (groups if base >= ffn_dim else 0)

    def up(f, base):
        u = jnp.dot(hn_ref[...], win_ref[:, base:base + FFN_CHUNK], preferred_element_type=F32)
        for g in range(groups):
            u_ref[slabs(f, base) + g] = u[:, g * LANES:(g + 1) * LANES]

    def conv(f, base):
        outs = []
        for g in range(groups):
            cols = slice(base + g * LANES, base + (g + 1) * LANES)
            slab = slabs(f, base) + g
            y = cw_ref[3:4, cols] + u_ref[slab, pl.ds(HALO - 2, tm), :] * cw_ref[0:1, cols]
            y = y + u_ref[slab, pl.ds(HALO - 1, tm), :] * cw_ref[1:2, cols]
            outs.append(y + u_ref[slab, pl.ds(HALO, tm), :] * cw_ref[2:3, cols])
        return jnp.concatenate(outs, axis=1)

    def down(f, act):
        part = jnp.dot(act, wout_ref[f * FFN_CHUNK:(f + 1) * FFN_CHUNK, :],
                       preferred_element_type=F32)
        if f == 0:
            acc_ref[...] = part
        else:
            acc_ref[...] += part

    n_chunks = ffn_dim // FFN_CHUNK
    up(0, 0)
    up(0, ffn_dim)
    act = None
    for f in range(n_chunks):
        if f + 1 < n_chunks:
            up(f + 1, (f + 1) * FFN_CHUNK)
            up(f + 1, ffn_dim + (f + 1) * FFN_CHUNK)
        if f > 0:
            down(f - 1, act)
        act = (_silu(conv(f, f * FFN_CHUNK)) * conv(f, ffn_dim + f * FFN_CHUNK)).astype(BF16)
    down(n_chunks - 1, act)

    h2 = h1_ref[...] + g2 * acc_ref[...]
    out_ref[0] = _rms(h2, fg) if final_norm else h2


def _ffn_call(o, h, vec, wo, win, cw, wout, layer, final_norm):
    B, S, D = h.shape
    tm = ROW_TILE
    halo_blocks = tm // HALO
    halo_idx = lambda i: jnp.maximum(i * halo_blocks - 1, 0)
    return pl.pallas_call(
        functools.partial(_ffn_body, final_norm=final_norm),
        grid=(B, S // tm),
        in_specs=[pl.BlockSpec((1, HEAD_PAIRS, tm, LANES), lambda b, i: (b, 0, i, 0)),
                  pl.BlockSpec((1, HEAD_PAIRS, HALO, LANES), lambda b, i: (b, 0, halo_idx(i), 0)),
                  pl.BlockSpec((1, tm, D), lambda b, i: (b, i, 0)),
                  pl.BlockSpec((1, HALO, D), lambda b, i: (b, halo_idx(i), 0)),
                  pl.BlockSpec((1, SUBLANES_F32, D), lambda b, i: (b, 0, 0)),
                  _resident(wo.shape), _resident_layer(win.shape, layer), _resident(cw.shape),
                  _resident_layer(wout.shape, layer)],
        out_specs=pl.BlockSpec((1, tm, D), lambda b, i: (b, i, 0)),
        out_shape=jax.ShapeDtypeStruct((B, S, D), F32),
        scratch_shapes=[pltpu.VMEM((tm + HALO, D), BF16),
                        pltpu.VMEM((tm, D), F32),
                        pltpu.VMEM((tm, D), F32),
                        pltpu.VMEM((4 * FFN_CHUNK // LANES, tm + HALO, LANES), F32)],
        compiler_params=_params(56, 2),
        name="ffn_final" if final_norm else "ffn",
    )(o, o, h, h, vec, wo, win, cw, wout)


def _kv_body(h_ref, vec_ref, lg_ref, w1_ref, wu_ref, cos_ref, sin_ref, k_ref, v_ref):
    vec = vec_ref[0]
    hn = _norm_mod(h_ref[0], vec[0:1], vec[1:2], vec[2:3]).astype(BF16)
    t = jnp.dot(hn, w1_ref[...], preferred_element_type=F32)
    lat = lg_ref.shape[1]
    ckv = _rms(t[:, :lat], lg_ref[0:1]).astype(BF16)
    kr = (t[:, lat:lat + LANES] * cos_ref[0] + t[:, lat + LANES:] * sin_ref[0]).astype(BF16)
    kv = jnp.dot(ckv, wu_ref[...], preferred_element_type=F32)
    half = HEAD_PAIRS * LANES
    for p in range(HEAD_PAIRS):
        k_ref[0, p, :, :LANES] = kv[:, p * LANES:(p + 1) * LANES].astype(BF16)
        k_ref[0, p, :, LANES:] = kr
        vt = kv[:, half + p * LANES:half + (p + 1) * LANES].T.astype(BF16)
        ones = jnp.ones((MLA_VROWS - B_V_DIM, vt.shape[1]), BF16)
        for hh in range(2):
            v_ref[0, p, hh * MLA_VROWS:hh * MLA_VROWS + B_V_DIM] = vt[hh * B_V_DIM:(hh + 1) * B_V_DIM]
            v_ref[0, p, hh * MLA_VROWS + B_V_DIM:(hh + 1) * MLA_VROWS] = ones


def _kv_call(h, vec, lat_g, w1, wu, cos, sin):
    B, S, D = h.shape
    tm = PROJ_TILE
    row = lambda b, i: (b, i, 0)
    return pl.pallas_call(
        _kv_body,
        grid=(B, S // tm),
        in_specs=[pl.BlockSpec((1, tm, D), row),
                  pl.BlockSpec((1, SUBLANES_F32, D), lambda b, i: (b, 0, 0)),
                  _resident(lat_g.shape), _resident(w1.shape), _resident(wu.shape),
                  pl.BlockSpec((1, tm, LANES), row), pl.BlockSpec((1, tm, LANES), row)],
        out_specs=[pl.BlockSpec((1, HEAD_PAIRS, tm, 2 * LANES), lambda b, i: (b, 0, i, 0)),
                   pl.BlockSpec((1, HEAD_PAIRS, 2 * MLA_VROWS, tm), lambda b, i: (b, 0, 0, i))],
        out_shape=[jax.ShapeDtypeStruct((B, HEAD_PAIRS, S, 2 * LANES), BF16),
                   jax.ShapeDtypeStruct((B, HEAD_PAIRS, 2 * MLA_VROWS, S), BF16)],
        compiler_params=_params(48, 2),
        name="shared_kv",
    )(h, vec, lat_g, w1, wu, cos, sin)


def _q_body(h_ref, vec_ref, qg_ref, wdq_ref, wq_ref, cos_ref, sin_ref, q_ref, *, score_scale):
    vec = vec_ref[0]
    hn = _norm_mod(h_ref[0], vec[0:1], vec[1:2], vec[2:3]).astype(BF16)
    cq = _rms(jnp.dot(hn, wdq_ref[...], preferred_element_type=F32), qg_ref[0:1]).astype(BF16)
    t = jnp.dot(cq, wq_ref[...], preferred_element_type=F32)
    width = HEAD_PAIRS * LANES
    cos = cos_ref[0]
    sin = sin_ref[0]
    for p in range(HEAD_PAIRS):
        g = slice(p * LANES, (p + 1) * LANES)
        rope = t[:, width:2 * width][:, g] * cos + t[:, 2 * width:][:, g] * sin
        qp = jnp.concatenate([t[:, g], rope], axis=1) * score_scale
        q_ref[0, p] = qp.T.astype(BF16)


def _q_call(h, vec, q_g, wdq, wq, cos, sin, score_scale):
    B, S, D = h.shape
    tm = PROJ_TILE
    row = lambda b, i: (b, i, 0)
    return pl.pallas_call(
        functools.partial(_q_body, score_scale=score_scale),
        grid=(B, S // tm),
        in_specs=[pl.BlockSpec((1, tm, D), row),
                  pl.BlockSpec((1, SUBLANES_F32, D), lambda b, i: (b, 0, 0)),
                  _resident(q_g.shape), _resident(wdq.shape), _resident(wq.shape),
                  pl.BlockSpec((1, tm, LANES), row), pl.BlockSpec((1, tm, LANES), row)],
        out_specs=pl.BlockSpec((1, HEAD_PAIRS, 2 * LANES, tm), lambda b, i: (b, 0, 0, i)),
        out_shape=jax.ShapeDtypeStruct((B, HEAD_PAIRS, 2 * LANES, S), BF16),
        compiler_params=_params(52, 2),
        name="mla_q",
    )(h, vec, q_g, wdq, wq, cos, sin)


def _mla_body(qT_ref, k_ref, vT_ref, mask_ref, o_ref, qs_ref, sa_ref, xa_ref,
              pa_ref, pb_ref, aa_ref, ab_ref, m_ref, jump_ref, acc_ref):
    qi = pl.program_id(2)
    tq = qT_ref.shape[3]
    tk = sa_ref.shape[0]
    n_sub = tq // tk
    width = 2 * tk
    feat = lax.broadcasted_iota(jnp.int32, (2 * LANES, 1), 0)
    in_even = jnp.logical_or(feat < B_NOPE_DIM,
                             jnp.logical_and(feat >= LANES, feat < LANES + B_ROPE_DIM))
    in_odd = jnp.logical_and(jnp.logical_not(in_even), feat < LANES + 2 * B_ROPE_DIM)
    qT = qT_ref[0, 0]
    for t in range(n_sub):
        qt = qT[:, t * tk:(t + 1) * tk]
        qs_ref[:, t * width:t * width + tk] = jnp.where(in_even, qt, jnp.zeros_like(qt))
        qs_ref[:, t * width + tk:(t + 1) * width] = jnp.where(in_odd, qt, jnp.zeros_like(qt))
    n_full = qi * n_sub

    def keys(j):
        return k_ref[0, 0, pl.ds(pl.multiple_of(j * tk, tk), tk), :]

    def values(j):
        return vT_ref[0, 0, :, pl.ds(pl.multiple_of(j * tk, tk), tk)]

    def pv_dots(vT, p_of, subs):
        return jnp.concatenate(
            [jnp.dot(vT[hh * MLA_VROWS:(hh + 1) * MLA_VROWS], p_of(t, hh),
                     preferred_element_type=F32)
             for t in subs for hh in range(2)], axis=1)

    pbuf = ((pa_ref, aa_ref), (pb_ref, ab_ref))

    def probs(j, slot, first_sub=0, diagonal=False):
        p_ref, a_ref = pbuf[slot]
        c0 = first_sub * width
        sT = jnp.dot(keys(j), qs_ref[:, c0:], preferred_element_type=F32)
        if diagonal:
            masked = sT[:, :width] + mask_ref[...]
            sT = masked if first_sub == n_sub - 1 else jnp.concatenate([masked, sT[:, width:]], axis=1)
        m_old = m_ref[:, c0:]
        m_cur = jnp.max(sT, axis=0, keepdims=True)
        p_ref[:, c0:] = jnp.exp2(sT - m_old).astype(BF16)
        m_new = jnp.maximum(m_old, m_cur)
        a_ref[:, c0:] = jnp.exp2(m_old - m_new)
        jump_ref[:, c0:] = jnp.maximum(jump_ref[:, c0:], m_cur - m_old)
        m_ref[:, c0:] = m_new

    def pv(j, slot, first_sub=0):
        p_ref, a_ref = pbuf[slot]
        c0 = first_sub * width
        out = pv_dots(values(jnp.maximum(j, 0)),
                      lambda t, hh: p_ref[:, t * width + hh * tk:t * width + (hh + 1) * tk],
                      range(first_sub, n_sub))
        acc_ref[:, c0:] = (acc_ref[:, c0:] + out) * a_ref[:, c0:]

    acc_ref[...] = jnp.zeros(acc_ref.shape, F32)
    jump_ref[...] = jnp.zeros(jump_ref.shape, F32)
    m_ref[...] = jnp.dot(k_ref[0, 0, :SUBLANES_BF16, :], qs_ref[...],
                         preferred_element_type=F32)[:1]
    pb_ref[...] = jnp.zeros(pb_ref.shape, BF16)
    ab_ref[...] = jnp.ones(ab_ref.shape, F32)

    def four_blocks(u, carry):
        for r in range(4):
            probs(4 * u + r, r % 2)
            pv(4 * u + r - 1, (r + 1) % 2)
        return carry

    lax.fori_loop(0, n_full // 4, four_blocks, 0)
    for r in range(n_sub):
        probs(n_full + r, r % 2, first_sub=r, diagonal=True)
        pv(n_full + r - 1, (r + 1) % 2, first_sub=max(r - 1, 0))
    pv(n_full + n_sub - 1, (n_sub - 1) % 2, first_sub=n_sub - 1)

    def scores(j, first_sub=0):
        c0 = first_sub * width
        sT = jnp.dot(keys(j), qs_ref[:, c0:], preferred_element_type=F32)
        sa_ref[:, c0:] = sT
        xa_ref[:, c0:] = jnp.max(sT, axis=0, keepdims=True)

    def softmax_pv(j, subs, diagonal=False):
        c0, c1 = subs[0] * width, (subs[-1] + 1) * width
        sT = sa_ref[:, c0:c1]
        if diagonal:
            sT = sT + mask_ref[...]
            m_cur = jnp.max(sT, axis=0, keepdims=True)
        else:
            m_cur = xa_ref[:, c0:c1]
        m_old = m_ref[:, c0:c1]
        m_new = jnp.maximum(m_old, m_cur)
        alpha = jnp.exp2(m_old - m_new)
        pTb = jnp.exp2(sT - m_new).astype(BF16)
        out = pv_dots(values(j), lambda t, hh: pTb[:, (t - subs[0]) * width + hh * tk:
                                                   (t - subs[0]) * width + (hh + 1) * tk], subs)
        acc_ref[:, c0:c1] = alpha * acc_ref[:, c0:c1] + out
        m_ref[:, c0:c1] = m_new

    @pl.when(jnp.max(jump_ref[...]) > MLA_MAX_JUMP)
    def _():
        m_ref[...] = jnp.full(m_ref.shape, NEG_INF, F32)
        acc_ref[...] = jnp.zeros(acc_ref.shape, F32)

        def one_block(j, carry):
            scores(j)
            softmax_pv(j, tuple(range(n_sub)))
            return carry

        lax.fori_loop(0, n_full, one_block, 0)
        for r in range(n_sub):
            scores(n_full + r, first_sub=r)
            softmax_pv(n_full + r, (r,), diagonal=True)
            if r + 1 < n_sub:
                softmax_pv(n_full + r, tuple(range(r + 1, n_sub)))

    oT = acc_ref[:B_V_DIM, :] / acc_ref[B_V_DIM:B_V_DIM + 1, :]
    for t in range(n_sub):
        both = jnp.concatenate([oT[:, t * width:t * width + tk],
                                oT[:, t * width + tk:(t + 1) * width]], axis=0)
        o_ref[0, 0, t * tk:(t + 1) * tk, :] = both.T.astype(BF16)


def _mla_call(qT, k, vT, mask):
    B, P, _, S = qT.shape
    tq, tk = MLA_QTILE, MLA_KTILE
    assert tq % (4 * tk) == 0
    return pl.pallas_call(
        _mla_body,
        grid=(B, P, S // tq),
        in_specs=[pl.BlockSpec((1, 1, 2 * LANES, tq), lambda b, p, i: (b, p, 0, i)),
                  pl.BlockSpec((1, 1, S, 2 * LANES), lambda b, p, i: (b, p, 0, 0)),
                  pl.BlockSpec((1, 1, 2 * MLA_VROWS, S), lambda b, p, i: (b, p, 0, 0)),
                  _resident(mask.shape)],
        out_specs=pl.BlockSpec((1, 1, tq, LANES), lambda b, p, i: (b, p, i, 0)),
        out_shape=jax.ShapeDtypeStruct((B, P, S, LANES), BF16),
        scratch_shapes=[pltpu.VMEM((2 * LANES, 2 * tq), BF16),
                        pltpu.VMEM((tk, 2 * tq), F32),
                        pltpu.VMEM((1, 2 * tq), F32),
                        pltpu.VMEM((tk, 2 * tq), BF16),
                        pltpu.VMEM((tk, 2 * tq), BF16),
                        pltpu.VMEM((1, 2 * tq), F32),
                        pltpu.VMEM((1, 2 * tq), F32),
                        pltpu.VMEM((1, 2 * tq), F32),
                        pltpu.VMEM((1, 2 * tq), F32),
                        pltpu.VMEM((MLA_VROWS, 2 * tq), F32)],
        compiler_params=_params(52, 3),
        name="mla_attn",
    )(qT, k, vT, mask)


def _vec_rows(rows, batch, d):
    full = [jnp.broadcast_to(r, (batch, d)) for r in rows]
    full += [jnp.zeros((batch, d), F32)] * (SUBLANES_F32 - len(full))
    return jnp.stack(full, axis=1)


def _pad_rows(rows, width):
    full = [r.reshape(1, width) for r in rows]
    full += [jnp.zeros((1, width), F32)] * (SUBLANES_F32 - len(full))
    return jnp.concatenate(full, axis=0)


def _pair_rope_cols(w, heads):
    kdim = w.shape[0]
    half = B_ROPE_DIM // 2
    wh = w.reshape(kdim, heads // 2, 2 * B_ROPE_DIM)
    pad = jnp.zeros((kdim, heads // 2, LANES - 2 * B_ROPE_DIM), w.dtype)
    plain = jnp.concatenate([wh, pad], axis=-1).reshape(kdim, -1)
    w4 = w.reshape(kdim, heads, 2, half)
    sw = jnp.concatenate([w4[:, :, 1], w4[:, :, 0]], axis=-1).reshape(kdim, heads // 2, 2 * B_ROPE_DIM)
    swapped = jnp.concatenate([sw, pad], axis=-1).reshape(kdim, -1)
    return plain, swapped


def _diagonal_mask():
    kc = np.arange(MLA_KTILE)[:, None] // CHUNK
    qc = (np.arange(2 * MLA_KTILE)[None, :] % MLA_KTILE) // CHUNK
    return jnp.asarray(np.where(kc <= qc, 0.0, NEG_INF), F32)


def _bias_table(rel_bias):
    heads = rel_bias.shape[0]
    n, w = A_SUB, A_WIN
    near, far = n - 1 - A_MAX_REL, w - 1 - A_MAX_REL
    by_dist = jnp.concatenate(
        [jnp.broadcast_to(rel_bias[:, :1], (heads, max(near, 0))),
         rel_bias[:, max(-near, 0):],
         jnp.broadcast_to(rel_bias[:, -1:], (heads, far))], axis=1).astype(F32)
    length = n - 1 + w
    flat = jnp.tile(by_dist[:, ::-1], (1, n))[:, n - 1:n - 1 + n * (length - 1)]
    table = flat.reshape(heads, n, length - 1)[:, :, :w]
    ql = np.arange(n)[:, None]
    kl = np.arange(w)[None, :]
    band = kl // CHUNK - ql // CHUNK
    in_band = np.logical_and(band >= 0, band <= A_LEFT_CHUNKS)
    table = jnp.where(in_band[None], table * math.log2(math.e), NEG_INF)
    return jnp.swapaxes(table, 1, 2).reshape(heads // 2, 2, w, n).transpose(0, 2, 1, 3).reshape(
        heads // 2, w, 2 * n)


def kernel(x, c, positions, mod_w, mod_b, norm1_g, norm2_g, a_wqkv, a_wo, a_rel_bias, kv_mod_w, kv_mod_b, kv_norm_g, b_wdkv, b_kv_lat_norm_g, b_wuk, b_wuv, b_wkr, b_wdq, b_q_norm_g, b_wuq, b_wqr, b_wo, f_win, f_conv_w, f_conv_b, f_wout, final_g):
    B, S, D = x.shape
    depth = mod_w.shape[0]
    ffn_dim = f_wout.shape[1]

    c_pad = jnp.pad(c, ((0, SUBLANES_F32 - B), (0, 0)))
    mod = _mod_call(c_pad, mod_w, mod_b.reshape(depth, 1, 6 * D))[:, :B]
    kv_mod = _mod_call(c_pad, kv_mod_w[None], kv_mod_b.reshape(1, 1, 2 * D))[0, :B]
    mods = [[mod[l, :, k * D:(k + 1) * D] for k in range(6)] for l in range(depth)]

    half = B_ROPE_DIM // 2
    inv_freq = jnp.power(jnp.float32(ROPE_THETA),
                         -jnp.arange(half, dtype=F32) * (2.0 / B_ROPE_DIM))
    ang = positions.astype(F32)[..., None] * inv_freq
    cos, sin = jnp.cos(ang), jnp.sin(ang)
    reps = LANES // B_ROPE_DIM
    cos_t = jnp.tile(jnp.concatenate([cos, cos], axis=-1), (1, 1, reps))
    sin_t = jnp.tile(jnp.concatenate([-sin, sin], axis=-1), (1, 1, reps))

    win_all = f_win.astype(BF16)
    wout_all = f_wout.astype(BF16)

    def conv_rows(l):
        return _pad_rows([f_conv_w[l, t] for t in range(CONV_WIDTH)] + [f_conv_b[l]], 2 * ffn_dim)

    sh1, sc1, g1, sh2, sc2, g2 = mods[0]
    qT, k, vT = _qkv_call(x, _vec_rows([norm1_g[0], sh1, sc1], B, D), a_wqkv[0].astype(BF16),
                          float(A_HEAD_DIM ** -0.5 * math.log2(math.e)))
    o = _chunk_attn_call(qT, k, vT, _bias_table(a_rel_bias[0]))
    h = _ffn_call(o, x, _vec_rows([g1, norm2_g[0], sh2, sc2, g2, final_g], B, D),
                  a_wo[0].astype(BF16), win_all, conv_rows(0), wout_all, 0, final_norm=False)

    kr_plain, kr_swapped = _pair_rope_cols(jnp.concatenate([b_wkr, b_wkr], axis=1), 2)
    w1 = jnp.concatenate([b_wdkv, kr_plain, kr_swapped], axis=1).astype(BF16)
    wu = jnp.concatenate([b_wuk, b_wuv], axis=1).astype(BF16)
    k_cat, v = _kv_call(h, _vec_rows([kv_norm_g, kv_mod[:, :D], kv_mod[:, D:]], B, D),
                        _pad_rows([b_kv_lat_norm_g], b_kv_lat_norm_g.shape[0]), w1, wu,
                        cos_t, sin_t)

    sh1, sc1, g1, sh2, sc2, g2 = mods[1]
    qr_plain, qr_swapped = _pair_rope_cols(b_wqr[0], B_HEADS)
    wq = jnp.concatenate([b_wuq[0], qr_plain, qr_swapped], axis=1).astype(BF16)
    score_scale = float((B_NOPE_DIM + B_ROPE_DIM) ** -0.5 * math.log2(math.e))
    q_cat = _q_call(h, _vec_rows([norm1_g[1], sh1, sc1], B, D),
                    _pad_rows([b_q_norm_g[0]], b_q_norm_g.shape[1]),
                    b_wdq[0].astype(BF16), wq, cos_t, sin_t, score_scale)
    o = _mla_call(q_cat, k_cat, v, _diagonal_mask())
    return _ffn_call(o, h, _vec_rows([g1, norm2_g[1], sh2, sc2, g2, final_g], B, D),
                     b_wo[0].astype(BF16), win_all, conv_rows(1), wout_all, 1, final_norm=True)
```

```python
import functools
import math

import jax
import jax.numpy as jnp
import numpy as np
from jax import lax
from jax.experimental import pallas as pl
from jax.experimental.pallas import tpu as pltpu

F32 = jnp.float32
BF16 = jnp.bfloat16

CHUNK = 64
A_HEADS = 16
A_HEAD_DIM = 64
A_LEFT_CHUNKS = 8
A_MAX_REL = 2 * CHUNK
B_HEADS = 16
B_NOPE_DIM = 64
B_ROPE_DIM = 32
B_V_DIM = 64
ROPE_THETA = 10000.0
CONV_WIDTH = 3
NORM_EPS = 1e-6
NEG_INF = -1e30

LANES = 128
MXU_COLS = 256
SUBLANES_F32 = 8
SUBLANES_BF16 = 16
VMEM_BYTES = 64 * 1024 * 1024

HEAD_PAIRS = 8
HALO = SUBLANES_BF16
ROW_TILE = 512
PROJ_TILE = 1024
A_QTILE = 1024
A_SUB = 2 * CHUNK
A_PREV = A_LEFT_CHUNKS * CHUNK
A_WIN = A_PREV + A_SUB
MLA_QTILE = 2048
MLA_KTILE = 512
MLA_MAX_JUMP = 64.0
MLA_VROWS = 64 + SUBLANES_BF16
FFN_CHUNK = MXU_COLS


def _params(vmem_mib, n_axes):
    return pltpu.CompilerParams(
        dimension_semantics=("arbitrary",) * n_axes,
        vmem_limit_bytes=vmem_mib * 1024 * 1024)


def _resident(shape):
    zeros = (0,) * len(shape)
    return pl.BlockSpec(shape, lambda *_: zeros, pipeline_mode=pl.Buffered(1))


def _resident_layer(shape, layer):
    return pl.BlockSpec((None,) + tuple(shape[1:]), lambda *_: (layer,) + (0,) * (len(shape) - 1),
                        pipeline_mode=pl.Buffered(1))


def _rms(x, g):
    return x * lax.rsqrt(jnp.mean(x * x, axis=-1, keepdims=True) + NORM_EPS) * g


def _norm_mod(x, g, shift, scale):
    return _rms(x, g) * (1.0 + scale) + shift


def _lane_tile(x):
    return jnp.concatenate([x] * (LANES // x.shape[1]), axis=1)


def _silu(x):
    return x * (1.0 / (1.0 + jnp.exp(-x)))


def _mod_body(c_ref, w_ref, b_ref, o_ref):
    ca = _silu(c_ref[...]).astype(BF16)
    o_ref[0] = jnp.dot(ca, w_ref[0].astype(BF16), preferred_element_type=F32) + b_ref[0]


def _mod_call(c_pad, w, b, tn=1024):
    L, D, N = w.shape
    return pl.pallas_call(
        _mod_body,
        grid=(L, N // tn),
        in_specs=[pl.BlockSpec((SUBLANES_F32, D), lambda l, n: (0, 0)),
                  pl.BlockSpec((1, D, tn), lambda l, n: (l, 0, n)),
                  pl.BlockSpec((1, 1, tn), lambda l, n: (l, 0, n))],
        out_specs=pl.BlockSpec((1, SUBLANES_F32, tn), lambda l, n: (l, 0, n)),
        out_shape=jax.ShapeDtypeStruct((L, SUBLANES_F32, N), F32),
        compiler_params=_params(32, 2),
        name="mod",
    )(c_pad, w, b)


def _qkv_body(x_ref, vec_ref, w_ref, qT_ref, k_ref, vT_ref, hn_ref, *, score_scale):
    vec = vec_ref[0]
    hn_ref[...] = _norm_mod(x_ref[0], vec[0:1], vec[1:2], vec[2:3]).astype(BF16)
    width = HEAD_PAIRS * LANES
    for c in range(HEAD_PAIRS // 2):
        def proj(base):
            cols = slice(base + c * MXU_COLS, base + (c + 1) * MXU_COLS)
            return jnp.dot(hn_ref[...], w_ref[:, cols], preferred_element_type=F32)
        q = proj(0) * score_scale
        k = proj(width)
        v = proj(2 * width)
        for half in range(2):
            g = slice(half * LANES, (half + 1) * LANES)
            qT_ref[0, 2 * c + half] = q[:, g].T.astype(BF16)
            k_ref[0, 2 * c + half] = k[:, g].astype(BF16)
            vT_ref[0, 2 * c + half] = v[:, g].T.astype(BF16)


def _qkv_call(x, vec, w, score_scale):
    B, S, D = x.shape
    tm = PROJ_TILE
    rows = pl.BlockSpec((1, HEAD_PAIRS, tm, LANES), lambda b, i: (b, 0, i, 0))
    cols = pl.BlockSpec((1, HEAD_PAIRS, LANES, tm), lambda b, i: (b, 0, 0, i))
    return pl.pallas_call(
        functools.partial(_qkv_body, score_scale=score_scale),
        grid=(B, S // tm),
        in_specs=[pl.BlockSpec((1, tm, D), lambda b, i: (b, i, 0)),
                  pl.BlockSpec((1, SUBLANES_F32, D), lambda b, i: (b, 0, 0)),
                  _resident(w.shape)],
        out_specs=[cols, rows, cols],
        out_shape=[jax.ShapeDtypeStruct((B, HEAD_PAIRS, LANES, S), BF16),
                   jax.ShapeDtypeStruct((B, HEAD_PAIRS, S, LANES), BF16),
                   jax.ShapeDtypeStruct((B, HEAD_PAIRS, LANES, S), BF16)],
        scratch_shapes=[pltpu.VMEM((tm, D), BF16)],
        compiler_params=_params(40, 2),
        name="qkv",
    )(x, vec, w)


def _chunk_attn_body(qT_ref, kp_ref, kc_ref, vTp_ref, vTc_ref, bias_ref, o_ref, qs_ref, s_ref):
    i = pl.program_id(2)
    n_sub = A_QTILE // A_SUB
    feat = lax.broadcasted_iota(jnp.int32, (LANES, 1), 0)
    qT = qT_ref[0, 0]
    q_even = jnp.where(feat < A_HEAD_DIM, qT, jnp.zeros_like(qT))
    q_odd = jnp.where(feat >= A_HEAD_DIM, qT, jnp.zeros_like(qT))
    for j in range(n_sub):
        cols = slice(j * A_SUB, (j + 1) * A_SUB)
        qs_ref[j, :, :A_SUB] = q_even[:, cols]
        qs_ref[j, :, A_SUB:] = q_odd[:, cols]

    def window(prev_ref, cur_ref, lo, axis):
        idx = lambda s: (0, 0, s, slice(None)) if axis == 0 else (0, 0, slice(None), s)
        if lo >= A_PREV:
            return cur_ref[idx(slice(lo - A_PREV, lo + A_SUB))]
        return jnp.concatenate([prev_ref[idx(slice(lo, A_PREV))],
                                cur_ref[idx(slice(0, lo + A_SUB))]], axis=axis)

    def scores(j):
        k2 = window(kp_ref, kc_ref, j * A_SUB, 0)
        half = A_WIN // 2
        s_ref[j % 2, :half] = jnp.dot(k2[:half], qs_ref[j], preferred_element_type=F32)
        s_ref[j % 2, half:] = jnp.dot(k2[half:], qs_ref[j], preferred_element_type=F32)

    def softmax_pv(j, first_tile):
        lo = j * A_SUB
        sT = s_ref[j % 2] + bias_ref[0]
        if first_tile and lo < A_PREV:
            row = lax.broadcasted_iota(jnp.int32, (A_WIN, 1), 0)
            sT = jnp.where(row >= A_PREV - lo, sT, NEG_INF)
        m = jnp.max(sT, axis=0, keepdims=True)
        pT = jnp.exp2(sT - m)
        l = jnp.sum(pT, axis=0, keepdims=True)
        pTb = pT.astype(BF16)
        vT = window(vTp_ref, vTc_ref, lo, 1)
        pv = jnp.concatenate(
            [jnp.dot(vT[:A_HEAD_DIM], pTb[:, :A_SUB], preferred_element_type=F32),
             jnp.dot(vT[A_HEAD_DIM:], pTb[:, A_SUB:], preferred_element_type=F32)], axis=1)
        oT = pv / l
        o_ref[0, 0, lo:lo + A_SUB, :] = jnp.concatenate(
            [oT[:, :A_SUB], oT[:, A_SUB:]], axis=0).T.astype(BF16)

    def tile(first_tile):
        scores(0)
        for j in range(n_sub):
            if j + 1 < n_sub:
                scores(j + 1)
            softmax_pv(j, first_tile)

    @pl.when(i == 0)
    def _():
        tile(True)

    @pl.when(i > 0)
    def _():
        tile(False)


def _chunk_attn_call(qT, k, vT, bias):
    B, _, _, S = qT.shape
    tq = A_QTILE
    ratio = tq // A_PREV
    prev_idx = lambda i: jnp.maximum(i * ratio - 1, 0)
    rows = lambda n: (1, 1, n, LANES)
    cols = lambda n: (1, 1, LANES, n)
    return pl.pallas_call(
        _chunk_attn_body,
        grid=(HEAD_PAIRS, B, S // tq),
        in_specs=[pl.BlockSpec(cols(tq), lambda p, b, i: (b, p, 0, i)),
                  pl.BlockSpec(rows(A_PREV), lambda p, b, i: (b, p, prev_idx(i), 0)),
                  pl.BlockSpec(rows(tq), lambda p, b, i: (b, p, i, 0)),
                  pl.BlockSpec(cols(A_PREV), lambda p, b, i: (b, p, 0, prev_idx(i))),
                  pl.BlockSpec(cols(tq), lambda p, b, i: (b, p, 0, i)),
                  pl.BlockSpec((1, A_WIN, 2 * A_SUB), lambda p, b, i: (p, 0, 0))],
        out_specs=pl.BlockSpec(rows(tq), lambda p, b, i: (b, p, i, 0)),
        out_shape=jax.ShapeDtypeStruct((B, HEAD_PAIRS, S, LANES), BF16),
        scratch_shapes=[pltpu.VMEM((tq // A_SUB, LANES, 2 * A_SUB), BF16),
                        pltpu.VMEM((2, A_WIN, 2 * A_SUB), F32)],
        compiler_params=_params(32, 3),
        name="chunk_attn",
    )(qT, k, k, vT, vT, bias)


def _ffn_body(o_ref, oh_ref, h_ref, hh_ref, vec_ref, wo_ref, win_ref, cw_ref, wout_ref,
              out_ref, hn_ref, h1_ref, acc_ref, u_ref, *, final_norm):
    i = pl.program_id(1)
    tm = h_ref.shape[1]
    vec = vec_ref[0]
    g1, n2g, sh2, sc2, g2, fg = (vec[r:r + 1] for r in range(6))

    o_cat = jnp.concatenate(
        [jnp.concatenate([oh_ref[0, p] for p in range(HEAD_PAIRS)], axis=1),
         jnp.concatenate([o_ref[0, p] for p in range(HEAD_PAIRS)], axis=1)], axis=0)
    h_cat = jnp.concatenate([hh_ref[0], h_ref[0]], axis=0)
    h1 = h_cat + g1 * jnp.dot(o_cat, wo_ref[...], preferred_element_type=F32)
    hn = _norm_mod(h1, n2g, sh2, sc2)
    row = lax.broadcasted_iota(jnp.int32, (tm + HALO, 1), 0)
    hn = jnp.where(jnp.logical_and(row < HALO, i == 0), 0.0, hn)
    hn_ref[...] = hn.astype(BF16)
    h1_ref[...] = h1[HALO:]

    ffn_dim = wout_ref.shape[0]
    groups = FFN_CHUNK // LANES

    def slabs(f, base):
        return (f % 2) * 2 * groups + (groups if base >= ffn_dim else 0)

    def up(f, base):
        u = jnp.dot(hn_ref[...], win_ref[:, base:base + FFN_CHUNK], preferred_element_type=F32)
        for g in range(groups):
            u_ref[slabs(f, base) + g] = u[:, g * LANES:(g + 1) * LANES]

    def conv(f, base):
        outs = []
        for g in range(groups):
            cols = slice(base + g * LANES, base + (g + 1) * LANES)
            slab = slabs(f, base) + g
            y = cw_ref[3:4, cols] + u_ref[slab, pl.ds(HALO - 2, tm), :] * cw_ref[0:1, cols]
            y = y + u_ref[slab, pl.ds(HALO - 1, tm), :] * cw_ref[1:2, cols]
            outs.append(y + u_ref[slab, pl.ds(HALO, tm), :] * cw_ref[2:3, cols])
        return jnp.concatenate(outs, axis=1)

    def down(f, act):
        part = jnp.dot(act, wout_ref[f * FFN_CHUNK:(f + 1) * FFN_CHUNK, :],
                       preferred_element_type=F32)
        if f == 0:
            acc_ref[...] = part
        else:
            acc_ref[...] += part

    n_chunks = ffn_dim // FFN_CHUNK
    up(0, 0)
    up(0, ffn_dim)
    act = None
    for f in range(n_chunks):
        if f + 1 < n_chunks:
            up(f + 1, (f + 1) * FFN_CHUNK)
            up(f + 1, ffn_dim + (f + 1) * FFN_CHUNK)
        if f > 0:
            down(f - 1, act)
        act = (_silu(conv(f, f * FFN_CHUNK)) * conv(f, ffn_dim + f * FFN_CHUNK)).astype(BF16)
    down(n_chunks - 1, act)

    h2 = h1_ref[...] + g2 * acc_ref[...]
    out_ref[0] = _rms(h2, fg) if final_norm else h2


def _ffn_call(o, h, vec, wo, win, cw, wout, layer, final_norm):
    B, S, D = h.shape
    tm = ROW_TILE
    halo_blocks = tm // HALO
    halo_idx = lambda i: jnp.maximum(i * halo_blocks - 1, 0)
    return pl.pallas_call(
        functools.partial(_ffn_body, final_norm=final_norm),
        grid=(B, S // tm),
        in_specs=[pl.BlockSpec((1, HEAD_PAIRS, tm, LANES), lambda b, i: (b, 0, i, 0)),
                  pl.BlockSpec((1, HEAD_PAIRS, HALO, LANES), lambda b, i: (b, 0, halo_idx(i), 0)),
                  pl.BlockSpec((1, tm, D), lambda b, i: (b, i, 0)),
                  pl.BlockSpec((1, HALO, D), lambda b, i: (b, halo_idx(i), 0)),
                  pl.BlockSpec((1, SUBLANES_F32, D), lambda b, i: (b, 0, 0)),
                  _resident(wo.shape), _resident_layer(win.shape, layer), _resident(cw.shape),
                  _resident_layer(wout.shape, layer)],
        out_specs=pl.BlockSpec((1, tm, D), lambda b, i: (b, i, 0)),
        out_shape=jax.ShapeDtypeStruct((B, S, D), F32),
        scratch_shapes=[pltpu.VMEM((tm + HALO, D), BF16),
                        pltpu.VMEM((tm, D), F32),
                        pltpu.VMEM((tm, D), F32),
                        pltpu.VMEM((4 * FFN_CHUNK // LANES, tm + HALO, LANES), F32)],
        compiler_params=_params(56, 2),
        name="ffn_final" if final_norm else "ffn",
    )(o, o, h, h, vec, wo, win, cw, wout)


def _kv_body(h_ref, vec_ref, lg_ref, w1_ref, wu_ref, cos_ref, sin_ref, k_ref, v_ref):
    vec = vec_ref[0]
    hn = _norm_mod(h_ref[0], vec[0:1], vec[1:2], vec[2:3]).astype(BF16)
    t = jnp.dot(hn, w1_ref[...], preferred_element_type=F32)
    lat = lg_ref.shape[1]
    ckv = _rms(t[:, :lat], lg_ref[0:1]).astype(BF16)
    cos, sin = _lane_tile(cos_ref[0]), _lane_tile(sin_ref[0])
    kr = (t[:, lat:lat + LANES] * cos + t[:, lat + LANES:] * sin).astype(BF16)
    kv = jnp.dot(ckv, wu_ref[...], preferred_element_type=F32)
    half = HEAD_PAIRS * LANES
    for p in range(HEAD_PAIRS):
        k_ref[0, p, :, :LANES] = kv[:, p * LANES:(p + 1) * LANES].astype(BF16)
        k_ref[0, p, :, LANES:] = kr
        vt = kv[:, half + p * LANES:half + (p + 1) * LANES].T.astype(BF16)
        ones = jnp.ones((MLA_VROWS - B_V_DIM, vt.shape[1]), BF16)
        for hh in range(2):
            v_ref[0, p, hh * MLA_VROWS:hh * MLA_VROWS + B_V_DIM] = vt[hh * B_V_DIM:(hh + 1) * B_V_DIM]
            v_ref[0, p, hh * MLA_VROWS + B_V_DIM:(hh + 1) * MLA_VROWS] = ones


def _kv_call(h, vec, lat_g, w1, wu, cos, sin):
    B, S, D = h.shape
    tm = PROJ_TILE
    row = lambda b, i: (b, i, 0)
    return pl.pallas_call(
        _kv_body,
        grid=(B, S // tm),
        in_specs=[pl.BlockSpec((1, tm, D), row),
                  pl.BlockSpec((1, SUBLANES_F32, D), lambda b, i: (b, 0, 0)),
                  _resident(lat_g.shape), _resident(w1.shape), _resident(wu.shape),
                  pl.BlockSpec((1, tm, B_ROPE_DIM), row), pl.BlockSpec((1, tm, B_ROPE_DIM), row)],
        out_specs=[pl.BlockSpec((1, HEAD_PAIRS, tm, 2 * LANES), lambda b, i: (b, 0, i, 0)),
                   pl.BlockSpec((1, HEAD_PAIRS, 2 * MLA_VROWS, tm), lambda b, i: (b, 0, 0, i))],
        out_shape=[jax.ShapeDtypeStruct((B, HEAD_PAIRS, S, 2 * LANES), BF16),
                   jax.ShapeDtypeStruct((B, HEAD_PAIRS, 2 * MLA_VROWS, S), BF16)],
        compiler_params=_params(48, 2),
        name="shared_kv",
    )(h, vec, lat_g, w1, wu, cos, sin)


def _q_body(h_ref, vec_ref, qg_ref, wdq_ref, wq_ref, cos_ref, sin_ref, q_ref, *, score_scale):
    vec = vec_ref[0]
    hn = _norm_mod(h_ref[0], vec[0:1], vec[1:2], vec[2:3]).astype(BF16)
    cq = _rms(jnp.dot(hn, wdq_ref[...], preferred_element_type=F32), qg_ref[0:1]).astype(BF16)
    t = jnp.dot(cq, wq_ref[...], preferred_element_type=F32)
    width = HEAD_PAIRS * LANES
    cos = _lane_tile(cos_ref[0])
    sin = _lane_tile(sin_ref[0])
    for p in range(HEAD_PAIRS):
        g = slice(p * LANES, (p + 1) * LANES)
        rope = t[:, width:2 * width][:, g] * cos + t[:, 2 * width:][:, g] * sin
        qp = jnp.concatenate([t[:, g], rope], axis=1) * score_scale
        q_ref[0, p] = qp.T.astype(BF16)


def _q_call(h, vec, q_g, wdq, wq, cos, sin, score_scale):
    B, S, D = h.shape
    tm = PROJ_TILE
    row = lambda b, i: (b, i, 0)
    return pl.pallas_call(
        functools.partial(_q_body, score_scale=score_scale),
        grid=(B, S // tm),
        in_specs=[pl.BlockSpec((1, tm, D), row),
                  pl.BlockSpec((1, SUBLANES_F32, D), lambda b, i: (b, 0, 0)),
                  _resident(q_g.shape), _resident(wdq.shape), _resident(wq.shape),
                  pl.BlockSpec((1, tm, B_ROPE_DIM), row), pl.BlockSpec((1, tm, B_ROPE_DIM), row)],
        out_specs=pl.BlockSpec((1, HEAD_PAIRS, 2 * LANES, tm), lambda b, i: (b, 0, 0, i)),
        out_shape=jax.ShapeDtypeStruct((B, HEAD_PAIRS, 2 * LANES, S), BF16),
        compiler_params=_params(52, 2),
        name="mla_q",
    )(h, vec, q_g, wdq, wq, cos, sin)


def _mla_body(qT_ref, k_ref, vT_ref, mask_ref, o_ref, qs_ref, sa_ref, xa_ref,
              pa_ref, pb_ref, aa_ref, ab_ref, m_ref, jump_ref, acc_ref):
    qi = pl.program_id(2)
    tq = qT_ref.shape[3]
    tk = sa_ref.shape[0]
    n_sub = tq // tk
    width = 2 * tk
    feat = lax.broadcasted_iota(jnp.int32, (2 * LANES, 1), 0)
    in_even = jnp.logical_or(feat < B_NOPE_DIM,
                             jnp.logical_and(feat >= LANES, feat < LANES + B_ROPE_DIM))
    in_odd = jnp.logical_and(jnp.logical_not(in_even), feat < LANES + 2 * B_ROPE_DIM)
    qT = qT_ref[0, 0]
    for t in range(n_sub):
        qt = qT[:, t * tk:(t + 1) * tk]
        qs_ref[:, t * width:t * width + tk] = jnp.where(in_even, qt, jnp.zeros_like(qt))
        qs_ref[:, t * width + tk:(t + 1) * width] = jnp.where(in_odd, qt, jnp.zeros_like(qt))
    n_full = qi * n_sub

    def keys(j):
        return k_ref[0, 0, pl.ds(pl.multiple_of(j * tk, tk), tk), :]

    def values(j):
        return vT_ref[0, 0, :, pl.ds(pl.multiple_of(j * tk, tk), tk)]

    def pv_dots(vT, p_of, subs):
        return jnp.concatenate(
            [jnp.dot(vT[hh * MLA_VROWS:(hh + 1) * MLA_VROWS], p_of(t, hh),
                     preferred_element_type=F32)
             for t in subs for hh in range(2)], axis=1)

    pbuf = ((pa_ref, aa_ref), (pb_ref, ab_ref))

    def probs(j, slot, first_sub=0, diagonal=False):
        p_ref, a_ref = pbuf[slot]
        c0 = first_sub * width
        sT = jnp.dot(keys(j), qs_ref[:, c0:], preferred_element_type=F32)
        if diagonal:
            masked = sT[:, :width] + mask_ref[...]
            sT = masked if first_sub == n_sub - 1 else jnp.concatenate([masked, sT[:, width:]], axis=1)
        m_old = m_ref[:, c0:]
        m_cur = jnp.max(sT, axis=0, keepdims=True)
        p_ref[:, c0:] = jnp.exp2(sT - m_old).astype(BF16)
        m_new = jnp.maximum(m_old, m_cur)
        a_ref[:, c0:] = jnp.exp2(m_old - m_new)
        jump_ref[:, c0:] = jnp.maximum(jump_ref[:, c0:], m_cur - m_old)
        m_ref[:, c0:] = m_new

    def pv(j, slot, first_sub=0):
        p_ref, a_ref = pbuf[slot]
        c0 = first_sub * width
        out = pv_dots(values(jnp.maximum(j, 0)),
                      lambda t, hh: p_ref[:, t * width + hh * tk:t * width + (hh + 1) * tk],
                      range(first_sub, n_sub))
        acc_ref[:, c0:] = (acc_ref[:, c0:] + out) * a_ref[:, c0:]

    acc_ref[...] = jnp.zeros(acc_ref.shape, F32)
    jump_ref[...] = jnp.zeros(jump_ref.shape, F32)
    m_ref[...] = jnp.dot(k_ref[0, 0, :SUBLANES_BF16, :], qs_ref[...],
                         preferred_element_type=F32)[:1]
    pb_ref[...] = jnp.zeros(pb_ref.shape, BF16)
    ab_ref[...] = jnp.ones(ab_ref.shape, F32)

    def four_blocks(u, carry):
        for r in range(4):
            probs(4 * u + r, r % 2)
            pv(4 * u + r - 1, (r + 1) % 2)
        return carry

    lax.fori_loop(0, n_full // 4, four_blocks, 0)
    for r in range(n_sub):
        probs(n_full + r, r % 2, first_sub=r, diagonal=True)
        pv(n_full + r - 1, (r + 1) % 2, first_sub=max(r - 1, 0))
    pv(n_full + n_sub - 1, (n_sub - 1) % 2, first_sub=n_sub - 1)

    def scores(j, first_sub=0):
        c0 = first_sub * width
        sT = jnp.dot(keys(j), qs_ref[:, c0:], preferred_element_type=F32)
        sa_ref[:, c0:] = sT
        xa_ref[:, c0:] = jnp.max(sT, axis=0, keepdims=True)

    def softmax_pv(j, subs, diagonal=False):
        c0, c1 = subs[0] * width, (subs[-1] + 1) * width
        sT = sa_ref[:, c0:c1]
        if diagonal:
            sT = sT + mask_ref[...]
            m_cur = jnp.max(sT, axis=0, keepdims=True)
        else:
            m_cur = xa_ref[:, c0:c1]
        m_old = m_ref[:, c0:c1]
        m_new = jnp.maximum(m_old, m_cur)
        alpha = jnp.exp2(m_old - m_new)
        pTb = jnp.exp2(sT - m_new).astype(BF16)
        out = pv_dots(values(j), lambda t, hh: pTb[:, (t - subs[0]) * width + hh * tk:
                                                   (t - subs[0]) * width + (hh + 1) * tk], subs)
        acc_ref[:, c0:c1] = alpha * acc_ref[:, c0:c1] + out
        m_ref[:, c0:c1] = m_new

    @pl.when(jnp.max(jump_ref[...]) > MLA_MAX_JUMP)
    def _():
        m_ref[...] = jnp.full(m_ref.shape, NEG_INF, F32)
        acc_ref[...] = jnp.zeros(acc_ref.shape, F32)

        def one_block(j, carry):
            scores(j)
            softmax_pv(j, tuple(range(n_sub)))
            return carry

        lax.fori_loop(0, n_full, one_block, 0)
        for r in range(n_sub):
            scores(n_full + r, first_sub=r)
            softmax_pv(n_full + r, (r,), diagonal=True)
            if r + 1 < n_sub:
                softmax_pv(n_full + r, tuple(range(r + 1, n_sub)))

    oT = acc_ref[:B_V_DIM, :] / acc_ref[B_V_DIM:B_V_DIM + 1, :]
    for t in range(n_sub):
        both = jnp.concatenate([oT[:, t * width:t * width + tk],
                                oT[:, t * width + tk:(t + 1) * width]], axis=0)
        o_ref[0, 0, t * tk:(t + 1) * tk, :] = both.T.astype(BF16)


def _mla_call(qT, k, vT, mask):
    B, P, _, S = qT.shape
    tq, tk = MLA_QTILE, MLA_KTILE
    assert tq % (4 * tk) == 0
    return pl.pallas_call(
        _mla_body,
        grid=(B, P, S // tq),
        in_specs=[pl.BlockSpec((1, 1, 2 * LANES, tq), lambda b, p, i: (b, p, 0, i)),
                  pl.BlockSpec((1, 1, S, 2 * LANES), lambda b, p, i: (b, p, 0, 0)),
                  pl.BlockSpec((1, 1, 2 * MLA_VROWS, S), lambda b, p, i: (b, p, 0, 0)),
                  _resident(mask.shape)],
        out_specs=pl.BlockSpec((1, 1, tq, LANES), lambda b, p, i: (b, p, i, 0)),
        out_shape=jax.ShapeDtypeStruct((B, P, S, LANES), BF16),
        scratch_shapes=[pltpu.VMEM((2 * LANES, 2 * tq), BF16),
                        pltpu.VMEM((tk, 2 * tq), F32),
                        pltpu.VMEM((1, 2 * tq), F32),
                        pltpu.VMEM((tk, 2 * tq), BF16),
                        pltpu.VMEM((tk, 2 * tq), BF16),
                        pltpu.VMEM((1, 2 * tq), F32),
                        pltpu.VMEM((1, 2 * tq), F32),
                        pltpu.VMEM((1, 2 * tq), F32),
                        pltpu.VMEM((1, 2 * tq), F32),
                        pltpu.VMEM((MLA_VROWS, 2 * tq), F32)],
        compiler_params=_params(52, 3),
        name="mla_attn",
    )(qT, k, vT, mask)


def _vec_rows(rows, batch, d):
    full = [jnp.broadcast_to(r, (batch, d)) for r in rows]
    full += [jnp.zeros((batch, d), F32)] * (SUBLANES_F32 - len(full))
    return jnp.stack(full, axis=1)


def _pad_rows(rows, width):
    full = [r.reshape(1, width) for r in rows]
    full += [jnp.zeros((1, width), F32)] * (SUBLANES_F32 - len(full))
    return jnp.concatenate(full, axis=0)


def _pair_rope_cols(w, heads):
    kdim = w.shape[0]
    half = B_ROPE_DIM // 2
    wh = w.reshape(kdim, heads // 2, 2 * B_ROPE_DIM)
    pad = jnp.zeros((kdim, heads // 2, LANES - 2 * B_ROPE_DIM), w.dtype)
    plain = jnp.concatenate([wh, pad], axis=-1).reshape(kdim, -1)
    w4 = w.reshape(kdim, heads, 2, half)
    sw = jnp.concatenate([w4[:, :, 1], w4[:, :, 0]], axis=-1).reshape(kdim, heads // 2, 2 * B_ROPE_DIM)
    swapped = jnp.concatenate([sw, pad], axis=-1).reshape(kdim, -1)
    return plain, swapped


def _diagonal_mask():
    kc = np.arange(MLA_KTILE)[:, None] // CHUNK
    qc = (np.arange(2 * MLA_KTILE)[None, :] % MLA_KTILE) // CHUNK
    return jnp.asarray(np.where(kc <= qc, 0.0, NEG_INF), F32)


def _bias_table(rel_bias):
    heads = rel_bias.shape[0]
    n, w = A_SUB, A_WIN
    near, far = n - 1 - A_MAX_REL, w - 1 - A_MAX_REL
    by_dist = jnp.concatenate(
        [jnp.broadcast_to(rel_bias[:, :1], (heads, max(near, 0))),
         rel_bias[:, max(-near, 0):],
         jnp.broadcast_to(rel_bias[:, -1:], (heads, far))], axis=1).astype(F32)
    length = n - 1 + w
    flat = jnp.tile(by_dist[:, ::-1], (1, n))[:, n - 1:n - 1 + n * (length - 1)]
    table = flat.reshape(heads, n, length - 1)[:, :, :w]
    ql = np.arange(n)[:, None]
    kl = np.arange(w)[None, :]
    band = kl // CHUNK - ql // CHUNK
    in_band = np.logical_and(band >= 0, band <= A_LEFT_CHUNKS)
    table = jnp.where(in_band[None], table * math.log2(math.e), NEG_INF)
    return jnp.swapaxes(table, 1, 2).reshape(heads // 2, 2, w, n).transpose(0, 2, 1, 3).reshape(
        heads // 2, w, 2 * n)


def kernel(x, c, positions, mod_w, mod_b, norm1_g, norm2_g, a_wqkv, a_wo, a_rel_bias, kv_mod_w, kv_mod_b, kv_norm_g, b_wdkv, b_kv_lat_norm_g, b_wuk, b_wuv, b_wkr, b_wdq, b_q_norm_g, b_wuq, b_wqr, b_wo, f_win, f_conv_w, f_conv_b, f_wout, final_g):
    B, S, D = x.shape
    depth = mod_w.shape[0]
    ffn_dim = f_wout.shape[1]

    c_pad = jnp.pad(c, ((0, SUBLANES_F32 - B), (0, 0)))
    mod = _mod_call(c_pad, mod_w, mod_b.reshape(depth, 1, 6 * D))[:, :B]
    kv_mod = _mod_call(c_pad, kv_mod_w[None], kv_mod_b.reshape(1, 1, 2 * D))[0, :B]
    mods = [[mod[l, :, k * D:(k + 1) * D] for k in range(6)] for l in range(depth)]

    half = B_ROPE_DIM // 2
    inv_freq = jnp.power(jnp.float32(ROPE_THETA),
                         -jnp.arange(half, dtype=F32) * (2.0 / B_ROPE_DIM))
    ang = positions.astype(F32)[..., None] * inv_freq
    cos, sin = jnp.cos(ang), jnp.sin(ang)
    cos_t = jnp.concatenate([cos, cos], axis=-1)
    sin_t = jnp.concatenate([-sin, sin], axis=-1)

    win_all = f_win.astype(BF16)
    wout_all = f_wout.astype(BF16)

    def conv_rows(l):
        return _pad_rows([f_conv_w[l, t] for t in range(CONV_WIDTH)] + [f_conv_b[l]], 2 * ffn_dim)

    sh1, sc1, g1, sh2, sc2, g2 = mods[0]
    qT, k, vT = _qkv_call(x, _vec_rows([norm1_g[0], sh1, sc1], B, D), a_wqkv[0].astype(BF16),
                          float(A_HEAD_DIM ** -0.5 * math.log2(math.e)))
    o = _chunk_attn_call(qT, k, vT, _bias_table(a_rel_bias[0]))
    h = _ffn_call(o, x, _vec_rows([g1, norm2_g[0], sh2, sc2, g2, final_g], B, D),
                  a_wo[0].astype(BF16), win_all, conv_rows(0), wout_all, 0, final_norm=False)

    kr_plain, kr_swapped = _pair_rope_cols(jnp.concatenate([b_wkr, b_wkr], axis=1), 2)
    w1 = jnp.concatenate([b_wdkv, kr_plain, kr_swapped], axis=1).astype(BF16)
    wu = jnp.concatenate([b_wuk, b_wuv], axis=1).astype(BF16)
    k_cat, v = _kv_call(h, _vec_rows([kv_norm_g, kv_mod[:, :D], kv_mod[:, D:]], B, D),
                        _pad_rows([b_kv_lat_norm_g], b_kv_lat_norm_g.shape[0]), w1, wu,
                        cos_t, sin_t)

    sh1, sc1, g1, sh2, sc2, g2 = mods[1]
    qr_plain, qr_swapped = _pair_rope_cols(b_wqr[0], B_HEADS)
    wq = jnp.concatenate([b_wuq[0], qr_plain, qr_swapped], axis=1).astype(BF16)
    score_scale = float((B_NOPE_DIM + B_ROPE_DIM) ** -0.5 * math.log2(math.e))
    q_cat = _q_call(h, _vec_rows([norm1_g[1], sh1, sc1], B, D),
                    _pad_rows([b_q_norm_g[0]], b_q_norm_g.shape[1]),
                    b_wdq[0].astype(BF16), wq, cos_t, sin_t, score_scale)
    o = _mla_call(q_cat, k_cat, v, _diagonal_mask())
    return _ffn_call(o, h, _vec_rows([g1, norm2_g[1], sh2, sc2, g2, final_g], B, D),
                     b_wo[0].astype(BF16), win_all, conv_rows(1), wout_all, 1, final_norm=True)
```

```python
import functools
import math

import jax
import jax.numpy as jnp
import numpy as np
from jax import lax
from jax.experimental import pallas as pl
from jax.experimental.pallas import tpu as pltpu

F32 = jnp.float32
BF16 = jnp.bfloat16

CHUNK = 64
A_HEADS = 16
A_HEAD_DIM = 64
A_LEFT_CHUNKS = 8
A_MAX_REL = 2 * CHUNK
B_HEADS = 16
B_NOPE_DIM = 64
B_ROPE_DIM = 32
B_V_DIM = 64
ROPE_THETA = 10000.0
CONV_WIDTH = 3
NORM_EPS = 1e-6
NEG_INF = -1e30

LANES = 128
MXU_COLS = 256
SUBLANES_F32 = 8
SUBLANES_BF16 = 16
VMEM_BYTES = 64 * 1024 * 1024

HEAD_PAIRS = 8
HALO = SUBLANES_BF16
ROW_TILE = 512
PROJ_TILE = 1024
A_QTILE = 1024
A_SUB = 2 * CHUNK
A_PREV = A_LEFT_CHUNKS * CHUNK
A_WIN = A_PREV + A_SUB
MLA_QTILE = 2048
MLA_KTILE = 512
MLA_MAX_JUMP = 64.0
MLA_VROWS = 64 + SUBLANES_BF16
FFN_CHUNK = MXU_COLS


def _params(vmem_mib, n_axes):
    return pltpu.CompilerParams(
        dimension_semantics=("arbitrary",) * n_axes,
        vmem_limit_bytes=vmem_mib * 1024 * 1024)


def _resident(shape):
    zeros = (0,) * len(shape)
    return pl.BlockSpec(shape, lambda *_: zeros, pipeline_mode=pl.Buffered(1))


def _resident_layer(shape, layer):
    return pl.BlockSpec((None,) + tuple(shape[1:]), lambda *_: (layer,) + (0,) * (len(shape) - 1),
                        pipeline_mode=pl.Buffered(1))


def _rms(x, g):
    return x * lax.rsqrt(jnp.mean(x * x, axis=-1, keepdims=True) + NORM_EPS) * g


def _norm_mod(x, g, shift, scale):
    return _rms(x, g) * (1.0 + scale) + shift


def _lane_tile(x):
    return jnp.concatenate([x] * (LANES // x.shape[1]), axis=1)


def _silu(x):
    return x * (1.0 / (1.0 + jnp.exp(-x)))


def _mod_body(c_ref, w_ref, b_ref, o_ref):
    ca = _silu(c_ref[...]).astype(BF16)
    o_ref[0] = jnp.dot(ca, w_ref[0].astype(BF16), preferred_element_type=F32) + b_ref[0]


def _mod_call(c_pad, w, b, tn=1024):
    L, D, N = w.shape
    return pl.pallas_call(
        _mod_body,
        grid=(L, N // tn),
        in_specs=[pl.BlockSpec((SUBLANES_F32, D), lambda l, n: (0, 0)),
                  pl.BlockSpec((1, D, tn), lambda l, n: (l, 0, n)),
                  pl.BlockSpec((1, 1, tn), lambda l, n: (l, 0, n))],
        out_specs=pl.BlockSpec((1, SUBLANES_F32, tn), lambda l, n: (l, 0, n)),
        out_shape=jax.ShapeDtypeStruct((L, SUBLANES_F32, N), F32),
        compiler_params=_params(32, 2),
        name="mod",
    )(c_pad, w, b)


def _qkv_body(x_ref, vec_ref, w_ref, qT_ref, k_ref, vT_ref, hn_ref, *, score_scale):
    vec = vec_ref[0]
    hn_ref[...] = _norm_mod(x_ref[0], vec[0:1], vec[1:2], vec[2:3]).astype(BF16)
    width = HEAD_PAIRS * LANES
    for c in range(HEAD_PAIRS // 2):
        def proj(base):
            cols = slice(base + c * MXU_COLS, base + (c + 1) * MXU_COLS)
            return jnp.dot(hn_ref[...], w_ref[:, cols], preferred_element_type=F32)
        q = proj(0) * score_scale
        k = proj(width)
        v = proj(2 * width)
        for half in range(2):
            g = slice(half * LANES, (half + 1) * LANES)
            qT_ref[0, 2 * c + half] = q[:, g].T.astype(BF16)
            k_ref[0, 2 * c + half] = k[:, g].astype(BF16)
            vT_ref[0, 2 * c + half] = v[:, g].T.astype(BF16)


def _qkv_call(x, vec, w, score_scale):
    B, S, D = x.shape
    tm = PROJ_TILE
    rows = pl.BlockSpec((1, HEAD_PAIRS, tm, LANES), lambda b, i: (b, 0, i, 0))
    cols = pl.BlockSpec((1, HEAD_PAIRS, LANES, tm), lambda b, i: (b, 0, 0, i))
    return pl.pallas_call(
        functools.partial(_qkv_body, score_scale=score_scale),
        grid=(B, S // tm),
        in_specs=[pl.BlockSpec((1, tm, D), lambda b, i: (b, i, 0)),
                  pl.BlockSpec((1, SUBLANES_F32, D), lambda b, i: (b, 0, 0)),
                  _resident(w.shape)],
        out_specs=[cols, rows, cols],
        out_shape=[jax.ShapeDtypeStruct((B, HEAD_PAIRS, LANES, S), BF16),
                   jax.ShapeDtypeStruct((B, HEAD_PAIRS, S, LANES), BF16),
                   jax.ShapeDtypeStruct((B, HEAD_PAIRS, LANES, S), BF16)],
        scratch_shapes=[pltpu.VMEM((tm, D), BF16)],
        compiler_params=_params(40, 2),
        name="qkv",
    )(x, vec, w)


def _chunk_attn_body(qT_ref, kp_ref, kc_ref, vTp_ref, vTc_ref, bias_ref, o_ref, qs_ref, s_ref):
    i = pl.program_id(2)
    n_sub = A_QTILE // A_SUB
    feat = lax.broadcasted_iota(jnp.int32, (LANES, 1), 0)
    qT = qT_ref[0, 0]
    q_even = jnp.where(feat < A_HEAD_DIM, qT, jnp.zeros_like(qT))
    q_odd = jnp.where(feat >= A_HEAD_DIM, qT, jnp.zeros_like(qT))
    for j in range(n_sub):
        cols = slice(j * A_SUB, (j + 1) * A_SUB)
        qs_ref[j, :, :A_SUB] = q_even[:, cols]
        qs_ref[j, :, A_SUB:] = q_odd[:, cols]

    def window(prev_ref, cur_ref, lo, axis):
        idx = lambda s: (0, 0, s, slice(None)) if axis == 0 else (0, 0, slice(None), s)
        if lo >= A_PREV:
            return cur_ref[idx(slice(lo - A_PREV, lo + A_SUB))]
        return jnp.concatenate([prev_ref[idx(slice(lo, A_PREV))],
                                cur_ref[idx(slice(0, lo + A_SUB))]], axis=axis)

    def scores(j):
        k2 = window(kp_ref, kc_ref, j * A_SUB, 0)
        half = A_WIN // 2
        s_ref[j % 2, :half] = jnp.dot(k2[:half], qs_ref[j], preferred_element_type=F32)
        s_ref[j % 2, half:] = jnp.dot(k2[half:], qs_ref[j], preferred_element_type=F32)

    def softmax_pv(j, first_tile):
        lo = j * A_SUB
        sT = s_ref[j % 2] + bias_ref[0]
        if first_tile and lo < A_PREV:
            row = lax.broadcasted_iota(jnp.int32, (A_WIN, 1), 0)
            sT = jnp.where(row >= A_PREV - lo, sT, NEG_INF)
        m = jnp.max(sT, axis=0, keepdims=True)
        pT = jnp.exp2(sT - m)
        l = jnp.sum(pT, axis=0, keepdims=True)
        pTb = pT.astype(BF16)
        vT = window(vTp_ref, vTc_ref, lo, 1)
        pv = jnp.concatenate(
            [jnp.dot(vT[:A_HEAD_DIM], pTb[:, :A_SUB], preferred_element_type=F32),
             jnp.dot(vT[A_HEAD_DIM:], pTb[:, A_SUB:], preferred_element_type=F32)], axis=1)
        oT = pv / l
        o_ref[0, 0, lo:lo + A_SUB, :] = jnp.concatenate(
            [oT[:, :A_SUB], oT[:, A_SUB:]], axis=0).T.astype(BF16)

    def tile(first_tile):
        scores(0)
        for j in range(n_sub):
            if j + 1 < n_sub:
                scores(j + 1)
            softmax_pv(j, first_tile)

    @pl.when(i == 0)
    def _():
        tile(True)

    @pl.when(i > 0)
    def _():
        tile(False)


def _chunk_attn_call(qT, k, vT, bias):
    B, _, _, S = qT.shape
    tq = A_QTILE
    ratio = tq // A_PREV
    prev_idx = lambda i: jnp.maximum(i * ratio - 1, 0)
    rows = lambda n: (1, 1, n, LANES)
    cols = lambda n: (1, 1, LANES, n)
    return pl.pallas_call(
        _chunk_attn_body,
        grid=(HEAD_PAIRS, B, S // tq),
        in_specs=[pl.BlockSpec(cols(tq), lambda p, b, i: (b, p, 0, i)),
                  pl.BlockSpec(rows(A_PREV), lambda p, b, i: (b, p, prev_idx(i), 0)),
                  pl.BlockSpec(rows(tq), lambda p, b, i: (b, p, i, 0)),
                  pl.BlockSpec(cols(A_PREV), lambda p, b, i: (b, p, 0, prev_idx(i))),
                  pl.BlockSpec(cols(tq), lambda p, b, i: (b, p, 0, i)),
                  pl.BlockSpec((1, A_WIN, 2 * A_SUB), lambda p, b, i: (p, 0, 0))],
        out_specs=pl.BlockSpec(rows(tq), lambda p, b, i: (b, p, i, 0)),
        out_shape=jax.ShapeDtypeStruct((B, HEAD_PAIRS, S, LANES), BF16),
        scratch_shapes=[pltpu.VMEM((tq // A_SUB, LANES, 2 * A_SUB), BF16),
                        pltpu.VMEM((2, A_WIN, 2 * A_SUB), F32)],
        compiler_params=_params(32, 3),
        name="chunk_attn",
    )(qT, k, k, vT, vT, bias)


def _ffn_body(o_ref, oh_ref, h_ref, hh_ref, vec_ref, wo_ref, win_ref, cw_ref, wout_ref,
              out_ref, hn_ref, h1_ref, acc_ref, u_ref, *, final_norm):
    i = pl.program_id(1)
    tm = h_ref.shape[1]
    vec = vec_ref[0]
    g1, n2g, sh2, sc2, g2, fg = (vec[r:r + 1] for r in range(6))

    o_cat = jnp.concatenate(
        [jnp.concatenate([oh_ref[0, p] for p in range(HEAD_PAIRS)], axis=1),
         jnp.concatenate([o_ref[0, p] for p in range(HEAD_PAIRS)], axis=1)], axis=0)
    h_cat = jnp.concatenate([hh_ref[0], h_ref[0]], axis=0)
    h1 = h_cat + g1 * jnp.dot(o_cat, wo_ref[...], preferred_element_type=F32)
    hn = _norm_mod(h1, n2g, sh2, sc2)
    row = lax.broadcasted_iota(jnp.int32, (tm + HALO, 1), 0)
    hn = jnp.where(jnp.logical_and(row < HALO, i == 0), 0.0, hn)
    hn_ref[...] = hn.astype(BF16)
    h1_ref[...] = h1[HALO:]

    ffn_dim = wout_ref.shape[0]
    groups = FFN_CHUNK // LANES

    def slabs(f, base):
        return (f % 2) * 2 * groups + (groups if base >= ffn_dim else 0)

    def up(f, base):
        u = jnp.dot(hn_ref[...], win_ref[:, base:base + FFN_CHUNK], preferred_element_type=F32)
        for g in range(groups):
            u_ref[slabs(f, base) + g] = u[:, g * LANES:(g + 1) * LANES]

    def conv(f, base):
        outs = []
        for g in range(groups):
            cols = slice(base + g * LANES, base + (g + 1) * LANES)
            slab = slabs(f, base) + g
            y = cw_ref[3:4, cols] + u_ref[slab, pl.ds(HALO - 2, tm), :] * cw_ref[0:1, cols]
            y = y + u_ref[slab, pl.ds(HALO - 1, tm), :] * cw_ref[1:2, cols]
            outs.append(y + u_ref[slab, pl.ds(HALO, tm), :] * cw_ref[2:3, cols])
        return jnp.concatenate(outs, axis=1)

    def down(f, act):
        part = jnp.dot(act, wout_ref[f * FFN_CHUNK:(f + 1) * FFN_CHUNK, :],
                       preferred_element_type=F32)
        if f == 0:
            acc_ref[...] = part
        else:
            acc_ref[...] += part

    n_chunks = ffn_dim // FFN_CHUNK
    up(0, 0)
    up(0, ffn_dim)
    act = None
    for f in range(n_chunks):
        if f + 1 < n_chunks:
            up(f + 1, (f + 1) * FFN_CHUNK)
            up(f + 1, ffn_dim + (f + 1) * FFN_CHUNK)
        if f > 0:
            down(f - 1, act)
        act = (_silu(conv(f, f * FFN_CHUNK)) * conv(f, ffn_dim + f * FFN_CHUNK)).astype(BF16)
    down(n_chunks - 1, act)

    h2 = h1_ref[...] + g2 * acc_ref[...]
    out_ref[0] = _rms(h2, fg) if final_norm else h2


def _ffn_call(o, h, vec, wo, win, cw, wout, layer, final_norm):
    B, S, D = h.shape
    tm = ROW_TILE
    halo_blocks = tm // HALO
    halo_idx = lambda i: jnp.maximum(i * halo_blocks - 1, 0)
    return pl.pallas_call(
        functools.partial(_ffn_body, final_norm=final_norm),
        grid=(B, S // tm),
        in_specs=[pl.BlockSpec((1, HEAD_PAIRS, tm, LANES), lambda b, i: (b, 0, i, 0)),
                  pl.BlockSpec((1, HEAD_PAIRS, HALO, LANES), lambda b, i: (b, 0, halo_idx(i), 0)),
                  pl.BlockSpec((1, tm, D), lambda b, i: (b, i, 0)),
                  pl.BlockSpec((1, HALO, D), lambda b, i: (b, halo_idx(i), 0)),
                  pl.BlockSpec((1, SUBLANES_F32, D), lambda b, i: (b, 0, 0)),
                  _resident(wo.shape), _resident_layer(win.shape, layer), _resident(cw.shape),
                  _resident_layer(wout.shape, layer)],
        out_specs=pl.BlockSpec((1, tm, D), lambda b, i: (b, i, 0)),
        out_shape=jax.ShapeDtypeStruct((B, S, D), F32),
        scratch_shapes=[pltpu.VMEM((tm + HALO, D), BF16),
                        pltpu.VMEM((tm, D), F32),
                        pltpu.VMEM((tm, D), F32),
                        pltpu.VMEM((4 * FFN_CHUNK // LANES, tm + HALO, LANES), F32)],
        compiler_params=_params(56, 2),
        name="ffn_final" if final_norm else "ffn",
    )(o, o, h, h, vec, wo, win, cw, wout)


def _kv_body(h_ref, vec_ref, lg_ref, w1_ref, wu_ref, cos_ref, sin_ref, k_ref, v_ref):
    vec = vec_ref[0]
    hn = _norm_mod(h_ref[0], vec[0:1], vec[1:2], vec[2:3]).astype(BF16)
    t = jnp.dot(hn, w1_ref[...], preferred_element_type=F32)
    lat = lg_ref.shape[1]
    ckv = _rms(t[:, :lat], lg_ref[0:1]).astype(BF16)
    cos, sin = _lane_tile(cos_ref[0]), _lane_tile(sin_ref[0])
    kr = (t[:, lat:lat + LANES] * cos + t[:, lat + LANES:] * sin).astype(BF16)
    kv = jnp.dot(ckv, wu_ref[...], preferred_element_type=F32)
    half = HEAD_PAIRS * LANES
    for p in range(HEAD_PAIRS):
        k_ref[0, p, :, :LANES] = kv[:, p * LANES:(p + 1) * LANES].astype(BF16)
        k_ref[0, p, :, LANES:] = kr
        vt = kv[:, half + p * LANES:half + (p + 1) * LANES].T.astype(BF16)
        ones = jnp.ones((MLA_VROWS - B_V_DIM, vt.shape[1]), BF16)
        for hh in range(2):
            v_ref[0, p, hh * MLA_VROWS:hh * MLA_VROWS + B_V_DIM] = vt[hh * B_V_DIM:(hh + 1) * B_V_DIM]
            v_ref[0, p, hh * MLA_VROWS + B_V_DIM:(hh + 1) * MLA_VROWS] = ones


def _kv_call(h, vec, lat_g, w1, wu, cos, sin):
    B, S, D = h.shape
    tm = PROJ_TILE
    row = lambda b, i: (b, i, 0)
    return pl.pallas_call(
        _kv_body,
        grid=(B, S // tm),
        in_specs=[pl.BlockSpec((1, tm, D), row),
                  pl.BlockSpec((1, SUBLANES_F32, D), lambda b, i: (b, 0, 0)),
                  _resident(lat_g.shape), _resident(w1.shape), _resident(wu.shape),
                  pl.BlockSpec((1, tm, B_ROPE_DIM), row), pl.BlockSpec((1, tm, B_ROPE_DIM), row)],
        out_specs=[pl.BlockSpec((1, HEAD_PAIRS, tm, 2 * LANES), lambda b, i: (b, 0, i, 0)),
                   pl.BlockSpec((1, HEAD_PAIRS, 2 * MLA_VROWS, tm), lambda b, i: (b, 0, 0, i))],
        out_shape=[jax.ShapeDtypeStruct((B, HEAD_PAIRS, S, 2 * LANES), BF16),
                   jax.ShapeDtypeStruct((B, HEAD_PAIRS, 2 * MLA_VROWS, S), BF16)],
        compiler_params=_params(48, 2),
        name="shared_kv",
    )(h, vec, lat_g, w1, wu, cos, sin)


def _q_body(h_ref, vec_ref, qg_ref, wdq_ref, wq_ref, cos_ref, sin_ref, q_ref, *, score_scale):
    vec = vec_ref[0]
    hn = _norm_mod(h_ref[0], vec[0:1], vec[1:2], vec[2:3]).astype(BF16)
    cq = _rms(jnp.dot(hn, wdq_ref[...], preferred_element_type=F32), qg_ref[0:1]).astype(BF16)
    cos = _lane_tile(cos_ref[0])
    sin = _lane_tile(sin_ref[0])
    group = 3 * MXU_COLS

    def project(c):
        return jnp.dot(cq, wq_ref[:, c * group:(c + 1) * group], preferred_element_type=F32)

    t = project(0)
    for c in range(HEAD_PAIRS // 2):
        nxt = project(c + 1) if c + 1 < HEAD_PAIRS // 2 else None
        for half in range(2):
            g = slice(half * LANES, (half + 1) * LANES)
            rope = t[:, MXU_COLS:2 * MXU_COLS][:, g] * cos + t[:, 2 * MXU_COLS:][:, g] * sin
            qp = jnp.concatenate([t[:, g], rope], axis=1) * score_scale
            q_ref[0, 2 * c + half] = qp.T.astype(BF16)
        t = nxt


def _q_call(h, vec, q_g, wdq, wq, cos, sin, score_scale):
    B, S, D = h.shape
    tm = PROJ_TILE
    row = lambda b, i: (b, i, 0)
    return pl.pallas_call(
        functools.partial(_q_body, score_scale=score_scale),
        grid=(B, S // tm),
        in_specs=[pl.BlockSpec((1, tm, D), row),
                  pl.BlockSpec((1, SUBLANES_F32, D), lambda b, i: (b, 0, 0)),
                  _resident(q_g.shape), _resident(wdq.shape), _resident(wq.shape),
                  pl.BlockSpec((1, tm, B_ROPE_DIM), row), pl.BlockSpec((1, tm, B_ROPE_DIM), row)],
        out_specs=pl.BlockSpec((1, HEAD_PAIRS, 2 * LANES, tm), lambda b, i: (b, 0, 0, i)),
        out_shape=jax.ShapeDtypeStruct((B, HEAD_PAIRS, 2 * LANES, S), BF16),
        compiler_params=_params(52, 2),
        name="mla_q",
    )(h, vec, q_g, wdq, wq, cos, sin)


def _mla_body(qT_ref, k_ref, vT_ref, mask_ref, o_ref, qs_ref, sa_ref, xa_ref,
              pa_ref, pb_ref, aa_ref, ab_ref, m_ref, jump_ref, acc_ref):
    qi = pl.program_id(2)
    tq = qT_ref.shape[3]
    tk = sa_ref.shape[0]
    n_sub = tq // tk
    width = 2 * tk
    feat = lax.broadcasted_iota(jnp.int32, (2 * LANES, 1), 0)
    in_even = jnp.logical_or(feat < B_NOPE_DIM,
                             jnp.logical_and(feat >= LANES, feat < LANES + B_ROPE_DIM))
    in_odd = jnp.logical_and(jnp.logical_not(in_even), feat < LANES + 2 * B_ROPE_DIM)
    qT = qT_ref[0, 0]
    for t in range(n_sub):
        qt = qT[:, t * tk:(t + 1) * tk]
        qs_ref[:, t * width:t * width + tk] = jnp.where(in_even, qt, jnp.zeros_like(qt))
        qs_ref[:, t * width + tk:(t + 1) * width] = jnp.where(in_odd, qt, jnp.zeros_like(qt))
    n_full = qi * n_sub

    def keys(j):
        return k_ref[0, 0, pl.ds(pl.multiple_of(j * tk, tk), tk), :]

    def values(j):
        return vT_ref[0, 0, :, pl.ds(pl.multiple_of(j * tk, tk), tk)]

    def pv_dots(vT, p_of, subs):
        return jnp.concatenate(
            [jnp.dot(vT[hh * MLA_VROWS:(hh + 1) * MLA_VROWS], p_of(t, hh),
                     preferred_element_type=F32)
             for t in subs for hh in range(2)], axis=1)

    pbuf = ((pa_ref, aa_ref), (pb_ref, ab_ref))

    def probs(j, slot, first_sub=0, diagonal=False):
        p_ref, a_ref = pbuf[slot]
        c0 = first_sub * width
        sT = jnp.dot(keys(j), qs_ref[:, c0:], preferred_element_type=F32)
        if diagonal:
            masked = sT[:, :width] + mask_ref[...]
            sT = masked if first_sub == n_sub - 1 else jnp.concatenate([masked, sT[:, width:]], axis=1)
        m_old = m_ref[:, c0:]
        m_cur = jnp.max(sT, axis=0, keepdims=True)
        p_ref[:, c0:] = jnp.exp2(sT - m_old).astype(BF16)
        m_new = jnp.maximum(m_old, m_cur)
        a_ref[:, c0:] = jnp.exp2(m_old - m_new)
        jump_ref[:, c0:] = jnp.maximum(jump_ref[:, c0:], m_cur - m_old)
        m_ref[:, c0:] = m_new

    def pv(j, slot, first_sub=0):
        p_ref, a_ref = pbuf[slot]
        c0 = first_sub * width
        out = pv_dots(values(jnp.maximum(j, 0)),
                      lambda t, hh: p_ref[:, t * width + hh * tk:t * width + (hh + 1) * tk],
                      range(first_sub, n_sub))
        acc_ref[:, c0:] = (acc_ref[:, c0:] + out) * a_ref[:, c0:]

    acc_ref[...] = jnp.zeros(acc_ref.shape, F32)
    jump_ref[...] = jnp.zeros(jump_ref.shape, F32)
    m_ref[...] = jnp.dot(k_ref[0, 0, :SUBLANES_BF16, :], qs_ref[...],
                         preferred_element_type=F32)[:1]
    pb_ref[...] = jnp.zeros(pb_ref.shape, BF16)
    ab_ref[...] = jnp.ones(ab_ref.shape, F32)

    def four_blocks(u, carry):
        for r in range(4):
            probs(4 * u + r, r % 2)
            pv(4 * u + r - 1, (r + 1) % 2)
        return carry

    lax.fori_loop(0, n_full // 4, four_blocks, 0)
    for r in range(n_sub):
        probs(n_full + r, r % 2, first_sub=r, diagonal=True)
        pv(n_full + r - 1, (r + 1) % 2, first_sub=max(r - 1, 0))
    pv(n_full + n_sub - 1, (n_sub - 1) % 2, first_sub=n_sub - 1)

    def scores(j, first_sub=0):
        c0 = first_sub * width
        sT = jnp.dot(keys(j), qs_ref[:, c0:], preferred_element_type=F32)
        sa_ref[:, c0:] = sT
        xa_ref[:, c0:] = jnp.max(sT, axis=0, keepdims=True)

    def softmax_pv(j, subs, diagonal=False):
        c0, c1 = subs[0] * width, (subs[-1] + 1) * width
        sT = sa_ref[:, c0:c1]
        if diagonal:
            sT = sT + mask_ref[...]
            m_cur = jnp.max(sT, axis=0, keepdims=True)
        else:
            m_cur = xa_ref[:, c0:c1]
        m_old = m_ref[:, c0:c1]
        m_new = jnp.maximum(m_old, m_cur)
        alpha = jnp.exp2(m_old - m_new)
        pTb = jnp.exp2(sT - m_new).astype(BF16)
        out = pv_dots(values(j), lambda t, hh: pTb[:, (t - subs[0]) * width + hh * tk:
                                                   (t - subs[0]) * width + (hh + 1) * tk], subs)
        acc_ref[:, c0:c1] = alpha * acc_ref[:, c0:c1] + out
        m_ref[:, c0:c1] = m_new

    @pl.when(jnp.max(jump_ref[...]) > MLA_MAX_JUMP)
    def _():
        m_ref[...] = jnp.full(m_ref.shape, NEG_INF, F32)
        acc_ref[...] = jnp.zeros(acc_ref.shape, F32)

        def one_block(j, carry):
            scores(j)
            softmax_pv(j, tuple(range(n_sub)))
            return carry

        lax.fori_loop(0, n_full, one_block, 0)
        for r in range(n_sub):
            scores(n_full + r, first_sub=r)
            softmax_pv(n_full + r, (r,), diagonal=True)
            if r + 1 < n_sub:
                softmax_pv(n_full + r, tuple(range(r + 1, n_sub)))

    oT = acc_ref[:B_V_DIM, :] / acc_ref[B_V_DIM:B_V_DIM + 1, :]
    for t in range(n_sub):
        both = jnp.concatenate([oT[:, t * width:t * width + tk],
                                oT[:, t * width + tk:(t + 1) * width]], axis=0)
        o_ref[0, 0, t * tk:(t + 1) * tk, :] = both.T.astype(BF16)


def _mla_call(qT, k, vT, mask):
    B, P, _, S = qT.shape
    tq, tk = MLA_QTILE, MLA_KTILE
    assert tq % (4 * tk) == 0
    return pl.pallas_call(
        _mla_body,
        grid=(B, P, S // tq),
        in_specs=[pl.BlockSpec((1, 1, 2 * LANES, tq), lambda b, p, i: (b, p, 0, i)),
                  pl.BlockSpec((1, 1, S, 2 * LANES), lambda b, p, i: (b, p, 0, 0)),
                  pl.BlockSpec((1, 1, 2 * MLA_VROWS, S), lambda b, p, i: (b, p, 0, 0)),
                  _resident(mask.shape)],
        out_specs=pl.BlockSpec((1, 1, tq, LANES), lambda b, p, i: (b, p, i, 0)),
        out_shape=jax.ShapeDtypeStruct((B, P, S, LANES), BF16),
        scratch_shapes=[pltpu.VMEM((2 * LANES, 2 * tq), BF16),
                        pltpu.VMEM((tk, 2 * tq), F32),
                        pltpu.VMEM((1, 2 * tq), F32),
                        pltpu.VMEM((tk, 2 * tq), BF16),
                        pltpu.VMEM((tk, 2 * tq), BF16),
                        pltpu.VMEM((1, 2 * tq), F32),
                        pltpu.VMEM((1, 2 * tq), F32),
                        pltpu.VMEM((1, 2 * tq), F32),
                        pltpu.VMEM((1, 2 * tq), F32),
                        pltpu.VMEM((MLA_VROWS, 2 * tq), F32)],
        compiler_params=_params(52, 3),
        name="mla_attn",
    )(qT, k, vT, mask)


def _vec_rows(rows, batch, d):
    full = [jnp.broadcast_to(r, (batch, d)) for r in rows]
    full += [jnp.zeros((batch, d), F32)] * (SUBLANES_F32 - len(full))
    return jnp.stack(full, axis=1)


def _pad_rows(rows, width):
    full = [r.reshape(1, width) for r in rows]
    full += [jnp.zeros((1, width), F32)] * (SUBLANES_F32 - len(full))
    return jnp.concatenate(full, axis=0)


def _pair_rope_cols(w, heads):
    kdim = w.shape[0]
    half = B_ROPE_DIM // 2
    wh = w.reshape(kdim, heads // 2, 2 * B_ROPE_DIM)
    pad = jnp.zeros((kdim, heads // 2, LANES - 2 * B_ROPE_DIM), w.dtype)
    plain = jnp.concatenate([wh, pad], axis=-1).reshape(kdim, -1)
    w4 = w.reshape(kdim, heads, 2, half)
    sw = jnp.concatenate([w4[:, :, 1], w4[:, :, 0]], axis=-1).reshape(kdim, heads // 2, 2 * B_ROPE_DIM)
    swapped = jnp.concatenate([sw, pad], axis=-1).reshape(kdim, -1)
    return plain, swapped


def _diagonal_mask():
    kc = np.arange(MLA_KTILE)[:, None] // CHUNK
    qc = (np.arange(2 * MLA_KTILE)[None, :] % MLA_KTILE) // CHUNK
    return jnp.asarray(np.where(kc <= qc, 0.0, NEG_INF), F32)


def _bias_table(rel_bias):
    heads = rel_bias.shape[0]
    n, w = A_SUB, A_WIN
    near, far = n - 1 - A_MAX_REL, w - 1 - A_MAX_REL
    by_dist = jnp.concatenate(
        [jnp.broadcast_to(rel_bias[:, :1], (heads, max(near, 0))),
         rel_bias[:, max(-near, 0):],
         jnp.broadcast_to(rel_bias[:, -1:], (heads, far))], axis=1).astype(F32)
    length = n - 1 + w
    flat = jnp.tile(by_dist[:, ::-1], (1, n))[:, n - 1:n - 1 + n * (length - 1)]
    table = flat.reshape(heads, n, length - 1)[:, :, :w]
    ql = np.arange(n)[:, None]
    kl = np.arange(w)[None, :]
    band = kl // CHUNK - ql // CHUNK
    in_band = np.logical_and(band >= 0, band <= A_LEFT_CHUNKS)
    table = jnp.where(in_band[None], table * math.log2(math.e), NEG_INF)
    return jnp.swapaxes(table, 1, 2).reshape(heads // 2, 2, w, n).transpose(0, 2, 1, 3).reshape(
        heads // 2, w, 2 * n)


def kernel(x, c, positions, mod_w, mod_b, norm1_g, norm2_g, a_wqkv, a_wo, a_rel_bias, kv_mod_w, kv_mod_b, kv_norm_g, b_wdkv, b_kv_lat_norm_g, b_wuk, b_wuv, b_wkr, b_wdq, b_q_norm_g, b_wuq, b_wqr, b_wo, f_win, f_conv_w, f_conv_b, f_wout, final_g):
    B, S, D = x.shape
    depth = mod_w.shape[0]
    ffn_dim = f_wout.shape[1]

    c_pad = jnp.pad(c, ((0, SUBLANES_F32 - B), (0, 0)))
    mod = _mod_call(c_pad, mod_w, mod_b.reshape(depth, 1, 6 * D))[:, :B]
    kv_mod = _mod_call(c_pad, kv_mod_w[None], kv_mod_b.reshape(1, 1, 2 * D))[0, :B]
    mods = [[mod[l, :, k * D:(k + 1) * D] for k in range(6)] for l in range(depth)]

    half = B_ROPE_DIM // 2
    inv_freq = jnp.power(jnp.float32(ROPE_THETA),
                         -jnp.arange(half, dtype=F32) * (2.0 / B_ROPE_DIM))
    ang = positions.astype(F32)[..., None] * inv_freq
    cos, sin = jnp.cos(ang), jnp.sin(ang)
    cos_t = jnp.concatenate([cos, cos], axis=-1)
    sin_t = jnp.concatenate([-sin, sin], axis=-1)

    win_all = f_win.astype(BF16)
    wout_all = f_wout.astype(BF16)

    def conv_rows(l):
        return _pad_rows([f_conv_w[l, t] for t in range(CONV_WIDTH)] + [f_conv_b[l]], 2 * ffn_dim)

    sh1, sc1, g1, sh2, sc2, g2 = mods[0]
    qT, k, vT = _qkv_call(x, _vec_rows([norm1_g[0], sh1, sc1], B, D), a_wqkv[0].astype(BF16),
                          float(A_HEAD_DIM ** -0.5 * math.log2(math.e)))
    o = _chunk_attn_call(qT, k, vT, _bias_table(a_rel_bias[0]))
    h = _ffn_call(o, x, _vec_rows([g1, norm2_g[0], sh2, sc2, g2, final_g], B, D),
                  a_wo[0].astype(BF16), win_all, conv_rows(0), wout_all, 0, final_norm=False)

    kr_plain, kr_swapped = _pair_rope_cols(jnp.concatenate([b_wkr, b_wkr], axis=1), 2)
    w1 = jnp.concatenate([b_wdkv, kr_plain, kr_swapped], axis=1).astype(BF16)
    wu = jnp.concatenate([b_wuk, b_wuv], axis=1).astype(BF16)
    k_cat, v = _kv_call(h, _vec_rows([kv_norm_g, kv_mod[:, :D], kv_mod[:, D:]], B, D),
                        _pad_rows([b_kv_lat_norm_g], b_kv_lat_norm_g.shape[0]), w1, wu,
                        cos_t, sin_t)

    sh1, sc1, g1, sh2, sc2, g2 = mods[1]
    qr_plain, qr_swapped = _pair_rope_cols(b_wqr[0], B_HEADS)
    wq = jnp.stack([b_wuq[0], qr_plain, qr_swapped], axis=1).reshape(
        b_wuq.shape[1], 3, HEAD_PAIRS // 2, MXU_COLS).transpose(0, 2, 1, 3).reshape(
        b_wuq.shape[1], -1).astype(BF16)
    score_scale = float((B_NOPE_DIM + B_ROPE_DIM) ** -0.5 * math.log2(math.e))
    q_cat = _q_call(h, _vec_rows([norm1_g[1], sh1, sc1], B, D),
                    _pad_rows([b_q_norm_g[0]], b_q_norm_g.shape[1]),
                    b_wdq[0].astype(BF16), wq, cos_t, sin_t, score_scale)
    o = _mla_call(q_cat, k_cat, v, _diagonal_mask())
    return _ffn_call(o, h, _vec_rows([g1, norm2_g[1], sh2, sc2, g2, final_g], B, D),
                     b_wo[0].astype(BF16), win_all, conv_rows(1), wout_all, 1, final_norm=True)
```

```python
import functools
import math

import jax
import jax.numpy as jnp
import numpy as np
from jax import lax
from jax.experimental import pallas as pl
from jax.experimental.pallas import tpu as pltpu

F32 = jnp.float32
BF16 = jnp.bfloat16

CHUNK = 64
A_HEADS = 16
A_HEAD_DIM = 64
A_LEFT_CHUNKS = 8
A_MAX_REL = 2 * CHUNK
B_HEADS = 16
B_NOPE_DIM = 64
B_ROPE_DIM = 32
B_V_DIM = 64
ROPE_THETA = 10000.0
CONV_WIDTH = 3
NORM_EPS = 1e-6
NEG_INF = -1e30

LANES = 128
MXU_COLS = 256
SUBLANES_F32 = 8
SUBLANES_BF16 = 16
VMEM_BYTES = 64 * 1024 * 1024

HEAD_PAIRS = 8
HALO = SUBLANES_BF16
ROW_TILE = 512
PROJ_TILE = 1024
A_QTILE = 2048
A_SUB = 2 * CHUNK
A_PREV = A_LEFT_CHUNKS * CHUNK
A_WIN = A_PREV + A_SUB
MLA_QTILE = 2048
MLA_KTILE = 512
MLA_MAX_JUMP = 64.0
MLA_VROWS = 64 + SUBLANES_BF16
FFN_CHUNK = MXU_COLS


def _params(vmem_mib, n_axes):
    return pltpu.CompilerParams(
        dimension_semantics=("arbitrary",) * n_axes,
        vmem_limit_bytes=vmem_mib * 1024 * 1024)


def _resident(shape):
    zeros = (0,) * len(shape)
    return pl.BlockSpec(shape, lambda *_: zeros, pipeline_mode=pl.Buffered(1))


def _resident_layer(shape, layer):
    return pl.BlockSpec((None,) + tuple(shape[1:]), lambda *_: (layer,) + (0,) * (len(shape) - 1),
                        pipeline_mode=pl.Buffered(1))


def _rms(x, g):
    return x * lax.rsqrt(jnp.mean(x * x, axis=-1, keepdims=True) + NORM_EPS) * g


def _norm_mod(x, g, shift, scale):
    return _rms(x, g) * (1.0 + scale) + shift


def _lane_tile(x):
    return jnp.concatenate([x] * (LANES // x.shape[1]), axis=1)


def _silu(x):
    return x * (1.0 / (1.0 + jnp.exp(-x)))


def _mod_body(c_ref, w_ref, b_ref, o_ref):
    ca = _silu(c_ref[...]).astype(BF16)
    o_ref[0] = jnp.dot(ca, w_ref[0].astype(BF16), preferred_element_type=F32) + b_ref[0]


def _mod_call(c_pad, w, b, tn=1024):
    L, D, N = w.shape
    return pl.pallas_call(
        _mod_body,
        grid=(L, N // tn),
        in_specs=[pl.BlockSpec((SUBLANES_F32, D), lambda l, n: (0, 0)),
                  pl.BlockSpec((1, D, tn), lambda l, n: (l, 0, n)),
                  pl.BlockSpec((1, 1, tn), lambda l, n: (l, 0, n))],
        out_specs=pl.BlockSpec((1, SUBLANES_F32, tn), lambda l, n: (l, 0, n)),
        out_shape=jax.ShapeDtypeStruct((L, SUBLANES_F32, N), F32),
        compiler_params=_params(32, 2),
        name="mod",
    )(c_pad, w, b)


def _qkv_body(x_ref, vec_ref, w_ref, qT_ref, k_ref, vT_ref, hn_ref, *, score_scale):
    vec = vec_ref[0]
    hn_ref[...] = _norm_mod(x_ref[0], vec[0:1], vec[1:2], vec[2:3]).astype(BF16)
    width = HEAD_PAIRS * LANES
    for c in range(HEAD_PAIRS // 2):
        def proj(base):
            cols = slice(base + c * MXU_COLS, base + (c + 1) * MXU_COLS)
            return jnp.dot(hn_ref[...], w_ref[:, cols], preferred_element_type=F32)
        q = proj(0) * score_scale
        k = proj(width)
        v = proj(2 * width)
        for half in range(2):
            g = slice(half * LANES, (half + 1) * LANES)
            qT_ref[0, 2 * c + half] = q[:, g].T.astype(BF16)
            k_ref[0, 2 * c + half] = k[:, g].astype(BF16)
            vT_ref[0, 2 * c + half] = v[:, g].T.astype(BF16)


def _qkv_call(x, vec, w, score_scale):
    B, S, D = x.shape
    tm = PROJ_TILE
    rows = pl.BlockSpec((1, HEAD_PAIRS, tm, LANES), lambda b, i: (b, 0, i, 0))
    cols = pl.BlockSpec((1, HEAD_PAIRS, LANES, tm), lambda b, i: (b, 0, 0, i))
    return pl.pallas_call(
        functools.partial(_qkv_body, score_scale=score_scale),
        grid=(B, S // tm),
        in_specs=[pl.BlockSpec((1, tm, D), lambda b, i: (b, i, 0)),
                  pl.BlockSpec((1, SUBLANES_F32, D), lambda b, i: (b, 0, 0)),
                  _resident(w.shape)],
        out_specs=[cols, rows, cols],
        out_shape=[jax.ShapeDtypeStruct((B, HEAD_PAIRS, LANES, S), BF16),
                   jax.ShapeDtypeStruct((B, HEAD_PAIRS, S, LANES), BF16),
                   jax.ShapeDtypeStruct((B, HEAD_PAIRS, LANES, S), BF16)],
        scratch_shapes=[pltpu.VMEM((tm, D), BF16)],
        compiler_params=_params(40, 2),
        name="qkv",
    )(x, vec, w)


def _chunk_attn_body(qT_ref, kp_ref, kc_ref, vTp_ref, vTc_ref, bias_ref, o_ref, qs_ref, s_ref):
    i = pl.program_id(2)
    n_sub = A_QTILE // A_SUB
    feat = lax.broadcasted_iota(jnp.int32, (LANES, 1), 0)
    qT = qT_ref[0, 0]
    q_even = jnp.where(feat < A_HEAD_DIM, qT, jnp.zeros_like(qT))
    q_odd = jnp.where(feat >= A_HEAD_DIM, qT, jnp.zeros_like(qT))
    for j in range(n_sub):
        cols = slice(j * A_SUB, (j + 1) * A_SUB)
        qs_ref[j, :, :A_SUB] = q_even[:, cols]
        qs_ref[j, :, A_SUB:] = q_odd[:, cols]

    def window(prev_ref, cur_ref, lo, axis):
        idx = lambda s: (0, 0, s, slice(None)) if axis == 0 else (0, 0, slice(None), s)
        if lo >= A_PREV:
            return cur_ref[idx(slice(lo - A_PREV, lo + A_SUB))]
        return jnp.concatenate([prev_ref[idx(slice(lo, A_PREV))],
                                cur_ref[idx(slice(0, lo + A_SUB))]], axis=axis)

    def scores(j):
        k2 = window(kp_ref, kc_ref, j * A_SUB, 0)
        half = A_WIN // 2
        s_ref[j % 2, :half] = jnp.dot(k2[:half], qs_ref[j], preferred_element_type=F32)
        s_ref[j % 2, half:] = jnp.dot(k2[half:], qs_ref[j], preferred_element_type=F32)

    def softmax_pv(j, first_tile):
        lo = j * A_SUB
        sT = s_ref[j % 2] + bias_ref[0]
        if first_tile and lo < A_PREV:
            row = lax.broadcasted_iota(jnp.int32, (A_WIN, 1), 0)
            sT = jnp.where(row >= A_PREV - lo, sT, NEG_INF)
        m = jnp.max(sT, axis=0, keepdims=True)
        pT = jnp.exp2(sT - m)
        l = jnp.sum(pT, axis=0, keepdims=True)
        pTb = pT.astype(BF16)
        vT = window(vTp_ref, vTc_ref, lo, 1)
        pv = jnp.concatenate(
            [jnp.dot(vT[:A_HEAD_DIM], pTb[:, :A_SUB], preferred_element_type=F32),
             jnp.dot(vT[A_HEAD_DIM:], pTb[:, A_SUB:], preferred_element_type=F32)], axis=1)
        oT = pv / l
        o_ref[0, 0, lo:lo + A_SUB, :] = jnp.concatenate(
            [oT[:, :A_SUB], oT[:, A_SUB:]], axis=0).T.astype(BF16)

    def tile(first_tile):
        scores(0)
        for j in range(n_sub):
            if j + 1 < n_sub:
                scores(j + 1)
            softmax_pv(j, first_tile)

    @pl.when(i == 0)
    def _():
        tile(True)

    @pl.when(i > 0)
    def _():
        tile(False)


def _chunk_attn_call(qT, k, vT, bias):
    B, _, _, S = qT.shape
    tq = A_QTILE
    ratio = tq // A_PREV
    prev_idx = lambda i: jnp.maximum(i * ratio - 1, 0)
    rows = lambda n: (1, 1, n, LANES)
    cols = lambda n: (1, 1, LANES, n)
    return pl.pallas_call(
        _chunk_attn_body,
        grid=(HEAD_PAIRS, B, S // tq),
        in_specs=[pl.BlockSpec(cols(tq), lambda p, b, i: (b, p, 0, i)),
                  pl.BlockSpec(rows(A_PREV), lambda p, b, i: (b, p, prev_idx(i), 0)),
                  pl.BlockSpec(rows(tq), lambda p, b, i: (b, p, i, 0)),
                  pl.BlockSpec(cols(A_PREV), lambda p, b, i: (b, p, 0, prev_idx(i))),
                  pl.BlockSpec(cols(tq), lambda p, b, i: (b, p, 0, i)),
                  pl.BlockSpec((1, A_WIN, 2 * A_SUB), lambda p, b, i: (p, 0, 0))],
        out_specs=pl.BlockSpec(rows(tq), lambda p, b, i: (b, p, i, 0)),
        out_shape=jax.ShapeDtypeStruct((B, HEAD_PAIRS, S, LANES), BF16),
        scratch_shapes=[pltpu.VMEM((tq // A_SUB, LANES, 2 * A_SUB), BF16),
                        pltpu.VMEM((2, A_WIN, 2 * A_SUB), F32)],
        compiler_params=_params(32, 3),
        name="chunk_attn",
    )(qT, k, k, vT, vT, bias)


def _ffn_body(o_ref, oh_ref, h_ref, hh_ref, vec_ref, wo_ref, win_ref, cw_ref, wout_ref,
              out_ref, hn_ref, h1_ref, acc_ref, u_ref, *, final_norm):
    i = pl.program_id(1)
    tm = h_ref.shape[1]
    vec = vec_ref[0]
    g1, n2g, sh2, sc2, g2, fg = (vec[r:r + 1] for r in range(6))

    o_cat = jnp.concatenate(
        [jnp.concatenate([oh_ref[0, p] for p in range(HEAD_PAIRS)], axis=1),
         jnp.concatenate([o_ref[0, p] for p in range(HEAD_PAIRS)], axis=1)], axis=0)
    h_cat = jnp.concatenate([hh_ref[0], h_ref[0]], axis=0)
    h1 = h_cat + g1 * jnp.dot(o_cat, wo_ref[...], preferred_element_type=F32)
    hn = _norm_mod(h1, n2g, sh2, sc2)
    row = lax.broadcasted_iota(jnp.int32, (tm + HALO, 1), 0)
    hn = jnp.where(jnp.logical_and(row < HALO, i == 0), 0.0, hn)
    hn_ref[...] = hn.astype(BF16)
    h1_ref[...] = h1[HALO:]

    ffn_dim = wout_ref.shape[0]
    groups = FFN_CHUNK // LANES

    def slabs(f, base):
        return (f % 2) * 2 * groups + (groups if base >= ffn_dim else 0)

    def up(f, base):
        u = jnp.dot(hn_ref[...], win_ref[:, base:base + FFN_CHUNK], preferred_element_type=F32)
        for g in range(groups):
            u_ref[slabs(f, base) + g] = u[:, g * LANES:(g + 1) * LANES]

    def conv(f, base):
        outs = []
        for g in range(groups):
            cols = slice(base + g * LANES, base + (g + 1) * LANES)
            slab = slabs(f, base) + g
            y = cw_ref[3:4, cols] + u_ref[slab, pl.ds(HALO - 2, tm), :] * cw_ref[0:1, cols]
            y = y + u_ref[slab, pl.ds(HALO - 1, tm), :] * cw_ref[1:2, cols]
            outs.append(y + u_ref[slab, pl.ds(HALO, tm), :] * cw_ref[2:3, cols])
        return jnp.concatenate(outs, axis=1)

    def down(f, act):
        part = jnp.dot(act, wout_ref[f * FFN_CHUNK:(f + 1) * FFN_CHUNK, :],
                       preferred_element_type=F32)
        if f == 0:
            acc_ref[...] = part
        else:
            acc_ref[...] += part

    n_chunks = ffn_dim // FFN_CHUNK
    up(0, 0)
    up(0, ffn_dim)
    act = None
    for f in range(n_chunks):
        if f + 1 < n_chunks:
            up(f + 1, (f + 1) * FFN_CHUNK)
            up(f + 1, ffn_dim + (f + 1) * FFN_CHUNK)
        if f > 0:
            down(f - 1, act)
        act = (_silu(conv(f, f * FFN_CHUNK)) * conv(f, ffn_dim + f * FFN_CHUNK)).astype(BF16)
    down(n_chunks - 1, act)

    h2 = h1_ref[...] + g2 * acc_ref[...]
    out_ref[0] = _rms(h2, fg) if final_norm else h2


def _ffn_call(o, h, vec, wo, win, cw, wout, layer, final_norm):
    B, S, D = h.shape
    tm = ROW_TILE
    halo_blocks = tm // HALO
    halo_idx = lambda i: jnp.maximum(i * halo_blocks - 1, 0)
    return pl.pallas_call(
        functools.partial(_ffn_body, final_norm=final_norm),
        grid=(B, S // tm),
        in_specs=[pl.BlockSpec((1, HEAD_PAIRS, tm, LANES), lambda b, i: (b, 0, i, 0)),
                  pl.BlockSpec((1, HEAD_PAIRS, HALO, LANES), lambda b, i: (b, 0, halo_idx(i), 0)),
                  pl.BlockSpec((1, tm, D), lambda b, i: (b, i, 0)),
                  pl.BlockSpec((1, HALO, D), lambda b, i: (b, halo_idx(i), 0)),
                  pl.BlockSpec((1, SUBLANES_F32, D), lambda b, i: (b, 0, 0)),
                  _resident(wo.shape), _resident_layer(win.shape, layer), _resident(cw.shape),
                  _resident_layer(wout.shape, layer)],
        out_specs=pl.BlockSpec((1, tm, D), lambda b, i: (b, i, 0)),
        out_shape=jax.ShapeDtypeStruct((B, S, D), F32),
        scratch_shapes=[pltpu.VMEM((tm + HALO, D), BF16),
                        pltpu.VMEM((tm, D), F32),
                        pltpu.VMEM((tm, D), F32),
                        pltpu.VMEM((4 * FFN_CHUNK // LANES, tm + HALO, LANES), F32)],
        compiler_params=_params(56, 2),
        name="ffn_final" if final_norm else "ffn",
    )(o, o, h, h, vec, wo, win, cw, wout)


def _kv_body(h_ref, vec_ref, lg_ref, w1_ref, wu_ref, cos_ref, sin_ref, k_ref, v_ref):
    vec = vec_ref[0]
    hn = _norm_mod(h_ref[0], vec[0:1], vec[1:2], vec[2:3]).astype(BF16)
    t = jnp.dot(hn, w1_ref[...], preferred_element_type=F32)
    lat = lg_ref.shape[1]
    ckv = _rms(t[:, :lat], lg_ref[0:1]).astype(BF16)
    cos, sin = _lane_tile(cos_ref[0]), _lane_tile(sin_ref[0])
    kr = (t[:, lat:lat + LANES] * cos + t[:, lat + LANES:] * sin).astype(BF16)
    kv = jnp.dot(ckv, wu_ref[...], preferred_element_type=F32)
    half = HEAD_PAIRS * LANES
    for p in range(HEAD_PAIRS):
        k_ref[0, p, :, :LANES] = kv[:, p * LANES:(p + 1) * LANES].astype(BF16)
        k_ref[0, p, :, LANES:] = kr
        vt = kv[:, half + p * LANES:half + (p + 1) * LANES].T.astype(BF16)
        ones = jnp.ones((MLA_VROWS - B_V_DIM, vt.shape[1]), BF16)
        for hh in range(2):
            v_ref[0, p, hh * MLA_VROWS:hh * MLA_VROWS + B_V_DIM] = vt[hh * B_V_DIM:(hh + 1) * B_V_DIM]
            v_ref[0, p, hh * MLA_VROWS + B_V_DIM:(hh + 1) * MLA_VROWS] = ones


def _kv_call(h, vec, lat_g, w1, wu, cos, sin):
    B, S, D = h.shape
    tm = PROJ_TILE
    row = lambda b, i: (b, i, 0)
    return pl.pallas_call(
        _kv_body,
        grid=(B, S // tm),
        in_specs=[pl.BlockSpec((1, tm, D), row),
                  pl.BlockSpec((1, SUBLANES_F32, D), lambda b, i: (b, 0, 0)),
                  _resident(lat_g.shape), _resident(w1.shape), _resident(wu.shape),
                  pl.BlockSpec((1, tm, B_ROPE_DIM), row), pl.BlockSpec((1, tm, B_ROPE_DIM), row)],
        out_specs=[pl.BlockSpec((1, HEAD_PAIRS, tm, 2 * LANES), lambda b, i: (b, 0, i, 0)),
                   pl.BlockSpec((1, HEAD_PAIRS, 2 * MLA_VROWS, tm), lambda b, i: (b, 0, 0, i))],
        out_shape=[jax.ShapeDtypeStruct((B, HEAD_PAIRS, S, 2 * LANES), BF16),
                   jax.ShapeDtypeStruct((B, HEAD_PAIRS, 2 * MLA_VROWS, S), BF16)],
        compiler_params=_params(48, 2),
        name="shared_kv",
    )(h, vec, lat_g, w1, wu, cos, sin)


def _q_body(h_ref, vec_ref, qg_ref, wdq_ref, wq_ref, cos_ref, sin_ref, q_ref, *, score_scale):
    vec = vec_ref[0]
    hn = _norm_mod(h_ref[0], vec[0:1], vec[1:2], vec[2:3]).astype(BF16)
    cq = _rms(jnp.dot(hn, wdq_ref[...], preferred_element_type=F32), qg_ref[0:1]).astype(BF16)
    cos = _lane_tile(cos_ref[0])
    sin = _lane_tile(sin_ref[0])
    group = 3 * MXU_COLS

    def project(c):
        return jnp.dot(cq, wq_ref[:, c * group:(c + 1) * group], preferred_element_type=F32)

    t = project(0)
    for c in range(HEAD_PAIRS // 2):
        nxt = project(c + 1) if c + 1 < HEAD_PAIRS // 2 else None
        for half in range(2):
            g = slice(half * LANES, (half + 1) * LANES)
            rope = t[:, MXU_COLS:2 * MXU_COLS][:, g] * cos + t[:, 2 * MXU_COLS:][:, g] * sin
            qp = jnp.concatenate([t[:, g], rope], axis=1) * score_scale
            q_ref[0, 2 * c + half] = qp.T.astype(BF16)
        t = nxt


def _q_call(h, vec, q_g, wdq, wq, cos, sin, score_scale):
    B, S, D = h.shape
    tm = PROJ_TILE
    row = lambda b, i: (b, i, 0)
    return pl.pallas_call(
        functools.partial(_q_body, score_scale=score_scale),
        grid=(B, S // tm),
        in_specs=[pl.BlockSpec((1, tm, D), row),
                  pl.BlockSpec((1, SUBLANES_F32, D), lambda b, i: (b, 0, 0)),
                  _resident(q_g.shape), _resident(wdq.shape), _resident(wq.shape),
                  pl.BlockSpec((1, tm, B_ROPE_DIM), row), pl.BlockSpec((1, tm, B_ROPE_DIM), row)],
        out_specs=pl.BlockSpec((1, HEAD_PAIRS, 2 * LANES, tm), lambda b, i: (b, 0, 0, i)),
        out_shape=jax.ShapeDtypeStruct((B, HEAD_PAIRS, 2 * LANES, S), BF16),
        compiler_params=_params(52, 2),
        name="mla_q",
    )(h, vec, q_g, wdq, wq, cos, sin)


def _mla_body(qT_ref, k_ref, vT_ref, mask_ref, o_ref, qs_ref, sa_ref, xa_ref,
              pa_ref, pb_ref, aa_ref, ab_ref, m_ref, jump_ref, acc_ref):
    qi = pl.program_id(2)
    tq = qT_ref.shape[3]
    tk = sa_ref.shape[0]
    n_sub = tq // tk
    width = 2 * tk
    feat = lax.broadcasted_iota(jnp.int32, (2 * LANES, 1), 0)
    in_even = jnp.logical_or(feat < B_NOPE_DIM,
                             jnp.logical_and(feat >= LANES, feat < LANES + B_ROPE_DIM))
    in_odd = jnp.logical_and(jnp.logical_not(in_even), feat < LANES + 2 * B_ROPE_DIM)
    qT = qT_ref[0, 0]
    for t in range(n_sub):
        qt = qT[:, t * tk:(t + 1) * tk]
        qs_ref[:, t * width:t * width + tk] = jnp.where(in_even, qt, jnp.zeros_like(qt))
        qs_ref[:, t * width + tk:(t + 1) * width] = jnp.where(in_odd, qt, jnp.zeros_like(qt))
    n_full = qi * n_sub

    def keys(j):
        return k_ref[0, 0, pl.ds(pl.multiple_of(j * tk, tk), tk), :]

    def values(j):
        return vT_ref[0, 0, :, pl.ds(pl.multiple_of(j * tk, tk), tk)]

    def pv_dots(vT, p_of, subs):
        return jnp.concatenate(
            [jnp.dot(vT[hh * MLA_VROWS:(hh + 1) * MLA_VROWS], p_of(t, hh),
                     preferred_element_type=F32)
             for t in subs for hh in range(2)], axis=1)

    pbuf = ((pa_ref, aa_ref), (pb_ref, ab_ref))

    def probs(j, slot, first_sub=0, diagonal=False):
        p_ref, a_ref = pbuf[slot]
        c0 = first_sub * width
        sT = jnp.dot(keys(j), qs_ref[:, c0:], preferred_element_type=F32)
        if diagonal:
            masked = sT[:, :width] + mask_ref[...]
            sT = masked if first_sub == n_sub - 1 else jnp.concatenate([masked, sT[:, width:]], axis=1)
        m_old = m_ref[:, c0:]
        m_cur = jnp.max(sT, axis=0, keepdims=True)
        p_ref[:, c0:] = jnp.exp2(sT - m_old).astype(BF16)
        m_new = jnp.maximum(m_old, m_cur)
        a_ref[:, c0:] = jnp.exp2(m_old - m_new)
        jump_ref[:, c0:] = jnp.maximum(jump_ref[:, c0:], m_cur - m_old)
        m_ref[:, c0:] = m_new

    def pv(j, slot, first_sub=0):
        p_ref, a_ref = pbuf[slot]
        c0 = first_sub * width
        out = pv_dots(values(jnp.maximum(j, 0)),
                      lambda t, hh: p_ref[:, t * width + hh * tk:t * width + (hh + 1) * tk],
                      range(first_sub, n_sub))
        acc_ref[:, c0:] = (acc_ref[:, c0:] + out) * a_ref[:, c0:]

    acc_ref[...] = jnp.zeros(acc_ref.shape, F32)
    jump_ref[...] = jnp.zeros(jump_ref.shape, F32)
    m_ref[...] = jnp.dot(k_ref[0, 0, :SUBLANES_BF16, :], qs_ref[...],
                         preferred_element_type=F32)[:1]
    pb_ref[...] = jnp.zeros(pb_ref.shape, BF16)
    ab_ref[...] = jnp.ones(ab_ref.shape, F32)

    def four_blocks(u, carry):
        for r in range(4):
            probs(4 * u + r, r % 2)
            pv(4 * u + r - 1, (r + 1) % 2)
        return carry

    lax.fori_loop(0, n_full // 4, four_blocks, 0)
    for r in range(n_sub):
        probs(n_full + r, r % 2, first_sub=r, diagonal=True)
        pv(n_full + r - 1, (r + 1) % 2, first_sub=max(r - 1, 0))
    pv(n_full + n_sub - 1, (n_sub - 1) % 2, first_sub=n_sub - 1)

    def scores(j, first_sub=0):
        c0 = first_sub * width
        sT = jnp.dot(keys(j), qs_ref[:, c0:], preferred_element_type=F32)
        sa_ref[:, c0:] = sT
        xa_ref[:, c0:] = jnp.max(sT, axis=0, keepdims=True)

    def softmax_pv(j, subs, diagonal=False):
        c0, c1 = subs[0] * width, (subs[-1] + 1) * width
        sT = sa_ref[:, c0:c1]
        if diagonal:
            sT = sT + mask_ref[...]
            m_cur = jnp.max(sT, axis=0, keepdims=True)
        else:
            m_cur = xa_ref[:, c0:c1]
        m_old = m_ref[:, c0:c1]
        m_new = jnp.maximum(m_old, m_cur)
        alpha = jnp.exp2(m_old - m_new)
        pTb = jnp.exp2(sT - m_new).astype(BF16)
        out = pv_dots(values(j), lambda t, hh: pTb[:, (t - subs[0]) * width + hh * tk:
                                                   (t - subs[0]) * width + (hh + 1) * tk], subs)
        acc_ref[:, c0:c1] = alpha * acc_ref[:, c0:c1] + out
        m_ref[:, c0:c1] = m_new

    @pl.when(jnp.max(jump_ref[...]) > MLA_MAX_JUMP)
    def _():
        m_ref[...] = jnp.full(m_ref.shape, NEG_INF, F32)
        acc_ref[...] = jnp.zeros(acc_ref.shape, F32)

        def one_block(j, carry):
            scores(j)
            softmax_pv(j, tuple(range(n_sub)))
            return carry

        lax.fori_loop(0, n_full, one_block, 0)
        for r in range(n_sub):
            scores(n_full + r, first_sub=r)
            softmax_pv(n_full + r, (r,), diagonal=True)
            if r + 1 < n_sub:
                softmax_pv(n_full + r, tuple(range(r + 1, n_sub)))

    oT = acc_ref[:B_V_DIM, :] / acc_ref[B_V_DIM:B_V_DIM + 1, :]
    for t in range(n_sub):
        both = jnp.concatenate([oT[:, t * width:t * width + tk],
                                oT[:, t * width + tk:(t + 1) * width]], axis=0)
        o_ref[0, 0, t * tk:(t + 1) * tk, :] = both.T.astype(BF16)


def _mla_call(qT, k, vT, mask):
    B, P, _, S = qT.shape
    tq, tk = MLA_QTILE, MLA_KTILE
    assert tq % (4 * tk) == 0
    return pl.pallas_call(
        _mla_body,
        grid=(B, P, S // tq),
        in_specs=[pl.BlockSpec((1, 1, 2 * LANES, tq), lambda b, p, i: (b, p, 0, i)),
                  pl.BlockSpec((1, 1, S, 2 * LANES), lambda b, p, i: (b, p, 0, 0)),
                  pl.BlockSpec((1, 1, 2 * MLA_VROWS, S), lambda b, p, i: (b, p, 0, 0)),
                  _resident(mask.shape)],
        out_specs=pl.BlockSpec((1, 1, tq, LANES), lambda b, p, i: (b, p, i, 0)),
        out_shape=jax.ShapeDtypeStruct((B, P, S, LANES), BF16),
        scratch_shapes=[pltpu.VMEM((2 * LANES, 2 * tq), BF16),
                        pltpu.VMEM((tk, 2 * tq), F32),
                        pltpu.VMEM((1, 2 * tq), F32),
                        pltpu.VMEM((tk, 2 * tq), BF16),
                        pltpu.VMEM((tk, 2 * tq), BF16),
                        pltpu.VMEM((1, 2 * tq), F32),
                        pltpu.VMEM((1, 2 * tq), F32),
                        pltpu.VMEM((1, 2 * tq), F32),
                        pltpu.VMEM((1, 2 * tq), F32),
                        pltpu.VMEM((MLA_VROWS, 2 * tq), F32)],
        compiler_params=_params(52, 3),
        name="mla_attn",
    )(qT, k, vT, mask)


def _vec_rows(rows, batch, d):
    full = [jnp.broadcast_to(r, (batch, d)) for r in rows]
    full += [jnp.zeros((batch, d), F32)] * (SUBLANES_F32 - len(full))
    return jnp.stack(full, axis=1)


def _pad_rows(rows, width):
    full = [r.reshape(1, width) for r in rows]
    full += [jnp.zeros((1, width), F32)] * (SUBLANES_F32 - len(full))
    return jnp.concatenate(full, axis=0)


def _pair_rope_cols(w, heads):
    kdim = w.shape[0]
    half = B_ROPE_DIM // 2
    wh = w.reshape(kdim, heads // 2, 2 * B_ROPE_DIM)
    pad = jnp.zeros((kdim, heads // 2, LANES - 2 * B_ROPE_DIM), w.dtype)
    plain = jnp.concatenate([wh, pad], axis=-1).reshape(kdim, -1)
    w4 = w.reshape(kdim, heads, 2, half)
    sw = jnp.concatenate([w4[:, :, 1], w4[:, :, 0]], axis=-1).reshape(kdim, heads // 2, 2 * B_ROPE_DIM)
    swapped = jnp.concatenate([sw, pad], axis=-1).reshape(kdim, -1)
    return plain, swapped


def _diagonal_mask():
    kc = np.arange(MLA_KTILE)[:, None] // CHUNK
    qc = (np.arange(2 * MLA_KTILE)[None, :] % MLA_KTILE) // CHUNK
    return jnp.asarray(np.where(kc <= qc, 0.0, NEG_INF), F32)


def _bias_table(rel_bias):
    heads = rel_bias.shape[0]
    n, w = A_SUB, A_WIN
    near, far = n - 1 - A_MAX_REL, w - 1 - A_MAX_REL
    by_dist = jnp.concatenate(
        [jnp.broadcast_to(rel_bias[:, :1], (heads, max(near, 0))),
         rel_bias[:, max(-near, 0):],
         jnp.broadcast_to(rel_bias[:, -1:], (heads, far))], axis=1).astype(F32)
    length = n - 1 + w
    flat = jnp.tile(by_dist[:, ::-1], (1, n))[:, n - 1:n - 1 + n * (length - 1)]
    table = flat.reshape(heads, n, length - 1)[:, :, :w]
    ql = np.arange(n)[:, None]
    kl = np.arange(w)[None, :]
    band = kl // CHUNK - ql // CHUNK
    in_band = np.logical_and(band >= 0, band <= A_LEFT_CHUNKS)
    table = jnp.where(in_band[None], table * math.log2(math.e), NEG_INF)
    return jnp.swapaxes(table, 1, 2).reshape(heads // 2, 2, w, n).transpose(0, 2, 1, 3).reshape(
        heads // 2, w, 2 * n)


def kernel(x, c, positions, mod_w, mod_b, norm1_g, norm2_g, a_wqkv, a_wo, a_rel_bias, kv_mod_w, kv_mod_b, kv_norm_g, b_wdkv, b_kv_lat_norm_g, b_wuk, b_wuv, b_wkr, b_wdq, b_q_norm_g, b_wuq, b_wqr, b_wo, f_win, f_conv_w, f_conv_b, f_wout, final_g):
    B, S, D = x.shape
    depth = mod_w.shape[0]
    ffn_dim = f_wout.shape[1]

    c_pad = jnp.pad(c, ((0, SUBLANES_F32 - B), (0, 0)))
    mod = _mod_call(c_pad, mod_w, mod_b.reshape(depth, 1, 6 * D))[:, :B]
    kv_mod = _mod_call(c_pad, kv_mod_w[None], kv_mod_b.reshape(1, 1, 2 * D))[0, :B]
    mods = [[mod[l, :, k * D:(k + 1) * D] for k in range(6)] for l in range(depth)]

    half = B_ROPE_DIM // 2
    inv_freq = jnp.power(jnp.float32(ROPE_THETA),
                         -jnp.arange(half, dtype=F32) * (2.0 / B_ROPE_DIM))
    ang = positions.astype(F32)[..., None] * inv_freq
    cos, sin = jnp.cos(ang), jnp.sin(ang)
    cos_t = jnp.concatenate([cos, cos], axis=-1)
    sin_t = jnp.concatenate([-sin, sin], axis=-1)

    win_all = f_win.astype(BF16)
    wout_all = f_wout.astype(BF16)

    def conv_rows(l):
        return _pad_rows([f_conv_w[l, t] for t in range(CONV_WIDTH)] + [f_conv_b[l]], 2 * ffn_dim)

    sh1, sc1, g1, sh2, sc2, g2 = mods[0]
    qT, k, vT = _qkv_call(x, _vec_rows([norm1_g[0], sh1, sc1], B, D), a_wqkv[0].astype(BF16),
                          float(A_HEAD_DIM ** -0.5 * math.log2(math.e)))
    o = _chunk_attn_call(qT, k, vT, _bias_table(a_rel_bias[0]))
    h = _ffn_call(o, x, _vec_rows([g1, norm2_g[0], sh2, sc2, g2, final_g], B, D),
                  a_wo[0].astype(BF16), win_all, conv_rows(0), wout_all, 0, final_norm=False)

    kr_plain, kr_swapped = _pair_rope_cols(jnp.concatenate([b_wkr, b_wkr], axis=1), 2)
    w1 = jnp.concatenate([b_wdkv, kr_plain, kr_swapped], axis=1).astype(BF16)
    wu = jnp.concatenate([b_wuk, b_wuv], axis=1).astype(BF16)
    k_cat, v = _kv_call(h, _vec_rows([kv_norm_g, kv_mod[:, :D], kv_mod[:, D:]], B, D),
                        _pad_rows([b_kv_lat_norm_g], b_kv_lat_norm_g.shape[0]), w1, wu,
                        cos_t, sin_t)

    sh1, sc1, g1, sh2, sc2, g2 = mods[1]
    qr_plain, qr_swapped = _pair_rope_cols(b_wqr[0], B_HEADS)
    wq = jnp.stack([b_wuq[0], qr_plain, qr_swapped], axis=1).reshape(
        b_wuq.shape[1], 3, HEAD_PAIRS // 2, MXU_COLS).transpose(0, 2, 1, 3).reshape(
        b_wuq.shape[1], -1).astype(BF16)
    score_scale = float((B_NOPE_DIM + B_ROPE_DIM) ** -0.5 * math.log2(math.e))
    q_cat = _q_call(h, _vec_rows([norm1_g[1], sh1, sc1], B, D),
                    _pad_rows([b_q_norm_g[0]], b_q_norm_g.shape[1]),
                    b_wdq[0].astype(BF16), wq, cos_t, sin_t, score_scale)
    o = _mla_call(q_cat, k_cat, v, _diagonal_mask())
    return _ffn_call(o, h, _vec_rows([g1, norm2_g[1], sh2, sc2, g2, final_g], B, D),
                     b_wo[0].astype(BF16), win_all, conv_rows(1), wout_all, 1, final_norm=True)
```

```python
import functools
import math

import jax
import jax.numpy as jnp
import numpy as np
from jax import lax
from jax.experimental import pallas as pl
from jax.experimental.pallas import tpu as pltpu

F32 = jnp.float32
BF16 = jnp.bfloat16

CHUNK = 64
A_HEADS = 16
A_HEAD_DIM = 64
A_LEFT_CHUNKS = 8
A_MAX_REL = 2 * CHUNK
B_HEADS = 16
B_NOPE_DIM = 64
B_ROPE_DIM = 32
B_V_DIM = 64
ROPE_THETA = 10000.0
CONV_WIDTH = 3
NORM_EPS = 1e-6
NEG_INF = -1e30

LANES = 128
MXU_COLS = 256
SUBLANES_F32 = 8
SUBLANES_BF16 = 16
VMEM_BYTES = 64 * 1024 * 1024

HEAD_PAIRS = 8
HALO = SUBLANES_BF16
ROW_TILE = 1024
PROJ_TILE = 1024
A_QTILE = 2048
A_SUB = 2 * CHUNK
A_PREV = A_LEFT_CHUNKS * CHUNK
A_WIN = A_PREV + A_SUB
MLA_QTILE = 2048
MLA_KTILE = 512
MLA_MAX_JUMP = 64.0
MLA_VROWS = 64 + SUBLANES_BF16
FFN_CHUNK = MXU_COLS


def _params(vmem_mib, n_axes):
    return pltpu.CompilerParams(
        dimension_semantics=("arbitrary",) * n_axes,
        vmem_limit_bytes=vmem_mib * 1024 * 1024)


def _resident(shape):
    zeros = (0,) * len(shape)
    return pl.BlockSpec(shape, lambda *_: zeros, pipeline_mode=pl.Buffered(1))


def _resident_layer(shape, layer):
    return pl.BlockSpec((None,) + tuple(shape[1:]), lambda *_: (layer,) + (0,) * (len(shape) - 1),
                        pipeline_mode=pl.Buffered(1))


def _rms(x, g):
    return x * lax.rsqrt(jnp.mean(x * x, axis=-1, keepdims=True) + NORM_EPS) * g


def _norm_mod(x, g, shift, scale):
    return _rms(x, g) * (1.0 + scale) + shift


def _lane_tile(x):
    return jnp.concatenate([x] * (LANES // x.shape[1]), axis=1)


def _silu(x):
    return x * (1.0 / (1.0 + jnp.exp(-x)))


def _mod_body(c_ref, w_ref, b_ref, o_ref):
    ca = _silu(c_ref[...]).astype(BF16)
    o_ref[0] = jnp.dot(ca, w_ref[0].astype(BF16), preferred_element_type=F32) + b_ref[0]


def _mod_call(c_pad, w, b, tn=1024):
    L, D, N = w.shape
    return pl.pallas_call(
        _mod_body,
        grid=(L, N // tn),
        in_specs=[pl.BlockSpec((SUBLANES_F32, D), lambda l, n: (0, 0)),
                  pl.BlockSpec((1, D, tn), lambda l, n: (l, 0, n)),
                  pl.BlockSpec((1, 1, tn), lambda l, n: (l, 0, n))],
        out_specs=pl.BlockSpec((1, SUBLANES_F32, tn), lambda l, n: (l, 0, n)),
        out_shape=jax.ShapeDtypeStruct((L, SUBLANES_F32, N), F32),
        compiler_params=_params(32, 2),
        name="mod",
    )(c_pad, w, b)


def _qkv_body(x_ref, vec_ref, w_ref, qT_ref, k_ref, vT_ref, hn_ref, *, score_scale):
    vec = vec_ref[0]
    hn_ref[...] = _norm_mod(x_ref[0], vec[0:1], vec[1:2], vec[2:3]).astype(BF16)
    width = HEAD_PAIRS * LANES
    for c in range(HEAD_PAIRS // 2):
        def proj(base):
            cols = slice(base + c * MXU_COLS, base + (c + 1) * MXU_COLS)
            return jnp.dot(hn_ref[...], w_ref[:, cols], preferred_element_type=F32)
        q = proj(0) * score_scale
        k = proj(width)
        v = proj(2 * width)
        for half in range(2):
            g = slice(half * LANES, (half + 1) * LANES)
            qT_ref[0, 2 * c + half] = q[:, g].T.astype(BF16)
            k_ref[0, 2 * c + half] = k[:, g].astype(BF16)
            vT_ref[0, 2 * c + half] = v[:, g].T.astype(BF16)


def _qkv_call(x, vec, w, score_scale):
    B, S, D = x.shape
    tm = PROJ_TILE
    rows = pl.BlockSpec((1, HEAD_PAIRS, tm, LANES), lambda b, i: (b, 0, i, 0))
    cols = pl.BlockSpec((1, HEAD_PAIRS, LANES, tm), lambda b, i: (b, 0, 0, i))
    return pl.pallas_call(
        functools.partial(_qkv_body, score_scale=score_scale),
        grid=(B, S // tm),
        in_specs=[pl.BlockSpec((1, tm, D), lambda b, i: (b, i, 0)),
                  pl.BlockSpec((1, SUBLANES_F32, D), lambda b, i: (b, 0, 0)),
                  _resident(w.shape)],
        out_specs=[cols, rows, cols],
        out_shape=[jax.ShapeDtypeStruct((B, HEAD_PAIRS, LANES, S), BF16),
                   jax.ShapeDtypeStruct((B, HEAD_PAIRS, S, LANES), BF16),
                   jax.ShapeDtypeStruct((B, HEAD_PAIRS, LANES, S), BF16)],
        scratch_shapes=[pltpu.VMEM((tm, D), BF16)],
        compiler_params=_params(40, 2),
        name="qkv",
    )(x, vec, w)


def _chunk_attn_body(qT_ref, kp_ref, kc_ref, vTp_ref, vTc_ref, bias_ref, o_ref, qs_ref, s_ref):
    i = pl.program_id(2)
    n_sub = A_QTILE // A_SUB
    feat = lax.broadcasted_iota(jnp.int32, (LANES, 1), 0)
    qT = qT_ref[0, 0]
    q_even = jnp.where(feat < A_HEAD_DIM, qT, jnp.zeros_like(qT))
    q_odd = jnp.where(feat >= A_HEAD_DIM, qT, jnp.zeros_like(qT))
    for j in range(n_sub):
        cols = slice(j * A_SUB, (j + 1) * A_SUB)
        qs_ref[j, :, :A_SUB] = q_even[:, cols]
        qs_ref[j, :, A_SUB:] = q_odd[:, cols]

    def window(prev_ref, cur_ref, lo, axis):
        idx = lambda s: (0, 0, s, slice(None)) if axis == 0 else (0, 0, slice(None), s)
        if lo >= A_PREV:
            return cur_ref[idx(slice(lo - A_PREV, lo + A_SUB))]
        return jnp.concatenate([prev_ref[idx(slice(lo, A_PREV))],
                                cur_ref[idx(slice(0, lo + A_SUB))]], axis=axis)

    def scores(j):
        k2 = window(kp_ref, kc_ref, j * A_SUB, 0)
        half = A_WIN // 2
        s_ref[j % 2, :half] = jnp.dot(k2[:half], qs_ref[j], preferred_element_type=F32)
        s_ref[j % 2, half:] = jnp.dot(k2[half:], qs_ref[j], preferred_element_type=F32)

    def softmax_pv(j, first_tile):
        lo = j * A_SUB
        sT = s_ref[j % 2] + bias_ref[0]
        if first_tile and lo < A_PREV:
            row = lax.broadcasted_iota(jnp.int32, (A_WIN, 1), 0)
            sT = jnp.where(row >= A_PREV - lo, sT, NEG_INF)
        m = jnp.max(sT, axis=0, keepdims=True)
        pT = jnp.exp2(sT - m)
        l = jnp.sum(pT, axis=0, keepdims=True)
        pTb = pT.astype(BF16)
        vT = window(vTp_ref, vTc_ref, lo, 1)
        pv = jnp.concatenate(
            [jnp.dot(vT[:A_HEAD_DIM], pTb[:, :A_SUB], preferred_element_type=F32),
             jnp.dot(vT[A_HEAD_DIM:], pTb[:, A_SUB:], preferred_element_type=F32)], axis=1)
        oT = pv / l
        o_ref[0, 0, lo:lo + A_SUB, :] = jnp.concatenate(
            [oT[:, :A_SUB], oT[:, A_SUB:]], axis=0).T.astype(BF16)

    def tile(first_tile):
        scores(0)
        for j in range(n_sub):
            if j + 1 < n_sub:
                scores(j + 1)
            softmax_pv(j, first_tile)

    @pl.when(i == 0)
    def _():
        tile(True)

    @pl.when(i > 0)
    def _():
        tile(False)


def _chunk_attn_call(qT, k, vT, bias):
    B, _, _, S = qT.shape
    tq = A_QTILE
    ratio = tq // A_PREV
    prev_idx = lambda i: jnp.maximum(i * ratio - 1, 0)
    rows = lambda n: (1, 1, n, LANES)
    cols = lambda n: (1, 1, LANES, n)
    return pl.pallas_call(
        _chunk_attn_body,
        grid=(HEAD_PAIRS, B, S // tq),
        in_specs=[pl.BlockSpec(cols(tq), lambda p, b, i: (b, p, 0, i)),
                  pl.BlockSpec(rows(A_PREV), lambda p, b, i: (b, p, prev_idx(i), 0)),
                  pl.BlockSpec(rows(tq), lambda p, b, i: (b, p, i, 0)),
                  pl.BlockSpec(cols(A_PREV), lambda p, b, i: (b, p, 0, prev_idx(i))),
                  pl.BlockSpec(cols(tq), lambda p, b, i: (b, p, 0, i)),
                  pl.BlockSpec((1, A_WIN, 2 * A_SUB), lambda p, b, i: (p, 0, 0))],
        out_specs=pl.BlockSpec(rows(tq), lambda p, b, i: (b, p, i, 0)),
        out_shape=jax.ShapeDtypeStruct((B, HEAD_PAIRS, S, LANES), BF16),
        scratch_shapes=[pltpu.VMEM((tq // A_SUB, LANES, 2 * A_SUB), BF16),
                        pltpu.VMEM((2, A_WIN, 2 * A_SUB), F32)],
        compiler_params=_params(32, 3),
        name="chunk_attn",
    )(qT, k, k, vT, vT, bias)


def _ffn_body(o_ref, oh_ref, h_ref, hh_ref, vec_ref, wo_ref, win_ref, cw_ref, wout_ref,
              out_ref, hn_ref, acc_ref, u_ref, *, final_norm):
    i = pl.program_id(1)
    tm = h_ref.shape[1]
    vec = vec_ref[0]
    g1, n2g, sh2, sc2, g2, fg = (vec[r:r + 1] for r in range(6))

    o_cat = jnp.concatenate(
        [jnp.concatenate([oh_ref[0, p] for p in range(HEAD_PAIRS)], axis=1),
         jnp.concatenate([o_ref[0, p] for p in range(HEAD_PAIRS)], axis=1)], axis=0)
    h_cat = jnp.concatenate([hh_ref[0], h_ref[0]], axis=0)
    h1 = h_cat + g1 * jnp.dot(o_cat, wo_ref[...], preferred_element_type=F32)
    hn = _norm_mod(h1, n2g, sh2, sc2)
    row = lax.broadcasted_iota(jnp.int32, (tm + HALO, 1), 0)
    hn = jnp.where(jnp.logical_and(row < HALO, i == 0), 0.0, hn)
    hn_ref[...] = hn.astype(BF16)
    out_ref[0] = h1[HALO:]

    ffn_dim = wout_ref.shape[0]
    groups = FFN_CHUNK // LANES

    def slabs(f, base):
        return (f % 2) * 2 * groups + (groups if base >= ffn_dim else 0)

    def up(f, base):
        u = jnp.dot(hn_ref[...], win_ref[:, base:base + FFN_CHUNK], preferred_element_type=F32)
        for g in range(groups):
            u_ref[slabs(f, base) + g] = u[:, g * LANES:(g + 1) * LANES]

    def conv(f, base):
        outs = []
        for g in range(groups):
            cols = slice(base + g * LANES, base + (g + 1) * LANES)
            slab = slabs(f, base) + g
            y = cw_ref[3:4, cols] + u_ref[slab, pl.ds(HALO - 2, tm), :] * cw_ref[0:1, cols]
            y = y + u_ref[slab, pl.ds(HALO - 1, tm), :] * cw_ref[1:2, cols]
            outs.append(y + u_ref[slab, pl.ds(HALO, tm), :] * cw_ref[2:3, cols])
        return jnp.concatenate(outs, axis=1)

    def down(f, act):
        part = jnp.dot(act, wout_ref[f * FFN_CHUNK:(f + 1) * FFN_CHUNK, :],
                       preferred_element_type=F32)
        if f == 0:
            acc_ref[...] = part
        else:
            acc_ref[...] += part

    n_chunks = ffn_dim // FFN_CHUNK
    up(0, 0)
    up(0, ffn_dim)
    act = None
    for f in range(n_chunks):
        if f + 1 < n_chunks:
            up(f + 1, (f + 1) * FFN_CHUNK)
            up(f + 1, ffn_dim + (f + 1) * FFN_CHUNK)
        if f > 0:
            down(f - 1, act)
        act = (_silu(conv(f, f * FFN_CHUNK)) * conv(f, ffn_dim + f * FFN_CHUNK)).astype(BF16)
    down(n_chunks - 1, act)

    h2 = out_ref[0] + g2 * acc_ref[...]
    out_ref[0] = _rms(h2, fg) if final_norm else h2


def _ffn_call(o, h, vec, wo, win, cw, wout, layer, final_norm):
    B, S, D = h.shape
    tm = ROW_TILE
    halo_blocks = tm // HALO
    halo_idx = lambda i: jnp.maximum(i * halo_blocks - 1, 0)
    return pl.pallas_call(
        functools.partial(_ffn_body, final_norm=final_norm),
        grid=(B, S // tm),
        in_specs=[pl.BlockSpec((1, HEAD_PAIRS, tm, LANES), lambda b, i: (b, 0, i, 0)),
                  pl.BlockSpec((1, HEAD_PAIRS, HALO, LANES), lambda b, i: (b, 0, halo_idx(i), 0)),
                  pl.BlockSpec((1, tm, D), lambda b, i: (b, i, 0)),
                  pl.BlockSpec((1, HALO, D), lambda b, i: (b, halo_idx(i), 0)),
                  pl.BlockSpec((1, SUBLANES_F32, D), lambda b, i: (b, 0, 0)),
                  _resident(wo.shape), _resident_layer(win.shape, layer), _resident(cw.shape),
                  _resident_layer(wout.shape, layer)],
        out_specs=pl.BlockSpec((1, tm, D), lambda b, i: (b, i, 0)),
        out_shape=jax.ShapeDtypeStruct((B, S, D), F32),
        scratch_shapes=[pltpu.VMEM((tm + HALO, D), BF16),
                        pltpu.VMEM((tm, D), F32),
                        pltpu.VMEM((4 * FFN_CHUNK // LANES, tm + HALO, LANES), F32)],
        compiler_params=_params(58, 2),
        name="ffn_final" if final_norm else "ffn",
    )(o, o, h, h, vec, wo, win, cw, wout)


def _kv_body(h_ref, vec_ref, lg_ref, w1_ref, wu_ref, cos_ref, sin_ref, k_ref, v_ref):
    vec = vec_ref[0]
    hn = _norm_mod(h_ref[0], vec[0:1], vec[1:2], vec[2:3]).astype(BF16)
    t = jnp.dot(hn, w1_ref[...], preferred_element_type=F32)
    lat = lg_ref.shape[1]
    ckv = _rms(t[:, :lat], lg_ref[0:1]).astype(BF16)
    cos, sin = _lane_tile(cos_ref[0]), _lane_tile(sin_ref[0])
    kr = (t[:, lat:lat + LANES] * cos + t[:, lat + LANES:] * sin).astype(BF16)
    kv = jnp.dot(ckv, wu_ref[...], preferred_element_type=F32)
    half = HEAD_PAIRS * LANES
    for p in range(HEAD_PAIRS):
        k_ref[0, p, :, :LANES] = kv[:, p * LANES:(p + 1) * LANES].astype(BF16)
        k_ref[0, p, :, LANES:] = kr
        vt = kv[:, half + p * LANES:half + (p + 1) * LANES].T.astype(BF16)
        ones = jnp.ones((MLA_VROWS - B_V_DIM, vt.shape[1]), BF16)
        for hh in range(2):
            v_ref[0, p, hh * MLA_VROWS:hh * MLA_VROWS + B_V_DIM] = vt[hh * B_V_DIM:(hh + 1) * B_V_DIM]
            v_ref[0, p, hh * MLA_VROWS + B_V_DIM:(hh + 1) * MLA_VROWS] = ones


def _kv_call(h, vec, lat_g, w1, wu, cos, sin):
    B, S, D = h.shape
    tm = PROJ_TILE
    row = lambda b, i: (b, i, 0)
    return pl.pallas_call(
        _kv_body,
        grid=(B, S // tm),
        in_specs=[pl.BlockSpec((1, tm, D), row),
                  pl.BlockSpec((1, SUBLANES_F32, D), lambda b, i: (b, 0, 0)),
                  _resident(lat_g.shape), _resident(w1.shape), _resident(wu.shape),
                  pl.BlockSpec((1, tm, B_ROPE_DIM), row), pl.BlockSpec((1, tm, B_ROPE_DIM), row)],
        out_specs=[pl.BlockSpec((1, HEAD_PAIRS, tm, 2 * LANES), lambda b, i: (b, 0, i, 0)),
                   pl.BlockSpec((1, HEAD_PAIRS, 2 * MLA_VROWS, tm), lambda b, i: (b, 0, 0, i))],
        out_shape=[jax.ShapeDtypeStruct((B, HEAD_PAIRS, S, 2 * LANES), BF16),
                   jax.ShapeDtypeStruct((B, HEAD_PAIRS, 2 * MLA_VROWS, S), BF16)],
        compiler_params=_params(48, 2),
        name="shared_kv",
    )(h, vec, lat_g, w1, wu, cos, sin)


def _q_body(h_ref, vec_ref, qg_ref, wdq_ref, wq_ref, cos_ref, sin_ref, q_ref, *, score_scale):
    vec = vec_ref[0]
    hn = _norm_mod(h_ref[0], vec[0:1], vec[1:2], vec[2:3]).astype(BF16)
    cq = _rms(jnp.dot(hn, wdq_ref[...], preferred_element_type=F32), qg_ref[0:1]).astype(BF16)
    cos = _lane_tile(cos_ref[0])
    sin = _lane_tile(sin_ref[0])
    group = 3 * MXU_COLS

    def project(c):
        return jnp.dot(cq, wq_ref[:, c * group:(c + 1) * group], preferred_element_type=F32)

    t = project(0)
    for c in range(HEAD_PAIRS // 2):
        nxt = project(c + 1) if c + 1 < HEAD_PAIRS // 2 else None
        for half in range(2):
            g = slice(half * LANES, (half + 1) * LANES)
            rope = t[:, MXU_COLS:2 * MXU_COLS][:, g] * cos + t[:, 2 * MXU_COLS:][:, g] * sin
            qp = jnp.concatenate([t[:, g], rope], axis=1) * score_scale
            q_ref[0, 2 * c + half] = qp.T.astype(BF16)
        t = nxt


def _q_call(h, vec, q_g, wdq, wq, cos, sin, score_scale):
    B, S, D = h.shape
    tm = PROJ_TILE
    row = lambda b, i: (b, i, 0)
    return pl.pallas_call(
        functools.partial(_q_body, score_scale=score_scale),
        grid=(B, S // tm),
        in_specs=[pl.BlockSpec((1, tm, D), row),
                  pl.BlockSpec((1, SUBLANES_F32, D), lambda b, i: (b, 0, 0)),
                  _resident(q_g.shape), _resident(wdq.shape), _resident(wq.shape),
                  pl.BlockSpec((1, tm, B_ROPE_DIM), row), pl.BlockSpec((1, tm, B_ROPE_DIM), row)],
        out_specs=pl.BlockSpec((1, HEAD_PAIRS, 2 * LANES, tm), lambda b, i: (b, 0, 0, i)),
        out_shape=jax.ShapeDtypeStruct((B, HEAD_PAIRS, 2 * LANES, S), BF16),
        compiler_params=_params(52, 2),
        name="mla_q",
    )(h, vec, q_g, wdq, wq, cos, sin)


def _mla_body(qT_ref, k_ref, vT_ref, mask_ref, o_ref, qs_ref, sa_ref, xa_ref,
              pa_ref, pb_ref, aa_ref, ab_ref, m_ref, jump_ref, acc_ref):
    qi = pl.program_id(2)
    tq = qT_ref.shape[3]
    tk = sa_ref.shape[0]
    n_sub = tq // tk
    width = 2 * tk
    feat = lax.broadcasted_iota(jnp.int32, (2 * LANES, 1), 0)
    in_even = jnp.logical_or(feat < B_NOPE_DIM,
                             jnp.logical_and(feat >= LANES, feat < LANES + B_ROPE_DIM))
    in_odd = jnp.logical_and(jnp.logical_not(in_even), feat < LANES + 2 * B_ROPE_DIM)
    qT = qT_ref[0, 0]
    for t in range(n_sub):
        qt = qT[:, t * tk:(t + 1) * tk]
        qs_ref[:, t * width:t * width + tk] = jnp.where(in_even, qt, jnp.zeros_like(qt))
        qs_ref[:, t * width + tk:(t + 1) * width] = jnp.where(in_odd, qt, jnp.zeros_like(qt))
    n_full = qi * n_sub

    def keys(j):
        return k_ref[0, 0, pl.ds(pl.multiple_of(j * tk, tk), tk), :]

    def values(j):
        return vT_ref[0, 0, :, pl.ds(pl.multiple_of(j * tk, tk), tk)]

    def pv_dots(vT, p_of, subs):
        return jnp.concatenate(
            [jnp.dot(vT[hh * MLA_VROWS:(hh + 1) * MLA_VROWS], p_of(t, hh),
                     preferred_element_type=F32)
             for t in subs for hh in range(2)], axis=1)

    pbuf = ((pa_ref, aa_ref), (pb_ref, ab_ref))

    def probs(j, slot, first_sub=0, diagonal=False):
        p_ref, a_ref = pbuf[slot]
        c0 = first_sub * width
        sT = jnp.dot(keys(j), qs_ref[:, c0:], preferred_element_type=F32)
        if diagonal:
            masked = sT[:, :width] + mask_ref[...]
            sT = masked if first_sub == n_sub - 1 else jnp.concatenate([masked, sT[:, width:]], axis=1)
        m_old = m_ref[:, c0:]
        m_cur = jnp.max(sT, axis=0, keepdims=True)
        p_ref[:, c0:] = jnp.exp2(sT - m_old).astype(BF16)
        m_new = jnp.maximum(m_old, m_cur)
        a_ref[:, c0:] = jnp.exp2(m_old - m_new)
        jump_ref[:, c0:] = jnp.maximum(jump_ref[:, c0:], m_cur - m_old)
        m_ref[:, c0:] = m_new

    def pv(j, slot, first_sub=0):
        p_ref, a_ref = pbuf[slot]
        c0 = first_sub * width
        out = pv_dots(values(jnp.maximum(j, 0)),
                      lambda t, hh: p_ref[:, t * width + hh * tk:t * width + (hh + 1) * tk],
                      range(first_sub, n_sub))
        acc_ref[:, c0:] = (acc_ref[:, c0:] + out) * a_ref[:, c0:]

    acc_ref[...] = jnp.zeros(acc_ref.shape, F32)
    jump_ref[...] = jnp.zeros(jump_ref.shape, F32)
    m_ref[...] = jnp.dot(k_ref[0, 0, :SUBLANES_BF16, :], qs_ref[...],
                         preferred_element_type=F32)[:1]
    pb_ref[...] = jnp.zeros(pb_ref.shape, BF16)
    ab_ref[...] = jnp.ones(ab_ref.shape, F32)

    def four_blocks(u, carry):
        for r in range(4):
            probs(4 * u + r, r % 2)
            pv(4 * u + r - 1, (r + 1) % 2)
        return carry

    lax.fori_loop(0, n_full // 4, four_blocks, 0)
    for r in range(n_sub):
        probs(n_full + r, r % 2, first_sub=r, diagonal=True)
        pv(n_full + r - 1, (r + 1) % 2, first_sub=max(r - 1, 0))
    pv(n_full + n_sub - 1, (n_sub - 1) % 2, first_sub=n_sub - 1)

    def scores(j, first_sub=0):
        c0 = first_sub * width
        sT = jnp.dot(keys(j), qs_ref[:, c0:], preferred_element_type=F32)
        sa_ref[:, c0:] = sT
        xa_ref[:, c0:] = jnp.max(sT, axis=0, keepdims=True)

    def softmax_pv(j, subs, diagonal=False):
        c0, c1 = subs[0] * width, (subs[-1] + 1) * width
        sT = sa_ref[:, c0:c1]
        if diagonal:
            sT = sT + mask_ref[...]
            m_cur = jnp.max(sT, axis=0, keepdims=True)
        else:
            m_cur = xa_ref[:, c0:c1]
        m_old = m_ref[:, c0:c1]
        m_new = jnp.maximum(m_old, m_cur)
        alpha = jnp.exp2(m_old - m_new)
        pTb = jnp.exp2(sT - m_new).astype(BF16)
        out = pv_dots(values(j), lambda t, hh: pTb[:, (t - subs[0]) * width + hh * tk:
                                                   (t - subs[0]) * width + (hh + 1) * tk], subs)
        acc_ref[:, c0:c1] = alpha * acc_ref[:, c0:c1] + out
        m_ref[:, c0:c1] = m_new

    @pl.when(jnp.max(jump_ref[...]) > MLA_MAX_JUMP)
    def _():
        m_ref[...] = jnp.full(m_ref.shape, NEG_INF, F32)
        acc_ref[...] = jnp.zeros(acc_ref.shape, F32)

        def one_block(j, carry):
            scores(j)
            softmax_pv(j, tuple(range(n_sub)))
            return carry

        lax.fori_loop(0, n_full, one_block, 0)
        for r in range(n_sub):
            scores(n_full + r, first_sub=r)
            softmax_pv(n_full + r, (r,), diagonal=True)
            if r + 1 < n_sub:
                softmax_pv(n_full + r, tuple(range(r + 1, n_sub)))

    oT = acc_ref[:B_V_DIM, :] / acc_ref[B_V_DIM:B_V_DIM + 1, :]
    for t in range(n_sub):
        both = jnp.concatenate([oT[:, t * width:t * width + tk],
                                oT[:, t * width + tk:(t + 1) * width]], axis=0)
        o_ref[0, 0, t * tk:(t + 1) * tk, :] = both.T.astype(BF16)


def _mla_call(qT, k, vT, mask):
    B, P, _, S = qT.shape
    tq, tk = MLA_QTILE, MLA_KTILE
    assert tq % (4 * tk) == 0
    return pl.pallas_call(
        _mla_body,
        grid=(B, P, S // tq),
        in_specs=[pl.BlockSpec((1, 1, 2 * LANES, tq), lambda b, p, i: (b, p, 0, i)),
                  pl.BlockSpec((1, 1, S, 2 * LANES), lambda b, p, i: (b, p, 0, 0)),
                  pl.BlockSpec((1, 1, 2 * MLA_VROWS, S), lambda b, p, i: (b, p, 0, 0)),
                  _resident(mask.shape)],
        out_specs=pl.BlockSpec((1, 1, tq, LANES), lambda b, p, i: (b, p, i, 0)),
        out_shape=jax.ShapeDtypeStruct((B, P, S, LANES), BF16),
        scratch_shapes=[pltpu.VMEM((2 * LANES, 2 * tq), BF16),
                        pltpu.VMEM((tk, 2 * tq), F32),
                        pltpu.VMEM((1, 2 * tq), F32),
                        pltpu.VMEM((tk, 2 * tq), BF16),
                        pltpu.VMEM((tk, 2 * tq), BF16),
                        pltpu.VMEM((1, 2 * tq), F32),
                        pltpu.VMEM((1, 2 * tq), F32),
                        pltpu.VMEM((1, 2 * tq), F32),
                        pltpu.VMEM((1, 2 * tq), F32),
                        pltpu.VMEM((MLA_VROWS, 2 * tq), F32)],
        compiler_params=_params(52, 3),
        name="mla_attn",
    )(qT, k, vT, mask)


def _vec_rows(rows, batch, d):
    full = [jnp.broadcast_to(r, (batch, d)) for r in rows]
    full += [jnp.zeros((batch, d), F32)] * (SUBLANES_F32 - len(full))
    return jnp.stack(full, axis=1)


def _pad_rows(rows, width):
    full = [r.reshape(1, width) for r in rows]
    full += [jnp.zeros((1, width), F32)] * (SUBLANES_F32 - len(full))
    return jnp.concatenate(full, axis=0)


def _pair_rope_cols(w, heads):
    kdim = w.shape[0]
    half = B_ROPE_DIM // 2
    wh = w.reshape(kdim, heads // 2, 2 * B_ROPE_DIM)
    pad = jnp.zeros((kdim, heads // 2, LANES - 2 * B_ROPE_DIM), w.dtype)
    plain = jnp.concatenate([wh, pad], axis=-1).reshape(kdim, -1)
    w4 = w.reshape(kdim, heads, 2, half)
    sw = jnp.concatenate([w4[:, :, 1], w4[:, :, 0]], axis=-1).reshape(kdim, heads // 2, 2 * B_ROPE_DIM)
    swapped = jnp.concatenate([sw, pad], axis=-1).reshape(kdim, -1)
    return plain, swapped


def _diagonal_mask():
    kc = np.arange(MLA_KTILE)[:, None] // CHUNK
    qc = (np.arange(2 * MLA_KTILE)[None, :] % MLA_KTILE) // CHUNK
    return jnp.asarray(np.where(kc <= qc, 0.0, NEG_INF), F32)


def _bias_table(rel_bias):
    heads = rel_bias.shape[0]
    n, w = A_SUB, A_WIN
    near, far = n - 1 - A_MAX_REL, w - 1 - A_MAX_REL
    by_dist = jnp.concatenate(
        [jnp.broadcast_to(rel_bias[:, :1], (heads, max(near, 0))),
         rel_bias[:, max(-near, 0):],
         jnp.broadcast_to(rel_bias[:, -1:], (heads, far))], axis=1).astype(F32)
    length = n - 1 + w
    flat = jnp.tile(by_dist[:, ::-1], (1, n))[:, n - 1:n - 1 + n * (length - 1)]
    table = flat.reshape(heads, n, length - 1)[:, :, :w]
    ql = np.arange(n)[:, None]
    kl = np.arange(w)[None, :]
    band = kl // CHUNK - ql // CHUNK
    in_band = np.logical_and(band >= 0, band <= A_LEFT_CHUNKS)
    table = jnp.where(in_band[None], table * math.log2(math.e), NEG_INF)
    return jnp.swapaxes(table, 1, 2).reshape(heads // 2, 2, w, n).transpose(0, 2, 1, 3).reshape(
        heads // 2, w, 2 * n)


def kernel(x, c, positions, mod_w, mod_b, norm1_g, norm2_g, a_wqkv, a_wo, a_rel_bias, kv_mod_w, kv_mod_b, kv_norm_g, b_wdkv, b_kv_lat_norm_g, b_wuk, b_wuv, b_wkr, b_wdq, b_q_norm_g, b_wuq, b_wqr, b_wo, f_win, f_conv_w, f_conv_b, f_wout, final_g):
    B, S, D = x.shape
    depth = mod_w.shape[0]
    ffn_dim = f_wout.shape[1]

    c_pad = jnp.pad(c, ((0, SUBLANES_F32 - B), (0, 0)))
    mod = _mod_call(c_pad, mod_w, mod_b.reshape(depth, 1, 6 * D))[:, :B]
    kv_mod = _mod_call(c_pad, kv_mod_w[None], kv_mod_b.reshape(1, 1, 2 * D))[0, :B]
    mods = [[mod[l, :, k * D:(k + 1) * D] for k in range(6)] for l in range(depth)]

    half = B_ROPE_DIM // 2
    inv_freq = jnp.power(jnp.float32(ROPE_THETA),
                         -jnp.arange(half, dtype=F32) * (2.0 / B_ROPE_DIM))
    ang = positions.astype(F32)[..., None] * inv_freq
    cos, sin = jnp.cos(ang), jnp.sin(ang)
    cos_t = jnp.concatenate([cos, cos], axis=-1)
    sin_t = jnp.concatenate([-sin, sin], axis=-1)

    win_all = f_win.astype(BF16)
    wout_all = f_wout.astype(BF16)

    def conv_rows(l):
        return _pad_rows([f_conv_w[l, t] for t in range(CONV_WIDTH)] + [f_conv_b[l]], 2 * ffn_dim)

    sh1, sc1, g1, sh2, sc2, g2 = mods[0]
    qT, k, vT = _qkv_call(x, _vec_rows([norm1_g[0], sh1, sc1], B, D), a_wqkv[0].astype(BF16),
                          float(A_HEAD_DIM ** -0.5 * math.log2(math.e)))
    o = _chunk_attn_call(qT, k, vT, _bias_table(a_rel_bias[0]))
    h = _ffn_call(o, x, _vec_rows([g1, norm2_g[0], sh2, sc2, g2, final_g], B, D),
                  a_wo[0].astype(BF16), win_all, conv_rows(0), wout_all, 0, final_norm=False)

    kr_plain, kr_swapped = _pair_rope_cols(jnp.concatenate([b_wkr, b_wkr], axis=1), 2)
    w1 = jnp.concatenate([b_wdkv, kr_plain, kr_swapped], axis=1).astype(BF16)
    wu = jnp.concatenate([b_wuk, b_wuv], axis=1).astype(BF16)
    k_cat, v = _kv_call(h, _vec_rows([kv_norm_g, kv_mod[:, :D], kv_mod[:, D:]], B, D),
                        _pad_rows([b_kv_lat_norm_g], b_kv_lat_norm_g.shape[0]), w1, wu,
                        cos_t, sin_t)

    sh1, sc1, g1, sh2, sc2, g2 = mods[1]
    qr_plain, qr_swapped = _pair_rope_cols(b_wqr[0], B_HEADS)
    wq = jnp.stack([b_wuq[0], qr_plain, qr_swapped], axis=1).reshape(
        b_wuq.shape[1], 3, HEAD_PAIRS // 2, MXU_COLS).transpose(0, 2, 1, 3).reshape(
        b_wuq.shape[1], -1).astype(BF16)
    score_scale = float((B_NOPE_DIM + B_ROPE_DIM) ** -0.5 * math.log2(math.e))
    q_cat = _q_call(h, _vec_rows([norm1_g[1], sh1, sc1], B, D),
                    _pad_rows([b_q_norm_g[0]], b_q_norm_g.shape[1]),
                    b_wdq[0].astype(BF16), wq, cos_t, sin_t, score_scale)
    o = _mla_call(q_cat, k_cat, v, _diagonal_mask())
    return _ffn_call(o, h, _vec_rows([g1, norm2_g[1], sh2, sc2, g2, final_g], B, D),
                     b_wo[0].astype(BF16), win_all, conv_rows(1), wout_all, 1, final_norm=True)
```
